```python
import jax
import jax.numpy as jnp
from jax import lax
import numpy as np


D_MODEL = 1024
BATCH = 8
SEQ = 4096
DEPTH = 1

LRU_WIDTH = D_MODEL // 2
LRU_BLOCKS = 8
LRU_BLOCK = LRU_WIDTH // LRU_BLOCKS
CONV_WIDTH = 4
LRU_C = 8.0
NSA_HEADS = 8
HEAD_DIM = 64
NSA_WIDTH = NSA_HEADS * HEAD_DIM
NSA_KV_HEADS = 2
NSA_GROUP = NSA_HEADS // NSA_KV_HEADS
KV_WIDTH = NSA_KV_HEADS * HEAD_DIM
N_BRANCH = 3
CMP_STRIDE = 16
CMP_BLOCK = 2 * CMP_STRIDE
CMP_HIDDEN = 256
SEL_BLOCK = 64
SEL_TOPK = 16
WINDOW = 512
Q_BLOCK = 128
SEL_Q_CHUNK = 16
MIX_WIDTH = LRU_WIDTH + NSA_WIDTH
IN_SPLITS = (LRU_WIDTH, LRU_WIDTH, NSA_WIDTH, KV_WIDTH, KV_WIDTH, KV_WIDTH, KV_WIDTH, KV_WIDTH, KV_WIDTH, N_BRANCH * NSA_HEADS, NSA_WIDTH)
IN_WIDTH = sum(IN_SPLITS)
ATTN_SCALE = HEAD_DIM ** -0.5
NEG_INF = -1e30
FORCE = 1e6
EPS = 1e-6

kernel_name = 'hymba_rglru_nsa_alibi_adaln_block'


def rms_norm(x, g):
    xf = x.astype(jnp.float32)
    y = xf * lax.rsqrt(jnp.mean(xf * xf, axis=-1, keepdims=True) + EPS)
    return (y * g.astype(jnp.float32)).astype(x.dtype)


def alibi_slopes():
    h = jnp.arange(1, NSA_HEADS + 1, dtype=jnp.float32)
    return (2.0 ** (-8.0 * h / NSA_HEADS)).reshape(NSA_KV_HEADS, NSA_GROUP)


def masked_softmax(s, mask):
    return jax.nn.softmax(jnp.where(mask, s, NEG_INF), axis=-1)


def causal_depthwise_conv(x, w, b):
    y = lax.conv_general_dilated(x, w[:, None, :].astype(x.dtype), window_strides=(1,), padding=[(CONV_WIDTH - 1, 0)], dimension_numbers=('NWC', 'WIO', 'NWC'), feature_group_count=x.shape[-1])
    return y + b


def rg_lru(xc, w_a, b_a, w_x, b_x, lam):
    bsz, seq, _ = xc.shape
    xb = xc.reshape(bsz, seq, LRU_BLOCKS, LRU_BLOCK)
    r = jax.nn.sigmoid(jnp.einsum('bsni,nij->bsnj', xb, w_a).reshape(bsz, seq, LRU_WIDTH) + b_a)
    i = jax.nn.sigmoid(jnp.einsum('bsni,nij->bsnj', xb, w_x).reshape(bsz, seq, LRU_WIDTH) + b_x)
    log_a = -LRU_C * r.astype(jnp.float32) * jax.nn.softplus(-lam.astype(jnp.float32))
    a = jnp.exp(log_a)
    u = jnp.sqrt(-jnp.expm1(2.0 * log_a)) * (i * xc).astype(jnp.float32)

    def combine(left, right):
        a_l, h_l = left
        a_r, h_r = right
        return a_l * a_r, a_r * h_l + h_r

    _, h = lax.associative_scan(combine, (a, u), axis=1)
    return h.astype(xc.dtype)


def compress_blocks(kv, pos, w1, w2):
    bsz, seq, hk, dh = kv.shape
    chunks = kv.reshape(bsz, seq // CMP_STRIDE, CMP_STRIDE, hk, dh)
    blocks = jnp.concatenate([chunks[:, :-1], chunks[:, 1:]], axis=2) + pos[None, None, :, None, :]
    flat = jnp.moveaxis(blocks, 3, 2).reshape(bsz, seq // CMP_STRIDE - 1, hk, CMP_BLOCK * dh)
    return jax.nn.gelu(flat @ w1) @ w2


def nsa_branches(q, kc, vc, ks, vs, kw, vw, gate_logits):
    bsz, seq = q.shape[:2]
    n_cmp = kc.shape[1]
    n_sel = seq // SEL_BLOCK
    n_top = min(SEL_TOPK, n_sel)
    slopes = alibi_slopes()
    cmp_end = CMP_STRIDE * jnp.arange(n_cmp) + CMP_BLOCK - 1
    cmp_ids = jnp.arange(n_cmp)
    ratio = SEL_BLOCK // CMP_STRIDE
    overlap = jax.nn.one_hot(cmp_ids // ratio, n_sel, dtype=jnp.float32) + jax.nn.one_hot((cmp_ids + 1) // ratio, n_sel, dtype=jnp.float32)
    sel_ids = jnp.arange(n_sel)
    n_qb = seq // Q_BLOCK

    def cmp_step(i):
        t = i * Q_BLOCK + jnp.arange(Q_BLOCK)
        qb = lax.dynamic_slice_in_dim(q, i * Q_BLOCK, Q_BLOCK, axis=1)
        dist = (t[:, None] - cmp_end[None, :]).astype(jnp.float32)
        mask = dist >= 0
        s = jnp.einsum('bqhgd,bchd->bhgqc', qb, kc).astype(jnp.float32) * ATTN_SCALE - slopes[:, :, None, None] * dist
        p = jnp.where(mask, masked_softmax(s, mask), 0.0)
        o = jnp.einsum('bhgqc,bchd->bqhgd', p.astype(vc.dtype), vc)
        imp = jnp.einsum('bhgqc,cn->bqhn', p, overlap)
        cur = (t // SEL_BLOCK)[:, None, None]
        forced = (sel_ids == 0) | (sel_ids == cur) | (sel_ids == cur - 1)
        imp = jnp.where(forced, FORCE, jnp.where(sel_ids > cur, -FORCE, imp))
        _, idx = lax.top_k(imp, n_top)
        return o, idx

    o_cmp, sel_idx = lax.map(cmp_step, jnp.arange(n_qb))
    o_cmp = jnp.moveaxis(o_cmp, 0, 1).reshape(q.shape)
    sel_idx = jnp.moveaxis(sel_idx, 0, 1).reshape(bsz, seq, NSA_KV_HEADS, n_top)

    ks_b = ks.reshape(bsz, n_sel, SEL_BLOCK, NSA_KV_HEADS, HEAD_DIM).transpose(0, 3, 1, 2, 4)
    vs_b = vs.reshape(bsz, n_sel, SEL_BLOCK, NSA_KV_HEADS, HEAD_DIM).transpose(0, 3, 1, 2, 4)
    b_ix = jnp.arange(bsz)[:, None, None, None]
    h_ix = jnp.arange(NSA_KV_HEADS)[None, None, :, None]
    offs = jnp.arange(SEL_BLOCK)
    n_keys = n_top * SEL_BLOCK

    def slc_step(i):
        t = i * SEL_Q_CHUNK + jnp.arange(SEL_Q_CHUNK)
        qb = lax.dynamic_slice_in_dim(q, i * SEL_Q_CHUNK, SEL_Q_CHUNK, axis=1)
        ib = lax.dynamic_slice_in_dim(sel_idx, i * SEL_Q_CHUNK, SEL_Q_CHUNK, axis=1)
        kg = ks_b[b_ix, h_ix, ib]
        vg = vs_b[b_ix, h_ix, ib]
        kpos = ib[..., None] * SEL_BLOCK + offs
        dist = (t[None, :, None, None, None] - kpos).astype(jnp.float32)
        s = jnp.einsum('bqhgd,bqhnkd->bqhgnk', qb, kg).astype(jnp.float32) * ATTN_SCALE - slopes[None, None, :, :, None, None] * dist[:, :, :, None]
        mask = (dist >= 0)[:, :, :, None].reshape(bsz, SEL_Q_CHUNK, NSA_KV_HEADS, 1, n_keys)
        p = masked_softmax(s.reshape(bsz, SEL_Q_CHUNK, NSA_KV_HEADS, NSA_GROUP, n_keys), mask)
        return jnp.einsum('bqhgm,bqhmd->bqhgd', p.astype(vg.dtype), vg.reshape(bsz, SEL_Q_CHUNK, NSA_KV_HEADS, n_keys, HEAD_DIM))

    o_slc = lax.map(slc_step, jnp.arange(seq // SEL_Q_CHUNK))
    o_slc = jnp.moveaxis(o_slc, 0, 1).reshape(q.shape)

    pad = ((0, 0), (WINDOW, 0), (0, 0), (0, 0))
    kw_p = jnp.pad(kw, pad)
    vw_p = jnp.pad(vw, pad)
    span = WINDOW + Q_BLOCK

    def win_step(i):
        t = i * Q_BLOCK + jnp.arange(Q_BLOCK)
        spos = i * Q_BLOCK - WINDOW + jnp.arange(span)
        qb = lax.dynamic_slice_in_dim(q, i * Q_BLOCK, Q_BLOCK, axis=1)
        kb = lax.dynamic_slice_in_dim(kw_p, i * Q_BLOCK, span, axis=1)
        vb = lax.dynamic_slice_in_dim(vw_p, i * Q_BLOCK, span, axis=1)
        dist = (t[:, None] - spos[None, :]).astype(jnp.float32)
        mask = (dist >= 0) & (dist < WINDOW) & (spos[None, :] >= 0)
        s = jnp.einsum('bqhgd,bkhd->bhgqk', qb, kb).astype(jnp.float32) * ATTN_SCALE - slopes[:, :, None, None] * dist
        p = masked_softmax(s, mask)
        return jnp.einsum('bhgqk,bkhd->bqhgd', p.astype(vb.dtype), vb)

    o_win = lax.map(win_step, jnp.arange(n_qb))
    o_win = jnp.moveaxis(o_win, 0, 1).reshape(q.shape)

    g = jax.nn.sigmoid(gate_logits.astype(jnp.float32)).reshape(bsz, seq, N_BRANCH, NSA_KV_HEADS, NSA_GROUP, 1).astype(q.dtype)
    o = g[:, :, 0] * o_cmp + g[:, :, 1] * o_slc + g[:, :, 2] * o_win
    return o.reshape(bsz, seq, NSA_WIDTH)


def hybrid_layer(x, c, w_ada, b_ada, norm_g, w_in, conv_w, conv_b, w_rg_a, b_rg_a, w_rg_x, b_rg_x, lru_lambda, cmp_pos_k, cmp_w1_k, cmp_w2_k, cmp_pos_v, cmp_w1_v, cmp_w2_v, g_q, g_k_cmp, g_k_slc, g_k_win, g_out_lru, g_out_nsa, w_out):
    bsz, seq, _ = x.shape
    mod = jax.nn.silu(c) @ w_ada + b_ada
    shift, scale, gate = jnp.split(mod, 3, axis=-1)
    h = rms_norm(x, norm_g) * (1.0 + scale[:, None, :]) + shift[:, None, :]
    proj = h @ w_in
    offsets = [int(o) for o in np.cumsum(IN_SPLITS)[:-1]]
    x_lru, z_lru, q, k_cmp, v_cmp, k_slc, v_slc, k_win, v_win, gate_logits, z_nsa = jnp.split(proj, offsets, axis=-1)

    h_lru = rg_lru(causal_depthwise_conv(x_lru, conv_w, conv_b), w_rg_a, b_rg_a, w_rg_x, b_rg_x, lru_lambda)
    y_lru = rms_norm(h_lru, g_out_lru) * jax.nn.silu(z_lru)

    kv_shape = (bsz, seq, NSA_KV_HEADS, HEAD_DIM)
    qh = rms_norm(q.reshape(bsz, seq, NSA_KV_HEADS, NSA_GROUP, HEAD_DIM), g_q)
    kc = rms_norm(compress_blocks(k_cmp.reshape(kv_shape), cmp_pos_k, cmp_w1_k, cmp_w2_k), g_k_cmp)
    vc = compress_blocks(v_cmp.reshape(kv_shape), cmp_pos_v, cmp_w1_v, cmp_w2_v)
    ks = rms_norm(k_slc.reshape(kv_shape), g_k_slc)
    kw = rms_norm(k_win.reshape(kv_shape), g_k_win)
    o_nsa = nsa_branches(qh, kc, vc, ks, v_slc.reshape(kv_shape), kw, v_win.reshape(kv_shape), gate_logits)
    y_nsa = rms_norm(o_nsa, g_out_nsa) * jax.nn.silu(z_nsa)

    out = jnp.concatenate([y_lru, y_nsa], axis=-1) @ w_out
    return x + gate[:, None, :] * out


def setup_inputs(seed: int = 0) -> dict:
    key = jax.random.key(seed)
    ks = jax.random.split(key, 26)
    L = DEPTH

    def nrm(k, shape, s):
        return s * jax.random.normal(k, shape, jnp.float32)

    u = jax.random.uniform(ks[12], (L, LRU_WIDTH), jnp.float32, 0.9, 0.999)
    a0 = u ** (1.0 / LRU_C)
    lam = jnp.log(a0) - jnp.log1p(-a0)
    return {
        'x': nrm(ks[0], (BATCH, SEQ, D_MODEL), 1.0),
        'c': nrm(ks[1], (BATCH, D_MODEL), 1.0),
        'w_ada': nrm(ks[2], (L, D_MODEL, 3 * D_MODEL), 0.5 * D_MODEL ** -0.5),
        'b_ada': nrm(ks[3], (L, 3 * D_MODEL), 0.02),
        'norm_g': 1.0 + nrm(ks[4], (L, D_MODEL), 0.02),
        'w_in': nrm(ks[5], (L, D_MODEL, IN_WIDTH), D_MODEL ** -0.5),
        'conv_w': nrm(ks[6], (L, CONV_WIDTH, LRU_WIDTH), CONV_WIDTH ** -0.5),
        'conv_b': nrm(ks[7], (L, LRU_WIDTH), 0.02),
        'w_rg_a': nrm(ks[8], (L, LRU_BLOCKS, LRU_BLOCK, LRU_BLOCK), LRU_BLOCK ** -0.5),
        'b_rg_a': nrm(ks[9], (L, LRU_WIDTH), 0.02),
        'w_rg_x': nrm(ks[10], (L, LRU_BLOCKS, LRU_BLOCK, LRU_BLOCK), LRU_BLOCK ** -0.5),
        'b_rg_x': nrm(ks[11], (L, LRU_WIDTH), 0.02),
        'lru_lambda': lam,
        'cmp_pos_k': nrm(ks[13], (L, CMP_BLOCK, HEAD_DIM), 0.1),
        'cmp_w1_k': nrm(ks[14], (L, CMP_BLOCK * HEAD_DIM, CMP_HIDDEN), (CMP_BLOCK * HEAD_DIM) ** -0.5),
        'cmp_w2_k': nrm(ks[15], (L, CMP_HIDDEN, HEAD_DIM), CMP_HIDDEN ** -0.5),
        'cmp_pos_v': nrm(ks[16], (L, CMP_BLOCK, HEAD_DIM), 0.1),
        'cmp_w1_v': nrm(ks[17], (L, CMP_BLOCK * HEAD_DIM, CMP_HIDDEN), (CMP_BLOCK * HEAD_DIM) ** -0.5),
        'cmp_w2_v': nrm(ks[18], (L, CMP_HIDDEN, HEAD_DIM), CMP_HIDDEN ** -0.5),
        'g_q': 1.0 + nrm(ks[19], (L, HEAD_DIM), 0.02),
        'g_k_cmp': 1.0 + nrm(ks[20], (L, HEAD_DIM), 0.02),
        'g_k_slc': 1.0 + nrm(ks[21], (L, HEAD_DIM), 0.02),
        'g_k_win': 1.0 + nrm(ks[22], (L, HEAD_DIM), 0.02),
        'g_out_lru': 1.0 + nrm(ks[23], (L, LRU_WIDTH), 0.02),
        'g_out_nsa': 1.0 + nrm(ks[24], (L, NSA_WIDTH), 0.02),
        'w_out': nrm(ks[25], (L, MIX_WIDTH, D_MODEL), MIX_WIDTH ** -0.5),
    }


def reference(x, c, w_ada, b_ada, norm_g, w_in, conv_w, conv_b, w_rg_a, b_rg_a, w_rg_x, b_rg_x, lru_lambda, cmp_pos_k, cmp_w1_k, cmp_w2_k, cmp_pos_v, cmp_w1_v, cmp_w2_v, g_q, g_k_cmp, g_k_slc, g_k_win, g_out_lru, g_out_nsa, w_out):
    for layer in range(DEPTH):
        x = hybrid_layer(x, c, w_ada[layer], b_ada[layer], norm_g[layer], w_in[layer], conv_w[layer], conv_b[layer], w_rg_a[layer], b_rg_a[layer], w_rg_x[layer], b_rg_x[layer], lru_lambda[layer], cmp_pos_k[layer], cmp_w1_k[layer], cmp_w2_k[layer], cmp_pos_v[layer], cmp_w1_v[layer], cmp_w2_v[layer], g_q[layer], g_k_cmp[layer], g_k_slc[layer], g_k_win[layer], g_out_lru[layer], g_out_nsa[layer], w_out[layer])
    return x
```

```python
import functools

import numpy as np
import jax
import jax.numpy as jnp
from jax import lax
from jax.experimental import pallas as pl
from jax.experimental.pallas import tpu as pltpu

F32 = jnp.float32
BF16 = jnp.bfloat16
HIGHEST = lax.Precision.HIGHEST

LANES = 128
HALF = LANES // 2

LRU_BLOCKS = 8
CONV_WIDTH = 4
LRU_C = 8.0
NSA_HEADS = 8
HEAD_DIM = 64
NSA_KV_HEADS = 2
NSA_GROUP = NSA_HEADS // NSA_KV_HEADS
N_BRANCH = 3
CMP_STRIDE = 16
CMP_BLOCK = 2 * CMP_STRIDE
SEL_BLOCK = 64
SEL_SHIFT = SEL_BLOCK.bit_length() - 1
SEL_TOPK = 16
WINDOW = 512
ATTN_SCALE = HEAD_DIM ** -0.5
NEG_INF = -1e30
FORCE = 1e6
EPS = 1e-6

PROJ_ROWS = 512
ATTN_ROWS = 128
SLC_KEYS = 512
WIN_KEYS = 128
VMEM_LIMIT = 56 * 1024 * 1024

assert HEAD_DIM == HALF and NSA_KV_HEADS == 2


def _alibi_slope(head):
    return float(2.0 ** (-8.0 * (head + 1) / NSA_HEADS))


def _nt_dot(a, b):
    return lax.dot_general(a, b, (((1,), (1,)), ((), ())), preferred_element_type=F32)


def _half_rms(x, gain2):
    lane = lax.broadcasted_iota(jnp.int32, (1, LANES), 1)
    lo = lane < HALF
    sq = x * x
    ss_lo = jnp.sum(jnp.where(lo, sq, 0.0), axis=-1, keepdims=True)
    ss_hi = jnp.sum(jnp.where(lo, 0.0, sq), axis=-1, keepdims=True)
    r = jnp.where(lo, lax.rsqrt(ss_lo * (1.0 / HALF) + EPS), lax.rsqrt(ss_hi * (1.0 / HALF) + EPS))
    return x * r * gain2


def _mod_kernel(c_ref, w_ref, b_ref, o_ref):
    c = c_ref[...]
    o_ref[...] = jnp.dot(c * jax.nn.sigmoid(c), w_ref[...], preferred_element_type=F32, precision=HIGHEST) + b_ref[...]


def _mod(c, w_ada, b_ada):
    bsz, d = c.shape
    n = w_ada.shape[1]
    return pl.pallas_call(
        _mod_kernel,
        out_shape=jax.ShapeDtypeStruct((bsz, n), F32),
        grid=(n // d,),
        in_specs=[pl.BlockSpec((bsz, d), lambda j: (0, 0)),
                  pl.BlockSpec((d, d), lambda j: (0, j)),
                  pl.BlockSpec((1, d), lambda j: (0, j))],
        out_specs=pl.BlockSpec((bsz, d), lambda j: (0, j)),
        name="adaln_mod",
    )(c, w_ada, b_ada.reshape(1, n))


def _proj_kernel(cols, x_ref, mod_ref, ng_ref, w_ref, gq_ref, gks_ref, gkw_ref,
                 xlru_ref, zlru_ref, q_ref, kcmp_ref, vcmp_ref, kslc_ref, vslc_ref, kwin_ref, vwin_ref,
                 gate_ref, znsa_ref):
    rows = x_ref.shape[1]
    x = x_ref[0]
    ms = jnp.mean(x * x, axis=-1, keepdims=True)
    y = x * lax.rsqrt(ms + EPS) * ng_ref[...]
    h = y * (1.0 + mod_ref[0, 1:2, :]) + mod_ref[0, 0:1, :]
    hb = h.astype(BF16)

    def mm(name):
        c0, n = cols[name]
        return jnp.dot(hb, w_ref[:, c0:c0 + n], preferred_element_type=F32)

    xlru_ref[0] = mm("xlru")
    zlru_ref[0] = mm("zlru")
    znsa_ref[0] = mm("znsa")
    gate_ref[0] = jax.nn.sigmoid(mm("gates"))
    kcmp_ref[0] = mm("kcmp").astype(BF16)
    vcmp_ref[0] = mm("vcmp").astype(BF16)
    vslc_ref[0] = mm("vslc").astype(BF16)
    vwin_ref[0] = mm("vwin").astype(BF16)
    kwin_ref[0] = _half_rms(mm("kwin"), gkw_ref[...]).astype(BF16)

    qp = mm("q")
    gq = gq_ref[...] * ATTN_SCALE
    for p in range(NSA_GROUP):
        q_ref[0, :, p * LANES:(p + 1) * LANES] = _half_rms(qp[:, p * LANES:(p + 1) * LANES], gq).astype(BF16)

    kn = _half_rms(mm("kslc"), gks_ref[...])
    lane = lax.broadcasted_iota(jnp.int32, (1, LANES), 1)
    lo = lane < HALF
    tok = pl.program_id(1) * rows + lax.broadcasted_iota(jnp.int32, (rows, 1), 0)
    blk = jnp.right_shift(tok, SEL_SHIFT)
    kslc_ref[0, 0] = jnp.where(lo, kn, jnp.where(lane - HALF == blk, 1.0, 0.0)).astype(BF16)
    kslc_ref[0, 1] = jnp.where(lo, jnp.where(lane == blk, 1.0, 0.0), kn).astype(BF16)


def _proj(x, mod3, norm_g, w_big, cols, gq2, gks2, gkw2):
    bsz, seq, d = x.shape
    ts = PROJ_ROWS
    ncol = w_big.shape[1]
    nsa_w = NSA_HEADS * HEAD_DIM
    lru_w = cols["xlru"][1]

    def tok_spec(n):
        return pl.BlockSpec((1, ts, n), lambda b, s: (b, s, 0))

    def const_spec(shape):
        return pl.BlockSpec(shape, lambda b, s: (0,) * len(shape))

    out_shape = (
        jax.ShapeDtypeStruct((bsz, seq, lru_w), F32),
        jax.ShapeDtypeStruct((bsz, seq, lru_w), F32),
        jax.ShapeDtypeStruct((bsz, seq, nsa_w), BF16),
        jax.ShapeDtypeStruct((bsz, seq, LANES), BF16),
        jax.ShapeDtypeStruct((bsz, seq, LANES), BF16),
        jax.ShapeDtypeStruct((bsz, 2, seq, LANES), BF16),
        jax.ShapeDtypeStruct((bsz, seq, LANES), BF16),
        jax.ShapeDtypeStruct((bsz, seq, LANES), BF16),
        jax.ShapeDtypeStruct((bsz, seq, LANES), BF16),
        jax.ShapeDtypeStruct((bsz, seq, LANES), F32),
        jax.ShapeDtypeStruct((bsz, seq, nsa_w), F32),
    )
    out_specs = (
        tok_spec(lru_w), tok_spec(lru_w), tok_spec(nsa_w), tok_spec(LANES), tok_spec(LANES),
        pl.BlockSpec((1, 2, ts, LANES), lambda b, s: (b, 0, s, 0)),
        tok_spec(LANES), tok_spec(LANES), tok_spec(LANES), tok_spec(LANES), tok_spec(nsa_w),
    )
    return pl.pallas_call(
        functools.partial(_proj_kernel, cols),
        out_shape=out_shape,
        grid=(bsz, seq // ts),
        in_specs=[tok_spec(d),
                  pl.BlockSpec((1, 3, d), lambda b, s: (b, 0, 0)),
                  const_spec((1, d)),
                  const_spec((d, ncol)),
                  const_spec((1, LANES)), const_spec((1, LANES)), const_spec((1, LANES))],
        out_specs=out_specs,
        compiler_params=pltpu.CompilerParams(dimension_semantics=("parallel", "arbitrary"),
                                             vmem_limit_bytes=VMEM_LIMIT),
        name="in_proj",
    )(x, mod3, norm_g, w_big, gq2, gks2, gkw2)


def _log1p(y):
    w = 1.0 + y
    return jnp.where(w == 1.0, y, jnp.log(w) * (y / (w - 1.0)))


def _softplus(x):
    return jnp.maximum(x, 0.0) + _log1p(jnp.exp(-jnp.abs(x)))


def _lru_kernel(x_ref, z_ref, cw_ref, cb_ref, wa_ref, ba_ref, wx_ref, bx_ref, lam_ref, g_ref,
                o_ref, xbuf_ref, h_ref):
    rows = x_ref.shape[1]
    width = x_ref.shape[2]
    pad = 8
    first = pl.program_id(1) == 0

    @pl.when(first)
    def _():
        xbuf_ref[0:pad, :] = jnp.zeros((pad, width), F32)
        h_ref[...] = jnp.zeros_like(h_ref)

    @pl.when(jnp.logical_not(first))
    def _():
        xbuf_ref[0:pad, :] = xbuf_ref[rows:rows + pad, :]

    xbuf_ref[pad:pad + rows, :] = x_ref[0]

    xc = cb_ref[...]
    for j in range(CONV_WIDTH):
        off = pad - (CONV_WIDTH - 1) + j
        xc = xc + cw_ref[j:j + 1, :] * xbuf_ref[off:off + rows, :]

    xcb = xc.astype(BF16)
    nslot = width // LANES
    ra = jnp.concatenate([jnp.dot(xcb[:, s * LANES:(s + 1) * LANES], wa_ref[s], preferred_element_type=F32)
                          for s in range(nslot)], axis=1)
    ri = jnp.concatenate([jnp.dot(xcb[:, s * LANES:(s + 1) * LANES], wx_ref[s], preferred_element_type=F32)
                          for s in range(nslot)], axis=1)
    r = jax.nn.sigmoid(ra + ba_ref[...])
    i = jax.nn.sigmoid(ri + bx_ref[...])
    log_a = (-LRU_C) * r * _softplus(-lam_ref[...])
    a = jnp.exp(log_a)
    u = jnp.sqrt(-jnp.tanh(log_a) * (a * a + 1.0)) * (i * xc)

    row = lax.broadcasted_iota(jnp.int32, (rows, 1), 0)
    acc_a, acc_h = a, u
    d = 1
    while d < rows:
        keep = row >= d
        sh_a = pltpu.roll(acc_a, d, 0)
        sh_h = pltpu.roll(acc_h, d, 0)
        acc_h = jnp.where(keep, acc_a * sh_h + acc_h, acc_h)
        acc_a = jnp.where(keep, acc_a * sh_a, acc_a)
        d *= 2
    h = acc_h + acc_a * h_ref[0:1, :]
    h_ref[0:1, :] = h[rows - 1:rows, :]

    ms = jnp.mean(h * h, axis=-1, keepdims=True)
    z = z_ref[0]
    o_ref[0] = (h * lax.rsqrt(ms + EPS) * g_ref[...]) * (z * jax.nn.sigmoid(z))


def _lru(xlru, zlru, conv_w, conv_b, wa_bd, b_a, wx_bd, b_x, lam, g_out):
    bsz, seq, width = xlru.shape
    ts = PROJ_ROWS
    nslot = width // LANES

    def tok_spec():
        return pl.BlockSpec((1, ts, width), lambda b, s: (b, s, 0))

    def const_spec(shape):
        return pl.BlockSpec(shape, lambda b, s: (0,) * len(shape))

    return pl.pallas_call(
        _lru_kernel,
        out_shape=jax.ShapeDtypeStruct((bsz, seq, width), F32),
        grid=(bsz, seq // ts),
        in_specs=[tok_spec(), tok_spec(),
                  const_spec((CONV_WIDTH, width)), const_spec((1, width)),
                  const_spec((nslot, LANES, LANES)), const_spec((1, width)),
                  const_spec((nslot, LANES, LANES)), const_spec((1, width)),
                  const_spec((1, width)), const_spec((1, width))],
        out_specs=tok_spec(),
        scratch_shapes=[pltpu.VMEM((ts + 8, width), F32), pltpu.VMEM((8, width), F32)],
        compiler_params=pltpu.CompilerParams(dimension_semantics=("parallel", "arbitrary"),
                                             vmem_limit_bytes=VMEM_LIMIT),
        name="rg_lru",
    )(xlru, zlru, conv_w, conv_b, wa_bd, b_a, wx_bd, b_x, lam, g_out)


def _compress_kernel(normalise, c_ref, w1x_ref, pos_ref, w1_ref, w2lo_ref, w2hi_ref, g_ref, o_ref):
    nblk = c_ref.shape[1]
    hid = w1_ref.shape[1]
    ab = jnp.dot(c_ref[0], w1x_ref[...], preferred_element_type=F32)
    const = jnp.dot(pos_ref[...], w1_ref[...], preferred_element_type=F32, precision=HIGHEST)[0:1, :]
    acts = []
    for hk in range(NSA_KV_HEADS):
        first = ab[:, (2 * hk) * hid:(2 * hk + 1) * hid]
        second = ab[:, (2 * hk + 1) * hid:(2 * hk + 2) * hid]
        hidden = first + pltpu.roll(second, nblk - 1, 0) + const
        acts.append(jax.nn.gelu(hidden).astype(BF16))
    out = (jnp.dot(acts[0], w2lo_ref[...], preferred_element_type=F32)
           + jnp.dot(acts[1], w2hi_ref[...], preferred_element_type=F32))
    if normalise:
        out = _half_rms(out, g_ref[...])
    o_ref[0] = out.astype(BF16)


def _compress(chunks, w1x, pos_flat, w1, w2lo, w2hi, gain2, normalise):
    bsz, nblk, width = chunks.shape
    hid = w1.shape[1]

    def const_spec(shape):
        return pl.BlockSpec(shape, lambda b: (0,) * len(shape))

    return pl.pallas_call(
        functools.partial(_compress_kernel, normalise),
        out_shape=jax.ShapeDtypeStruct((bsz, nblk, LANES), BF16),
        grid=(bsz,),
        in_specs=[pl.BlockSpec((1, nblk, width), lambda b: (b, 0, 0)),
                  const_spec((width, 4 * hid)), const_spec((8, w1.shape[0])), const_spec(w1.shape),
                  const_spec((hid, LANES)), const_spec((hid, LANES)), const_spec((1, LANES))],
        out_specs=pl.BlockSpec((1, nblk, LANES), lambda b: (b, 0, 0)),
        compiler_params=pltpu.CompilerParams(dimension_semantics=("parallel",), vmem_limit_bytes=VMEM_LIMIT),
        name="compress_kv" if not normalise else "compress_k",
    )(chunks, w1x, pos_flat, w1, w2lo, w2hi, gain2)


def _online_update(carry, s, v):
    m, l, acc = carry
    m_new = jnp.maximum(m, jnp.max(s, axis=-1, keepdims=True))
    alpha = jnp.exp(m - m_new)
    p = jnp.exp(s - m_new)
    l = alpha * l + jnp.sum(p, axis=-1, keepdims=True)
    acc = alpha * acc + jnp.dot(p.astype(BF16), v, preferred_element_type=F32)
    return m_new, l, acc


def _attn_kernel(q_ref, kc_ref, vc_ref, ks_ref, vs_ref, kw_ref, vw_ref, gate_ref, ov_ref, rexp_ref,
                 o_ref, impt_ref):
    tq = q_ref.shape[1]
    n_cmp = kc_ref.shape[1]
    n_sel = impt_ref.shape[0]
    qi = pl.program_id(1)
    t0 = qi * tq
    lane = lax.broadcasted_iota(jnp.int32, (1, LANES), 1)
    lo = lane < HALF
    trow = t0 + lax.broadcasted_iota(jnp.int32, (tq, 1), 0)
    trow4 = jnp.concatenate([trow] * NSA_GROUP, axis=0)
    qfull = q_ref[0]

    cidx = lax.broadcasted_iota(jnp.int32, (1, n_cmp), 1)
    cend = CMP_STRIDE * cidx + (CMP_BLOCK - 1)
    cmp_mask = jnp.logical_and(cend <= trow, cidx < n_cmp - 1)
    cmp_rel = (cend - t0).astype(F32)
    row_has_cmp = trow >= CMP_BLOCK - 1
    cur = jnp.right_shift(trow, SEL_SHIFT)
    forced = jnp.logical_or(jnp.logical_or(lane == 0, lane == cur), lane == cur - 1)
    nrow = lax.broadcasted_iota(jnp.int32, (n_sel, 1), 0)

    branch_out = [[None] * NSA_KV_HEADS for _ in range(N_BRANCH)]
    for hk in range(NSA_KV_HEADS):
        mine = lo if hk == 0 else jnp.logical_not(lo)
        qh = [jnp.where(mine, qfull[:, g * LANES:(g + 1) * LANES], jnp.zeros((), BF16)) for g in range(NSA_GROUP)]
        slopes = [_alibi_slope(hk * NSA_GROUP + g) for g in range(NSA_GROUP)]
        slope_col = jnp.concatenate([jnp.full((tq, 1), s, F32) for s in slopes], axis=0)
        q_stack = jnp.concatenate(qh, axis=0)

        s_cmp = _nt_dot(q_stack, kc_ref[0])
        vc = vc_ref[0]
        p_sum = jnp.zeros((tq, n_cmp), F32)
        o_cmp = []
        for g in range(NSA_GROUP):
            s = jnp.where(cmp_mask, s_cmp[g * tq:(g + 1) * tq] + slopes[g] * cmp_rel, NEG_INF)
            e = jnp.exp(s - jnp.max(s, axis=-1, keepdims=True))
            inv = 1.0 / jnp.sum(e, axis=-1, keepdims=True)
            p = jnp.where(row_has_cmp, e * inv, 0.0)
            p_sum = p_sum + p
            o_cmp.append(jnp.dot(p.astype(BF16), vc, preferred_element_type=F32))
        branch_out[0][hk] = o_cmp
        imp = jnp.dot(p_sum, ov_ref[...], preferred_element_type=F32, precision=HIGHEST)
        imp = jnp.where(forced, FORCE, jnp.where(lane > cur, -FORCE, imp))

        imp_t = jnp.transpose(imp)[0:n_sel, :]
        impt_ref[...] = imp_t
        rank = jnp.zeros((n_sel, tq), F32)
        for m in range(n_sel):
            other = impt_ref[m:m + 1, :]
            ahead = jnp.logical_or(other > imp_t, jnp.logical_and(other == imp_t, nrow > m))
            rank = rank + jnp.where(ahead, 1.0, 0.0)
        pen_t = jnp.where(rank < float(SEL_TOPK), 0.0, NEG_INF)
        zeros_t = jnp.zeros((LANES - n_sel, tq), F32)
        pen_t = jnp.concatenate([zeros_t, pen_t] if hk == 0 else [pen_t, zeros_t], axis=0)
        pen = jnp.transpose(pen_t).astype(BF16)
        q_aug = jnp.concatenate([jnp.where(mine, q, pen) for q in qh], axis=0)

        def slc_body(j, carry, hk=hk, q_aug=q_aug, slope_col=slope_col):
            k0 = pl.multiple_of(j * SLC_KEYS, SLC_KEYS)
            k = ks_ref[0, hk, pl.ds(k0, SLC_KEYS), :]
            v = vs_ref[0, pl.ds(k0, SLC_KEYS), :]
            kpos = k0 + lax.broadcasted_iota(jnp.int32, (1, SLC_KEYS), 1)
            s = _nt_dot(q_aug, k) + slope_col * (kpos - t0).astype(F32)
            s = jnp.where(kpos <= trow4, s, NEG_INF)
            return _online_update(carry, s, v)

        init = (jnp.full((NSA_GROUP * tq, 1), NEG_INF, F32), jnp.zeros((NSA_GROUP * tq, 1), F32),
                jnp.zeros((NSA_GROUP * tq, LANES), F32))
        _, l, acc = lax.fori_loop(0, (t0 + tq + SLC_KEYS - 1) // SLC_KEYS, slc_body, init)
        o = acc * (1.0 / l)
        branch_out[1][hk] = [o[g * tq:(g + 1) * tq] for g in range(NSA_GROUP)]

        def win_body(c, carry, q_stack=q_stack, slope_col=slope_col):
            k0 = pl.multiple_of(t0 - c * WIN_KEYS, WIN_KEYS)
            k = kw_ref[0, pl.ds(k0, WIN_KEYS), :]
            v = vw_ref[0, pl.ds(k0, WIN_KEYS), :]
            kpos = k0 + lax.broadcasted_iota(jnp.int32, (1, WIN_KEYS), 1)
            dist = trow4 - kpos
            s = _nt_dot(q_stack, k) + slope_col * (kpos - t0).astype(F32)
            s = jnp.where(jnp.logical_and(dist >= 0, dist < WINDOW), s, NEG_INF)
            return _online_update(carry, s, v)

        n_win = jnp.minimum(qi * (tq // WIN_KEYS), WINDOW // WIN_KEYS) + tq // WIN_KEYS
        _, l, acc = lax.fori_loop(0, n_win, win_body, init)
        o = acc * (1.0 / l)
        branch_out[2][hk] = [o[g * tq:(g + 1) * tq] for g in range(NSA_GROUP)]

    gates = gate_ref[0]
    g_hi = gates.astype(BF16)
    g_lo = (gates - g_hi.astype(F32)).astype(BF16)
    total = jnp.zeros((tq, NSA_GROUP * LANES), F32)
    for br in range(N_BRANCH):
        gexp = (jnp.dot(g_hi, rexp_ref[br], preferred_element_type=F32)
                + jnp.dot(g_lo, rexp_ref[br], preferred_element_type=F32))
        o_pairs = jnp.concatenate([jnp.where(lo, branch_out[br][0][g], branch_out[br][1][g])
                                   for g in range(NSA_GROUP)], axis=1)
        total = total + gexp * o_pairs
    o_ref[0] = total


def _attn(q, kc, vc, kslc, vslc, kwin, vwin, gates, overlap, rexp):
    bsz, seq, nsa_w = q.shape
    tq = ATTN_ROWS
    n_cmp = kc.shape[1]
    n_sel = seq // SEL_BLOCK
    assert tq % WIN_KEYS == 0 and seq % SLC_KEYS == 0 and n_sel <= LANES // 2

    def per_batch(shape):
        return pl.BlockSpec((1,) + shape, lambda b, i: (b,) + (0,) * len(shape))

    def const_spec(shape):
        return pl.BlockSpec(shape, lambda b, i: (0,) * len(shape))

    return pl.pallas_call(
        _attn_kernel,
        out_shape=jax.ShapeDtypeStruct((bsz, seq, nsa_w), F32),
        grid=(bsz, seq // tq),
        in_specs=[pl.BlockSpec((1, tq, nsa_w), lambda b, i: (b, i, 0)),
                  per_batch((n_cmp, LANES)), per_batch((n_cmp, LANES)),
                  per_batch((2, seq, LANES)), per_batch((seq, LANES)),
                  per_batch((seq, LANES)), per_batch((seq, LANES)),
                  pl.BlockSpec((1, tq, LANES), lambda b, i: (b, i, 0)),
                  const_spec(overlap.shape), const_spec(rexp.shape)],
        out_specs=pl.BlockSpec((1, tq, nsa_w), lambda b, i: (b, i, 0)),
        scratch_shapes=[pltpu.VMEM((n_sel, tq), F32)],
        compiler_params=pltpu.CompilerParams(dimension_semantics=("parallel", "arbitrary"),
                                             vmem_limit_bytes=VMEM_LIMIT),
        name="nsa_attn",
    )(q, kc, vc, kslc, vslc, kwin, vwin, gates, overlap, rexp)


def _out_kernel(x_ref, ylru_ref, onsa_ref, znsa_ref, gn_ref, gate_ref, wl_ref, wn_ref, o_ref):
    o = onsa_ref[0]
    ms = jnp.mean(o * o, axis=-1, keepdims=True)
    z = znsa_ref[0]
    y_nsa = (o * lax.rsqrt(ms + EPS) * gn_ref[...]) * (z * jax.nn.sigmoid(z))
    out = (jnp.dot(ylru_ref[0].astype(BF16), wl_ref[...], preferred_element_type=F32)
           + jnp.dot(y_nsa.astype(BF16), wn_ref[...], preferred_element_type=F32))
    o_ref[0] = x_ref[0] + gate_ref[0, 2:3, :] * out


def _out(x, ylru, onsa, znsa, g_nsa, mod3, w_lru, w_nsa):
    bsz, seq, d = x.shape
    ts = PROJ_ROWS

    def tok_spec(n):
        return pl.BlockSpec((1, ts, n), lambda b, s: (b, s, 0))

    def const_spec(shape):
        return pl.BlockSpec(shape, lambda b, s: (0,) * len(shape))

    return pl.pallas_call(
        _out_kernel,
        out_shape=jax.ShapeDtypeStruct((bsz, seq, d), F32),
        grid=(bsz, seq // ts),
        in_specs=[tok_spec(d), tok_spec(ylru.shape[2]), tok_spec(onsa.shape[2]), tok_spec(znsa.shape[2]),
                  const_spec((1, onsa.shape[2])),
                  pl.BlockSpec((1, 3, d), lambda b, s: (b, 0, 0)),
                  const_spec(w_lru.shape), const_spec(w_nsa.shape)],
        out_specs=tok_spec(d),
        compiler_params=pltpu.CompilerParams(dimension_semantics=("parallel", "parallel"),
                                             vmem_limit_bytes=VMEM_LIMIT),
        name="out_proj",
    )(x, ylru, onsa, znsa, g_nsa, mod3, w_lru, w_nsa)


def _pair_perm():
    perm = np.empty(NSA_HEADS * HEAD_DIM, np.int32)
    for p in range(NSA_GROUP):
        for half in range(NSA_KV_HEADS):
            for dd in range(HEAD_DIM):
                perm[p * LANES + half * HEAD_DIM + dd] = (half * NSA_GROUP + p) * HEAD_DIM + dd
    return perm


def _gate_expand():
    r = np.zeros((N_BRANCH, LANES, NSA_GROUP * LANES), np.float32)
    for br in range(N_BRANCH):
        for hk in range(NSA_KV_HEADS):
            for g in range(NSA_GROUP):
                src = br * NSA_HEADS + hk * NSA_GROUP + g
                r[br, src, g * LANES + hk * HEAD_DIM: g * LANES + (hk + 1) * HEAD_DIM] = 1.0
    return r


def _overlap(n_cmp_pad, n_sel):
    ratio = SEL_BLOCK // CMP_STRIDE
    ov = np.zeros((n_cmp_pad, LANES), np.float32)
    for c in range(n_cmp_pad - 1):
        for n in (c // ratio, (c + 1) // ratio):
            if n < n_sel:
                ov[c, n] += 1.0
    return ov


def _block_diag_pairs(w):
    nb, bs, _ = w.shape
    z = jnp.zeros((bs, bs), w.dtype)
    return jnp.stack([jnp.block([[w[2 * s], z], [z, w[2 * s + 1]]]) for s in range(nb // 2)])


def _compress_weights(w1, w2):
    hid = w1.shape[1]
    w1r = w1.reshape(2, CMP_STRIDE, HEAD_DIM, hid)
    parts = []
    for hk in range(NSA_KV_HEADS):
        for half in range(2):
            slot = jnp.zeros((CMP_STRIDE, NSA_KV_HEADS, HEAD_DIM, hid), w1.dtype).at[:, hk].set(w1r[half])
            parts.append(slot.reshape(CMP_STRIDE * LANES, hid))
    w1x = jnp.concatenate(parts, axis=1).astype(BF16)
    zpad = jnp.zeros_like(w2)
    w2lo = jnp.concatenate([w2, zpad], axis=1).astype(BF16)
    w2hi = jnp.concatenate([zpad, w2], axis=1).astype(BF16)
    return w1x, w2lo, w2hi


def _layer(x, c, w_ada, b_ada, norm_g, w_in, conv_w, conv_b, w_rg_a, b_rg_a, w_rg_x, b_rg_x, lru_lambda,
           cmp_pos_k, cmp_w1_k, cmp_w2_k, cmp_pos_v, cmp_w1_v, cmp_w2_v, g_q, g_k_cmp, g_k_slc, g_k_win,
           g_out_lru, g_out_nsa, w_out):
    bsz, seq, d = x.shape
    lru_w = d // 2
    nsa_w = NSA_HEADS * HEAD_DIM
    kv_w = NSA_KV_HEADS * HEAD_DIM
    n_sel = seq // SEL_BLOCK
    perm = _pair_perm()

    splits = (lru_w, lru_w, nsa_w, kv_w, kv_w, kv_w, kv_w, kv_w, kv_w, N_BRANCH * NSA_HEADS, nsa_w)
    offs = np.concatenate([[0], np.cumsum(splits)])
    names = ("xlru", "zlru", "q", "kcmp", "vcmp", "kslc", "vslc", "kwin", "vwin", "gates", "znsa")
    src = {n: np.arange(offs[i], offs[i + 1]) for i, n in enumerate(names)}
    src["q"] = src["q"][perm]
    src["znsa"] = src["znsa"][perm]
    pieces, cols, at = [], {}, 0
    for n in names:
        wpiece = w_in[:, src[n]]
        width = -(-len(src[n]) // LANES) * LANES
        if width != len(src[n]):
            wpiece = jnp.pad(wpiece, ((0, 0), (0, width - len(src[n]))))
        pieces.append(wpiece)
        cols[n] = (at, width)
        at += width
    w_big = jnp.concatenate(pieces, axis=1).astype(BF16)

    def dup(g):
        return jnp.concatenate([g, g]).reshape(1, LANES)

    mod3 = _mod(c, w_ada, b_ada).reshape(bsz, 3, d)
    (xlru, zlru, q, kcmp, vcmp, kslc, vslc, kwin, vwin, gates, znsa) = _proj(
        x, mod3, norm_g.reshape(1, d), w_big, cols, dup(g_q), dup(g_k_slc), dup(g_k_win))

    ylru = _lru(xlru, zlru, conv_w, conv_b.reshape(1, lru_w),
                _block_diag_pairs(w_rg_a).astype(BF16), b_rg_a.reshape(1, lru_w),
                _block_diag_pairs(w_rg_x).astype(BF16), b_rg_x.reshape(1, lru_w),
                lru_lambda.reshape(1, lru_w), g_out_lru.reshape(1, lru_w))

    n_chunk = seq // CMP_STRIDE
    w1x_k, w2lo_k, w2hi_k = _compress_weights(cmp_w1_k, cmp_w2_k)
    w1x_v, w2lo_v, w2hi_v = _compress_weights(cmp_w1_v, cmp_w2_v)

    def pos_rows(pos):
        return jnp.broadcast_to(pos.reshape(1, CMP_BLOCK * HEAD_DIM), (8, CMP_BLOCK * HEAD_DIM))

    kc = _compress(kcmp.reshape(bsz, n_chunk, CMP_STRIDE * LANES), w1x_k, pos_rows(cmp_pos_k), cmp_w1_k,
                   w2lo_k, w2hi_k, dup(g_k_cmp), True)
    vc = _compress(vcmp.reshape(bsz, n_chunk, CMP_STRIDE * LANES), w1x_v, pos_rows(cmp_pos_v), cmp_w1_v,
                   w2lo_v, w2hi_v, dup(g_k_cmp), False)

    onsa = _attn(q, kc, vc, kslc, vslc, kwin, vwin, gates,
                 jnp.asarray(_overlap(n_chunk, n_sel)), jnp.asarray(_gate_expand(), dtype=BF16))

    w_lru = w_out[:lru_w].astype(BF16)
    w_nsa = w_out[lru_w:][perm].astype(BF16)
    return _out(x, ylru, onsa, znsa, g_out_nsa[perm].reshape(1, nsa_w), mod3, w_lru, w_nsa)


def kernel(x, c, w_ada, b_ada, norm_g, w_in, conv_w, conv_b, w_rg_a, b_rg_a, w_rg_x, b_rg_x, lru_lambda, cmp_pos_k, cmp_w1_k, cmp_w2_k, cmp_pos_v, cmp_w1_v, cmp_w2_v, g_q, g_k_cmp, g_k_slc, g_k_win, g_out_lru, g_out_nsa, w_out):
    params = (w_ada, b_ada, norm_g, w_in, conv_w, conv_b, w_rg_a, b_rg_a, w_rg_x, b_rg_x, lru_lambda,
              cmp_pos_k, cmp_w1_k, cmp_w2_k, cmp_pos_v, cmp_w1_v, cmp_w2_v, g_q, g_k_cmp, g_k_slc, g_k_win,
              g_out_lru, g_out_nsa, w_out)
    for layer in range(w_in.shape[0]):
        x = _layer(x, c, *(p[layer] for p in params))
    return x
```

```python
import functools
import math

import ml_dtypes
import numpy as np
import jax
import jax.numpy as jnp
from jax import lax
from jax.experimental import pallas as pl
from jax.experimental.pallas import tpu as pltpu

F32 = jnp.float32
BF16 = jnp.bfloat16
HIGHEST = lax.Precision.HIGHEST

LANES = 128
HALF = LANES // 2

LRU_BLOCKS = 8
CONV_WIDTH = 4
LRU_C = 8.0
NSA_HEADS = 8
HEAD_DIM = 64
NSA_KV_HEADS = 2
NSA_GROUP = NSA_HEADS // NSA_KV_HEADS
N_BRANCH = 3
CMP_STRIDE = 16
CMP_BLOCK = 2 * CMP_STRIDE
SEL_BLOCK = 64
SEL_SHIFT = SEL_BLOCK.bit_length() - 1
SEL_TOPK = 16
WINDOW = 512
ATTN_SCALE = HEAD_DIM ** -0.5
LOG2E = math.log2(math.e)
NEG_INF = -1e30
FORCE = 1e6
EPS = 1e-6

PROJ_ROWS = 512
ATTN_ROWS = 128
SLC_KEYS = 512
KEY_CHUNK = 128
VMEM_LIMIT = 56 * 1024 * 1024

FEAT_ONEHOT = 0
FEAT_BLK = SEL_BLOCK
FEAT_OFF = SEL_BLOCK + 3
FEAT_ONE = SEL_BLOCK + 6
N_SPLIT = 3

assert HEAD_DIM == HALF and NSA_KV_HEADS == 2 and ATTN_ROWS == LANES and KEY_CHUNK == LANES


def _bf16_terms(value):
    terms, rest = [], np.float32(value)
    for _ in range(N_SPLIT):
        t = np.float32(rest).astype(ml_dtypes.bfloat16).astype(np.float32)
        terms.append(float(t))
        rest = np.float32(rest - t)
    return terms


def _alibi_coef(head):
    return float(2.0 ** (-8.0 * (head + 1) / NSA_HEADS)) * LOG2E


def _nt_dot(a, b):
    return lax.dot_general(a, b, (((1,), (1,)), ((), ())), preferred_element_type=F32)


def _half_rms(x, gain2):
    lane = lax.broadcasted_iota(jnp.int32, (1, LANES), 1)
    lo = lane < HALF
    sq = x * x
    ss_lo = jnp.sum(jnp.where(lo, sq, 0.0), axis=-1, keepdims=True)
    ss_hi = jnp.sum(jnp.where(lo, 0.0, sq), axis=-1, keepdims=True)
    r = jnp.where(lo, lax.rsqrt(ss_lo * (1.0 / HALF) + EPS), lax.rsqrt(ss_hi * (1.0 / HALF) + EPS))
    return x * r * gain2


def _key_features(pos, onehot):
    lane = lax.broadcasted_iota(jnp.int32, (1, LANES), 1)
    blk = jnp.right_shift(pos, SEL_SHIFT)
    off = jnp.bitwise_and(pos, SEL_BLOCK - 1)
    f = jnp.where(jnp.logical_and(lane >= FEAT_BLK, lane < FEAT_OFF), blk.astype(F32),
                  jnp.where(jnp.logical_and(lane >= FEAT_OFF, lane < FEAT_ONE), off.astype(F32),
                            jnp.where(lane == FEAT_ONE, 1.0, 0.0)))
    if onehot:
        f = jnp.where(lane < SEL_BLOCK, jnp.where(lane == blk, 1.0, 0.0), f)
    return f


def _mod_kernel(c_ref, w_ref, b_ref, o_ref):
    c = c_ref[...]
    o_ref[...] = jnp.dot(c * jax.nn.sigmoid(c), w_ref[...], preferred_element_type=F32, precision=HIGHEST) + b_ref[...]


def _mod(c, w_ada, b_ada):
    bsz, d = c.shape
    n = w_ada.shape[1]
    return pl.pallas_call(
        _mod_kernel,
        out_shape=jax.ShapeDtypeStruct((bsz, n), F32),
        grid=(n // d,),
        in_specs=[pl.BlockSpec((bsz, d), lambda j: (0, 0)),
                  pl.BlockSpec((d, d), lambda j: (0, j)),
                  pl.BlockSpec((1, d), lambda j: (0, j))],
        out_specs=pl.BlockSpec((bsz, d), lambda j: (0, j)),
        name="adaln_mod",
    )(c, w_ada, b_ada.reshape(1, n))


def _proj_kernel(cols, rows_t, x_ref, mod_ref, ng_ref, w_ref, wt_ref, gq_ref, gks_ref, gkw_ref,
                 xlru_ref, zlru_ref, kcmp_ref, vcmp_ref, kslc_ref, kwin_ref,
                 qt_ref, vst_ref, vwt_ref, gt_ref, znsat_ref):
    rows = x_ref.shape[1]
    x = x_ref[0]
    ms = jnp.mean(x * x, axis=-1, keepdims=True)
    y = x * lax.rsqrt(ms + EPS) * ng_ref[...]
    h = y * (1.0 + mod_ref[0, 1:2, :]) + mod_ref[0, 0:1, :]
    hb = h.astype(BF16)

    def mm(name):
        c0, n = cols[name]
        return jnp.dot(hb, w_ref[:, c0:c0 + n], preferred_element_type=F32)

    def mmt(name):
        r0, n = rows_t[name]
        return _nt_dot(wt_ref[r0:r0 + n, :], hb)

    xlru_ref[0] = mm("xlru")
    zlru_ref[0] = mm("zlru")
    kcmp_ref[0] = mm("kcmp").astype(BF16)
    vcmp_ref[0] = mm("vcmp").astype(BF16)

    pos = pl.program_id(1) * rows + lax.broadcasted_iota(jnp.int32, (rows, 1), 0)
    kslc_ref[0, :, 0:LANES] = _half_rms(mm("kslc"), gks_ref[...]).astype(BF16)
    kslc_ref[0, :, LANES:2 * LANES] = _key_features(pos, True).astype(BF16)
    kwin_ref[0, :, 0:LANES] = _half_rms(mm("kwin"), gkw_ref[...]).astype(BF16)
    kwin_ref[0, :, LANES:2 * LANES] = _key_features(pos, False).astype(BF16)

    qt = mmt("q")
    for hd in range(NSA_HEADS):
        qh = qt[hd * HEAD_DIM:(hd + 1) * HEAD_DIM]
        r = lax.rsqrt(jnp.mean(qh * qh, axis=0, keepdims=True) + EPS)
        qt_ref[0, hd * HEAD_DIM:(hd + 1) * HEAD_DIM, :] = (qh * r * gq_ref[...]).astype(BF16)

    vst = mmt("vslc").astype(BF16)
    vwt = mmt("vwin").astype(BF16)
    for j in range(rows // KEY_CHUNK):
        vst_ref[0, j] = vst[:, j * KEY_CHUNK:(j + 1) * KEY_CHUNK]
        vwt_ref[0, j] = vwt[:, j * KEY_CHUNK:(j + 1) * KEY_CHUNK]
    gt_ref[0] = jax.nn.sigmoid(mmt("gates"))
    znsat_ref[0] = mmt("znsa")


def _proj(x, mod3, norm_g, w_big, cols, w_t, rows_t, gq_b, gks2, gkw2):
    bsz, seq, d = x.shape
    ts = PROJ_ROWS
    nsa_w = NSA_HEADS * HEAD_DIM
    lru_w = cols["xlru"][1]
    n_gate = rows_t["gates"][1]

    def tok_spec(n):
        return pl.BlockSpec((1, ts, n), lambda b, s: (b, s, 0))

    def feat_spec(n):
        return pl.BlockSpec((1, n, ts), lambda b, s: (b, 0, s))

    def chunk_spec():
        return pl.BlockSpec((1, ts // KEY_CHUNK, LANES, KEY_CHUNK), lambda b, s: (b, s, 0, 0))

    def const_spec(shape):
        return pl.BlockSpec(shape, lambda b, s: (0,) * len(shape))

    out_shape = (
        jax.ShapeDtypeStruct((bsz, seq, lru_w), F32),
        jax.ShapeDtypeStruct((bsz, seq, lru_w), F32),
        jax.ShapeDtypeStruct((bsz, seq, LANES), BF16),
        jax.ShapeDtypeStruct((bsz, seq, LANES), BF16),
        jax.ShapeDtypeStruct((bsz, seq, 2 * LANES), BF16),
        jax.ShapeDtypeStruct((bsz, seq, 2 * LANES), BF16),
        jax.ShapeDtypeStruct((bsz, nsa_w, seq), BF16),
        jax.ShapeDtypeStruct((bsz, seq // KEY_CHUNK, LANES, KEY_CHUNK), BF16),
        jax.ShapeDtypeStruct((bsz, seq // KEY_CHUNK, LANES, KEY_CHUNK), BF16),
        jax.ShapeDtypeStruct((bsz, n_gate, seq), F32),
        jax.ShapeDtypeStruct((bsz, nsa_w, seq), F32),
    )
    out_specs = (tok_spec(lru_w), tok_spec(lru_w), tok_spec(LANES), tok_spec(LANES),
                 tok_spec(2 * LANES), tok_spec(2 * LANES),
                 feat_spec(nsa_w), chunk_spec(), chunk_spec(), feat_spec(n_gate), feat_spec(nsa_w))
    return pl.pallas_call(
        functools.partial(_proj_kernel, cols, rows_t),
        out_shape=out_shape,
        grid=(bsz, seq // ts),
        in_specs=[tok_spec(d),
                  pl.BlockSpec((1, 3, d), lambda b, s: (b, 0, 0)),
                  const_spec((1, d)),
                  const_spec(w_big.shape), const_spec(w_t.shape),
                  const_spec((HEAD_DIM, ts)), const_spec((1, LANES)), const_spec((1, LANES))],
        out_specs=out_specs,
        compiler_params=pltpu.CompilerParams(dimension_semantics=("parallel", "arbitrary"),
                                             vmem_limit_bytes=VMEM_LIMIT),
        name="in_proj",
    )(x, mod3, norm_g, w_big, w_t, gq_b, gks2, gkw2)


def _log1p(y):
    w = 1.0 + y
    return jnp.where(w == 1.0, y, jnp.log(w) * (y / (w - 1.0)))


def _softplus(x):
    return jnp.maximum(x, 0.0) + _log1p(jnp.exp(-jnp.abs(x)))


def _lru_kernel(x_ref, z_ref, cw_ref, cb_ref, wa_ref, ba_ref, wx_ref, bx_ref, lam_ref, g_ref,
                o_ref, xbuf_ref, h_ref):
    rows = x_ref.shape[1]
    width = x_ref.shape[2]
    pad = 8
    first = pl.program_id(1) == 0

    @pl.when(first)
    def _():
        xbuf_ref[0:pad, :] = jnp.zeros((pad, width), F32)
        h_ref[...] = jnp.zeros_like(h_ref)

    @pl.when(jnp.logical_not(first))
    def _():
        xbuf_ref[0:pad, :] = xbuf_ref[rows:rows + pad, :]

    xbuf_ref[pad:pad + rows, :] = x_ref[0]

    xc = cb_ref[...]
    for j in range(CONV_WIDTH):
        off = pad - (CONV_WIDTH - 1) + j
        xc = xc + cw_ref[j:j + 1, :] * xbuf_ref[off:off + rows, :]

    xcb = xc.astype(BF16)
    nslot = width // LANES
    ra = jnp.concatenate([jnp.dot(xcb[:, s * LANES:(s + 1) * LANES], wa_ref[s], preferred_element_type=F32)
                          for s in range(nslot)], axis=1)
    ri = jnp.concatenate([jnp.dot(xcb[:, s * LANES:(s + 1) * LANES], wx_ref[s], preferred_element_type=F32)
                          for s in range(nslot)], axis=1)
    r = jax.nn.sigmoid(ra + ba_ref[...])
    i = jax.nn.sigmoid(ri + bx_ref[...])
    log_a = (-LRU_C) * r * _softplus(-lam_ref[...])
    a = jnp.exp(log_a)
    u = jnp.sqrt(-jnp.tanh(log_a) * (a * a + 1.0)) * (i * xc)

    row = lax.broadcasted_iota(jnp.int32, (rows, 1), 0)
    acc_a, acc_h = a, u
    d = 1
    while d < rows:
        keep = row >= d
        sh_a = pltpu.roll(acc_a, d, 0)
        sh_h = pltpu.roll(acc_h, d, 0)
        acc_h = jnp.where(keep, acc_a * sh_h + acc_h, acc_h)
        acc_a = jnp.where(keep, acc_a * sh_a, acc_a)
        d *= 2
    h = acc_h + acc_a * h_ref[0:1, :]
    h_ref[0:1, :] = h[rows - 1:rows, :]

    ms = jnp.mean(h * h, axis=-1, keepdims=True)
    z = z_ref[0]
    o_ref[0] = (h * lax.rsqrt(ms + EPS) * g_ref[...]) * (z * jax.nn.sigmoid(z))


def _lru(xlru, zlru, conv_w, conv_b, wa_bd, b_a, wx_bd, b_x, lam, g_out):
    bsz, seq, width = xlru.shape
    ts = PROJ_ROWS
    nslot = width // LANES

    def tok_spec():
        return pl.BlockSpec((1, ts, width), lambda b, s: (b, s, 0))

    def const_spec(shape):
        return pl.BlockSpec(shape, lambda b, s: (0,) * len(shape))

    return pl.pallas_call(
        _lru_kernel,
        out_shape=jax.ShapeDtypeStruct((bsz, seq, width), F32),
        grid=(bsz, seq // ts),
        in_specs=[tok_spec(), tok_spec(),
                  const_spec((CONV_WIDTH, width)), const_spec((1, width)),
                  const_spec((nslot, LANES, LANES)), const_spec((1, width)),
                  const_spec((nslot, LANES, LANES)), const_spec((1, width)),
                  const_spec((1, width)), const_spec((1, width))],
        out_specs=tok_spec(),
        scratch_shapes=[pltpu.VMEM((ts + 8, width), F32), pltpu.VMEM((8, width), F32)],
        compiler_params=pltpu.CompilerParams(dimension_semantics=("parallel", "arbitrary"),
                                             vmem_limit_bytes=VMEM_LIMIT),
        name="rg_lru",
    )(xlru, zlru, conv_w, conv_b, wa_bd, b_a, wx_bd, b_x, lam, g_out)


def _compress_kernel(is_key, c_ref, w1x_ref, pos_ref, w1_ref, w2lo_ref, w2hi_ref, g_ref, o_ref):
    nblk = c_ref.shape[1]
    hid = w1_ref.shape[1]
    ab = jnp.dot(c_ref[0], w1x_ref[...], preferred_element_type=F32)
    const = jnp.dot(pos_ref[...], w1_ref[...], preferred_element_type=F32, precision=HIGHEST)[0:1, :]
    acts = []
    for hk in range(NSA_KV_HEADS):
        first = ab[:, (2 * hk) * hid:(2 * hk + 1) * hid]
        second = ab[:, (2 * hk + 1) * hid:(2 * hk + 2) * hid]
        hidden = first + pltpu.roll(second, nblk - 1, 0) + const
        acts.append(jax.nn.gelu(hidden).astype(BF16))
    if is_key:
        out = (jnp.dot(acts[0], w2lo_ref[...], preferred_element_type=F32)
               + jnp.dot(acts[1], w2hi_ref[...], preferred_element_type=F32))
        o_ref[0, :, 0:LANES] = _half_rms(out, g_ref[...]).astype(BF16)
        cend = CMP_STRIDE * lax.broadcasted_iota(jnp.int32, (nblk, 1), 0) + (CMP_BLOCK - 1)
        o_ref[0, :, LANES:2 * LANES] = _key_features(cend, False).astype(BF16)
    else:
        out_t = _nt_dot(w2lo_ref[...], acts[0]) + _nt_dot(w2hi_ref[...], acts[1])
        o_ref[0] = out_t.astype(BF16)


def _compress(chunks, w1x, pos_flat, w1, w2lo, w2hi, gain2, is_key):
    bsz, nblk, width = chunks.shape
    hid = w1.shape[1]

    def const_spec(shape):
        return pl.BlockSpec(shape, lambda b: (0,) * len(shape))

    out_tail = (nblk, 2 * LANES) if is_key else (LANES, nblk)
    return pl.pallas_call(
        functools.partial(_compress_kernel, is_key),
        out_shape=jax.ShapeDtypeStruct((bsz,) + out_tail, BF16),
        grid=(bsz,),
        in_specs=[pl.BlockSpec((1, nblk, width), lambda b: (b, 0, 0)),
                  const_spec((width, 4 * hid)), const_spec((8, w1.shape[0])), const_spec(w1.shape),
                  const_spec(w2lo.shape), const_spec(w2hi.shape), const_spec((1, LANES))],
        out_specs=pl.BlockSpec((1,) + out_tail, lambda b: (b, 0, 0)),
        compiler_params=pltpu.CompilerParams(dimension_semantics=("parallel",), vmem_limit_bytes=VMEM_LIMIT),
        name="compress_k" if is_key else "compress_v",
    )(chunks, w1x, pos_flat, w1, w2lo, w2hi, gain2)


def _online_update(carry, s, vt):
    m, l, acc = carry
    m_new = jnp.maximum(m, jnp.max(s, axis=0, keepdims=True))
    alpha = jnp.exp2(m - m_new)
    p = jnp.exp2(s - m_new)
    l = alpha * l + jnp.sum(p, axis=0, keepdims=True)
    acc = alpha * acc + jnp.dot(vt, p.astype(BF16), preferred_element_type=F32)
    return m_new, l, acc


def _attn_kernel(qt_ref, kc_ref, vct_ref, ks_ref, vst_ref, kw_ref, vwt_ref, gt_ref, ovt_ref,
                 o_ref, impt_ref):
    tq = qt_ref.shape[2]
    n_cmp = kc_ref.shape[1]
    n_sel = impt_ref.shape[0]
    ncol = NSA_GROUP * tq
    qi = pl.program_id(1)
    t0 = qi * tq
    tloc = lax.broadcasted_iota(jnp.int32, (1, tq), 1)
    tcol = jnp.concatenate([tloc] * NSA_GROUP, axis=1)
    tpos = t0 + tloc
    nrow = lax.broadcasted_iota(jnp.int32, (n_sel, 1), 0)
    cur = jnp.right_shift(tpos, SEL_SHIFT)
    forced = jnp.logical_or(jnp.logical_or(nrow == 0, nrow == cur), nrow == cur - 1)
    frow = lax.broadcasted_iota(jnp.int32, (HEAD_DIM, 1), 0)
    krow = lax.broadcasted_iota(jnp.int32, (KEY_CHUNK, 1), 0)
    zeros_q = jnp.zeros((HEAD_DIM, tq), BF16)

    cidx = lax.broadcasted_iota(jnp.int32, (n_cmp, 1), 0)
    cend = CMP_STRIDE * cidx + (CMP_BLOCK - 1)
    cmp_ok = jnp.logical_and(cend <= t0 + tcol, cidx < n_cmp - 1)
    col_has_cmp = (t0 + tcol) >= CMP_BLOCK - 1
    gates = gt_ref[0]

    for hk in range(NSA_KV_HEADS):
        heads = [hk * NSA_GROUP + g for g in range(NSA_GROUP)]
        q_rows, alibi_rows = [], []
        for hd in heads:
            qh = qt_ref[0, hd * HEAD_DIM:(hd + 1) * HEAD_DIM, :]
            q_rows.append([qh, zeros_q] if hk == 0 else [zeros_q, qh])
            c = _alibi_coef(hd)
            a = jnp.zeros((HEAD_DIM, 1), F32)
            for i, term in enumerate(_bf16_terms(c)):
                a = jnp.where(frow == FEAT_BLK - SEL_BLOCK + i, SEL_BLOCK * term, a)
                a = jnp.where(frow == FEAT_OFF - SEL_BLOCK + i, term, a)
            a = jnp.where(frow == FEAT_ONE - SEL_BLOCK, -c * t0.astype(F32), a)
            alibi_rows.append(jnp.broadcast_to(a, (HEAD_DIM, tq)).astype(BF16))

        def q_operand(pen):
            return jnp.concatenate([jnp.concatenate(q_rows[g] + [pen, alibi_rows[g]], axis=0)
                                    for g in range(NSA_GROUP)], axis=1)

        q_plain = q_operand(zeros_q)

        s = jnp.dot(kc_ref[0], q_plain, preferred_element_type=F32)
        s = jnp.where(cmp_ok, s, NEG_INF)
        e = jnp.exp2(s - jnp.max(s, axis=0, keepdims=True))
        p = jnp.where(col_has_cmp, e * (1.0 / jnp.sum(e, axis=0, keepdims=True)), 0.0)
        o_cmp = jnp.dot(vct_ref[0, hk * HEAD_DIM:(hk + 1) * HEAD_DIM, :], p.astype(BF16),
                        preferred_element_type=F32)
        p_sum = p[:, 0:tq]
        for g in range(1, NSA_GROUP):
            p_sum = p_sum + p[:, g * tq:(g + 1) * tq]
        imp = jnp.zeros((n_sel, tq), F32)
        rest = p_sum
        for _ in range(N_SPLIT):
            term = rest.astype(BF16)
            imp = imp + jnp.dot(ovt_ref[...], term, preferred_element_type=F32)
            rest = rest - term.astype(F32)
        imp = jnp.where(forced, FORCE, jnp.where(nrow > cur, -FORCE, imp))

        impt_ref[...] = imp
        rank = jnp.zeros((n_sel, tq), F32)
        for m in range(n_sel):
            other = impt_ref[m:m + 1, :]
            ahead = jnp.logical_or(other > imp, jnp.logical_and(other == imp, nrow > m))
            rank = rank + jnp.where(ahead, 1.0, 0.0)
        pen = jnp.where(rank < float(SEL_TOPK), 0.0, NEG_INF).astype(BF16)
        q_sel = q_operand(pen)

        init = (jnp.full((1, ncol), NEG_INF, F32), jnp.zeros((1, ncol), F32), jnp.zeros((HEAD_DIM, ncol), F32))

        def values(ref, first_chunk, n):
            return jnp.concatenate([ref[0, first_chunk + i, hk * HEAD_DIM:(hk + 1) * HEAD_DIM, :]
                                    for i in range(n)], axis=1)

        def slc_wide(j, carry):
            k0 = pl.multiple_of(j * SLC_KEYS, SLC_KEYS)
            s = jnp.dot(ks_ref[0, pl.ds(k0, SLC_KEYS), :], q_sel, preferred_element_type=F32)
            return _online_update(carry, s, values(vst_ref, j * (SLC_KEYS // KEY_CHUNK), SLC_KEYS // KEY_CHUNK))

        def slc_narrow(i, carry):
            c0 = (t0 // SLC_KEYS) * (SLC_KEYS // KEY_CHUNK) + i
            k0 = pl.multiple_of(c0 * KEY_CHUNK, KEY_CHUNK)
            s = jnp.dot(ks_ref[0, pl.ds(k0, KEY_CHUNK), :], q_sel, preferred_element_type=F32)
            return _online_update(carry, s, values(vst_ref, c0, 1))

        carry = lax.fori_loop(0, t0 // SLC_KEYS, slc_wide, init)
        carry = lax.fori_loop(0, (t0 % SLC_KEYS) // KEY_CHUNK, slc_narrow, carry)
        kd = pl.multiple_of(t0, KEY_CHUNK)
        s = jnp.dot(ks_ref[0, pl.ds(kd, KEY_CHUNK), :], q_sel, preferred_element_type=F32)
        s = jnp.where(krow <= tcol, s, NEG_INF)
        _, l, acc = _online_update(carry, s, values(vst_ref, qi, 1))
        o_slc = acc * (1.0 / l)

        n_wc = (WINDOW + tq) // KEY_CHUNK
        wc0 = jnp.maximum(qi - WINDOW // KEY_CHUNK, 0)
        kw0 = pl.multiple_of(wc0 * KEY_CHUNK, KEY_CHUNK)
        s = jnp.dot(kw_ref[0, pl.ds(kw0, n_wc * KEY_CHUNK), :], q_plain, preferred_element_type=F32)

        def mask_steady(s):
            old = jnp.where(krow > tcol, s[0:KEY_CHUNK], NEG_INF)
            new = jnp.where(krow <= tcol, s[(n_wc - 1) * KEY_CHUNK:], NEG_INF)
            return jnp.concatenate([old, s[KEY_CHUNK:(n_wc - 1) * KEY_CHUNK], new], axis=0)

        def mask_start(s):
            kpos = lax.broadcasted_iota(jnp.int32, (n_wc * KEY_CHUNK, 1), 0)
            return jnp.where(kpos <= t0 + tcol, s, NEG_INF)

        s = lax.cond(qi >= WINDOW // KEY_CHUNK, mask_steady, mask_start, s)
        _, l, acc = _online_update(init, s, values(vwt_ref, wc0, n_wc))
        o_win = acc * (1.0 / l)

        for g, hd in enumerate(heads):
            tot = jnp.zeros((HEAD_DIM, tq), F32)
            for br, o_br in enumerate((o_cmp, o_slc, o_win)):
                r = br * NSA_HEADS + hd
                tot = tot + gates[r:r + 1, :] * o_br[:, g * tq:(g + 1) * tq]
            o_ref[0, hd * HEAD_DIM:(hd + 1) * HEAD_DIM, :] = tot


def _attn(qt, kc, vct, kslc, vst, kwin, vwt, gt, overlap_t):
    bsz, nsa_w, seq = qt.shape
    tq = ATTN_ROWS
    n_cmp = kc.shape[1]
    n_sel = seq // SEL_BLOCK
    n_gate = gt.shape[1]
    assert seq % SLC_KEYS == 0 and n_sel <= HALF and seq >= WINDOW + tq

    def per_batch(shape):
        return pl.BlockSpec((1,) + shape, lambda b, i: (b,) + (0,) * len(shape))

    def const_spec(shape):
        return pl.BlockSpec(shape, lambda b, i: (0,) * len(shape))

    return pl.pallas_call(
        _attn_kernel,
        out_shape=jax.ShapeDtypeStruct((bsz, nsa_w, seq), F32),
        grid=(bsz, seq // tq),
        in_specs=[pl.BlockSpec((1, nsa_w, tq), lambda b, i: (b, 0, i)),
                  per_batch((n_cmp, 2 * LANES)), per_batch((LANES, n_cmp)),
                  per_batch((seq, 2 * LANES)), per_batch((seq // KEY_CHUNK, LANES, KEY_CHUNK)),
                  per_batch((seq, 2 * LANES)), per_batch((seq // KEY_CHUNK, LANES, KEY_CHUNK)),
                  pl.BlockSpec((1, n_gate, tq), lambda b, i: (b, 0, i)),
                  const_spec(overlap_t.shape)],
        out_specs=pl.BlockSpec((1, nsa_w, tq), lambda b, i: (b, 0, i)),
        scratch_shapes=[pltpu.VMEM((n_sel, tq), F32)],
        compiler_params=pltpu.CompilerParams(dimension_semantics=("parallel", "arbitrary"),
                                             vmem_limit_bytes=VMEM_LIMIT),
        name="nsa_attn",
    )(qt, kc, vct, kslc, vst, kwin, vwt, gt, overlap_t)


def _out_kernel(x_ref, ylru_ref, onsat_ref, znsat_ref, gn_ref, gate_ref, wl_ref, wn_ref, o_ref):
    o = onsat_ref[0]
    ms = jnp.mean(o * o, axis=0, keepdims=True)
    z = znsat_ref[0]
    y_t = (o * lax.rsqrt(ms + EPS) * gn_ref[...]) * (z * jax.nn.sigmoid(z))
    y_nsa = jnp.transpose(y_t).astype(BF16)
    out = (jnp.dot(ylru_ref[0].astype(BF16), wl_ref[...], preferred_element_type=F32)
           + jnp.dot(y_nsa, wn_ref[...], preferred_element_type=F32))
    o_ref[0] = x_ref[0] + gate_ref[0, 2:3, :] * out


def _out(x, ylru, onsat, znsat, g_nsa_b, mod3, w_lru, w_nsa):
    bsz, seq, d = x.shape
    ts = PROJ_ROWS
    nsa_w = onsat.shape[1]

    def tok_spec(n):
        return pl.BlockSpec((1, ts, n), lambda b, s: (b, s, 0))

    def feat_spec(n):
        return pl.BlockSpec((1, n, ts), lambda b, s: (b, 0, s))

    def const_spec(shape):
        return pl.BlockSpec(shape, lambda b, s: (0,) * len(shape))

    return pl.pallas_call(
        _out_kernel,
        out_shape=jax.ShapeDtypeStruct((bsz, seq, d), F32),
        grid=(bsz, seq // ts),
        in_specs=[tok_spec(d), tok_spec(ylru.shape[2]), feat_spec(nsa_w), feat_spec(nsa_w),
                  const_spec((nsa_w, ts)),
                  pl.BlockSpec((1, 3, d), lambda b, s: (b, 0, 0)),
                  const_spec(w_lru.shape), const_spec(w_nsa.shape)],
        out_specs=tok_spec(d),
        compiler_params=pltpu.CompilerParams(dimension_semantics=("parallel", "parallel"),
                                             vmem_limit_bytes=VMEM_LIMIT),
        name="out_proj",
    )(x, ylru, onsat, znsat, g_nsa_b, mod3, w_lru, w_nsa)


def _overlap_t(n_cmp_pad, n_sel):
    ratio = SEL_BLOCK // CMP_STRIDE
    ov = np.zeros((n_sel, n_cmp_pad), np.float32)
    for c in range(n_cmp_pad - 1):
        for n in (c // ratio, (c + 1) // ratio):
            if n < n_sel:
                ov[n, c] += 1.0
    return ov


def _block_diag_pairs(w):
    nb, bs, _ = w.shape
    z = jnp.zeros((bs, bs), w.dtype)
    return jnp.stack([jnp.block([[w[2 * s], z], [z, w[2 * s + 1]]]) for s in range(nb // 2)])


def _compress_weights(w1, w2, transposed):
    hid = w1.shape[1]
    w1r = w1.reshape(2, CMP_STRIDE, HEAD_DIM, hid)
    parts = []
    for hk in range(NSA_KV_HEADS):
        for half in range(2):
            slot = jnp.zeros((CMP_STRIDE, NSA_KV_HEADS, HEAD_DIM, hid), w1.dtype).at[:, hk].set(w1r[half])
            parts.append(slot.reshape(CMP_STRIDE * LANES, hid))
    w1x = jnp.concatenate(parts, axis=1).astype(BF16)
    zpad = jnp.zeros_like(w2)
    w2lo = jnp.concatenate([w2, zpad], axis=1).astype(BF16)
    w2hi = jnp.concatenate([zpad, w2], axis=1).astype(BF16)
    if transposed:
        w2lo, w2hi = w2lo.T, w2hi.T
    return w1x, w2lo, w2hi


def _pad_to(n, m):
    return -(-n // m) * m


def _layer(x, c, w_ada, b_ada, norm_g, w_in, conv_w, conv_b, w_rg_a, b_rg_a, w_rg_x, b_rg_x, lru_lambda,
           cmp_pos_k, cmp_w1_k, cmp_w2_k, cmp_pos_v, cmp_w1_v, cmp_w2_v, g_q, g_k_cmp, g_k_slc, g_k_win,
           g_out_lru, g_out_nsa, w_out):
    bsz, seq, d = x.shape
    lru_w = d // 2
    nsa_w = NSA_HEADS * HEAD_DIM
    kv_w = NSA_KV_HEADS * HEAD_DIM
    n_sel = seq // SEL_BLOCK

    splits = (lru_w, lru_w, nsa_w, kv_w, kv_w, kv_w, kv_w, kv_w, kv_w, N_BRANCH * NSA_HEADS, nsa_w)
    offs = np.concatenate([[0], np.cumsum(splits)])
    names = ("xlru", "zlru", "q", "kcmp", "vcmp", "kslc", "vslc", "kwin", "vwin", "gates", "znsa")
    src = {n: (int(offs[i]), int(offs[i + 1])) for i, n in enumerate(names)}

    def plan(group, align):
        pieces, where, at = [], {}, 0
        for n in group:
            piece = w_in[:, src[n][0]:src[n][1]]
            width = _pad_to(piece.shape[1], align)
            if width != piece.shape[1]:
                piece = jnp.pad(piece, ((0, 0), (0, width - piece.shape[1])))
            pieces.append(piece)
            where[n] = (at, width)
            at += width
        return jnp.concatenate(pieces, axis=1).astype(BF16), where

    w_big, cols = plan(("xlru", "zlru", "kcmp", "vcmp", "kslc", "kwin"), LANES)
    w_tt, rows_t = plan(("q", "vslc", "vwin", "gates", "znsa"), 32)
    w_t = w_tt.T

    def dup(g):
        return jnp.concatenate([g, g]).reshape(1, LANES)

    gq_b = jnp.broadcast_to((g_q * (ATTN_SCALE * LOG2E)).reshape(HEAD_DIM, 1), (HEAD_DIM, PROJ_ROWS))
    mod3 = _mod(c, w_ada, b_ada).reshape(bsz, 3, d)
    (xlru, zlru, kcmp, vcmp, kslc, kwin, qt, vst, vwt, gt, znsat) = _proj(
        x, mod3, norm_g.reshape(1, d), w_big, cols, w_t, rows_t, gq_b, dup(g_k_slc), dup(g_k_win))

    ylru = _lru(xlru, zlru, conv_w, conv_b.reshape(1, lru_w),
                _block_diag_pairs(w_rg_a).astype(BF16), b_rg_a.reshape(1, lru_w),
                _block_diag_pairs(w_rg_x).astype(BF16), b_rg_x.reshape(1, lru_w),
                lru_lambda.reshape(1, lru_w), g_out_lru.reshape(1, lru_w))

    n_chunk = seq // CMP_STRIDE
    w1x_k, w2lo_k, w2hi_k = _compress_weights(cmp_w1_k, cmp_w2_k, False)
    w1x_v, w2lo_v, w2hi_v = _compress_weights(cmp_w1_v, cmp_w2_v, True)

    def pos_rows(pos):
        return jnp.broadcast_to(pos.reshape(1, CMP_BLOCK * HEAD_DIM), (8, CMP_BLOCK * HEAD_DIM))

    kc = _compress(kcmp.reshape(bsz, n_chunk, CMP_STRIDE * LANES), w1x_k, pos_rows(cmp_pos_k), cmp_w1_k,
                   w2lo_k, w2hi_k, dup(g_k_cmp), True)
    vct = _compress(vcmp.reshape(bsz, n_chunk, CMP_STRIDE * LANES), w1x_v, pos_rows(cmp_pos_v), cmp_w1_v,
                    w2lo_v, w2hi_v, dup(g_k_cmp), False)

    onsat = _attn(qt, kc, vct, kslc, vst, kwin, vwt, gt, jnp.asarray(_overlap_t(n_chunk, n_sel), dtype=BF16))

    g_nsa_b = jnp.broadcast_to(g_out_nsa.reshape(nsa_w, 1), (nsa_w, PROJ_ROWS))
    return _out(x, ylru, onsat, znsat, g_nsa_b, mod3, w_out[:lru_w].astype(BF16), w_out[lru_w:].astype(BF16))


def kernel(x, c, w_ada, b_ada, norm_g, w_in, conv_w, conv_b, w_rg_a, b_rg_a, w_rg_x, b_rg_x, lru_lambda, cmp_pos_k, cmp_w1_k, cmp_w2_k, cmp_pos_v, cmp_w1_v, cmp_w2_v, g_q, g_k_cmp, g_k_slc, g_k_win, g_out_lru, g_out_nsa, w_out):
    params = (w_ada, b_ada, norm_g, w_in, conv_w, conv_b, w_rg_a, b_rg_a, w_rg_x, b_rg_x, lru_lambda,
              cmp_pos_k, cmp_w1_k, cmp_w2_k, cmp_pos_v, cmp_w1_v, cmp_w2_v, g_q, g_k_cmp, g_k_slc, g_k_win,
              g_out_lru, g_out_nsa, w_out)
    for layer in range(w_in.shape[0]):
        x = _layer(x, c, *(p[layer] for p in params))
    return x
```

```python
import functools
import math

import ml_dtypes
import numpy as np
import jax
import jax.numpy as jnp
from jax import lax
from jax.experimental import pallas as pl
from jax.experimental.pallas import tpu as pltpu

F32 = jnp.float32
BF16 = jnp.bfloat16
HIGHEST = lax.Precision.HIGHEST

LANES = 128
HALF = LANES // 2
SUBLANES = 8

LRU_BLOCKS = 8
CONV_WIDTH = 4
LRU_C = 8.0
NSA_HEADS = 8
HEAD_DIM = 64
NSA_KV_HEADS = 2
NSA_GROUP = NSA_HEADS // NSA_KV_HEADS
N_BRANCH = 3
CMP_STRIDE = 16
CMP_BLOCK = 2 * CMP_STRIDE
SEL_BLOCK = 64
SEL_SHIFT = SEL_BLOCK.bit_length() - 1
SEL_TOPK = 16
WINDOW = 512
ATTN_SCALE = HEAD_DIM ** -0.5
LOG2E = math.log2(math.e)
NEG_INF = -1e30
FORCE = 1e6
EPS = 1e-6

PROJ_ROWS = 512
ATTN_ROWS = 128
SLC_KEYS = 512
KEY_CHUNK = 128
VMEM_LIMIT = 56 * 1024 * 1024

FEAT_ONEHOT = 0
FEAT_BLK = SEL_BLOCK
FEAT_OFF = SEL_BLOCK + 3
FEAT_ONE = SEL_BLOCK + 6
FEAT_CHUNK = LANES - 32
N_SPLIT = 3

assert HEAD_DIM == HALF and NSA_KV_HEADS == 2 and ATTN_ROWS == LANES and KEY_CHUNK == LANES


def _bf16_terms(value):
    terms, rest = [], np.float32(value)
    for _ in range(N_SPLIT):
        t = np.float32(rest).astype(ml_dtypes.bfloat16).astype(np.float32)
        terms.append(float(t))
        rest = np.float32(rest - t)
    return terms


def _alibi_coef(head):
    return float(2.0 ** (-8.0 * (head + 1) / NSA_HEADS)) * LOG2E


def _nt_dot(a, b):
    return lax.dot_general(a, b, (((1,), (1,)), ((), ())), preferred_element_type=F32)


def _half_rms(x, gain2):
    lane = lax.broadcasted_iota(jnp.int32, (1, LANES), 1)
    lo = lane < HALF
    sq = x * x
    ss_lo = jnp.sum(jnp.where(lo, sq, 0.0), axis=-1, keepdims=True)
    ss_hi = jnp.sum(jnp.where(lo, 0.0, sq), axis=-1, keepdims=True)
    r = jnp.where(lo, lax.rsqrt(ss_lo * (1.0 / HALF) + EPS), lax.rsqrt(ss_hi * (1.0 / HALF) + EPS))
    return x * r * gain2


def _key_features(pos, onehot):
    lane = lax.broadcasted_iota(jnp.int32, (1, LANES), 1)
    blk = jnp.right_shift(pos, SEL_SHIFT)
    off = jnp.bitwise_and(pos, SEL_BLOCK - 1)
    f = jnp.where(jnp.logical_and(lane >= FEAT_BLK, lane < FEAT_OFF), blk.astype(F32),
                  jnp.where(jnp.logical_and(lane >= FEAT_OFF, lane < FEAT_ONE), off.astype(F32),
                            jnp.where(lane == FEAT_ONE, 1.0, 0.0)))
    if onehot:
        f = jnp.where(lane < SEL_BLOCK, jnp.where(lane == blk, 1.0, 0.0), f)
        chunk = jnp.right_shift(pos, KEY_CHUNK.bit_length() - 1)
        f = jnp.where(lane >= FEAT_CHUNK, jnp.where(lane - FEAT_CHUNK == chunk, 1.0, 0.0), f)
    return f


def _mod_kernel(c_ref, w_ref, b_ref, o_ref):
    c = c_ref[...]
    o_ref[...] = jnp.dot(c * jax.nn.sigmoid(c), w_ref[...], preferred_element_type=F32, precision=HIGHEST) + b_ref[...]


def _mod(c, w_ada, b_ada):
    bsz, d = c.shape
    n = w_ada.shape[1]
    return pl.pallas_call(
        _mod_kernel,
        out_shape=jax.ShapeDtypeStruct((bsz, n), F32),
        grid=(n // d,),
        in_specs=[pl.BlockSpec((bsz, d), lambda j: (0, 0)),
                  pl.BlockSpec((d, d), lambda j: (0, j)),
                  pl.BlockSpec((1, d), lambda j: (0, j))],
        out_specs=pl.BlockSpec((bsz, d), lambda j: (0, j)),
        name="adaln_mod",
    )(c, w_ada, b_ada.reshape(1, n))


def _log1p(y):
    w = 1.0 + y
    return jnp.where(w == 1.0, y, jnp.log(w) * (y / (w - 1.0)))


def _softplus(x):
    return jnp.maximum(x, 0.0) + _log1p(jnp.exp(-jnp.abs(x)))


def _sigmoid(x):
    return 0.5 * jnp.tanh(0.5 * x) + 0.5


def _lru_gates(x, first, cw_ref, cb_ref, wa_ref, ba_ref, wx_ref, bx_ref, lam_ref, tail_ref, h_ref):
    rows, width = x.shape

    @pl.when(first)
    def _():
        tail_ref[...] = jnp.zeros_like(tail_ref)
        h_ref[...] = jnp.zeros_like(h_ref)

    tail = tail_ref[...]
    tail_ref[...] = x[rows - SUBLANES:rows]
    row8 = lax.broadcasted_iota(jnp.int32, (SUBLANES, 1), 0)
    xc = cb_ref[...] + cw_ref[CONV_WIDTH - 1:CONV_WIDTH, :] * x
    for k in range(1, CONV_WIDTH):
        xs = pltpu.roll(x, k, 0)
        head = jnp.where(row8 < k, pltpu.roll(tail, k, 0), xs[0:SUBLANES])
        xs = jnp.concatenate([head, xs[SUBLANES:]], axis=0)
        xc = xc + cw_ref[CONV_WIDTH - 1 - k:CONV_WIDTH - k, :] * xs

    xcb = xc.astype(BF16)
    nslot = width // LANES
    ra = jnp.concatenate([jnp.dot(xcb[:, s * LANES:(s + 1) * LANES], wa_ref[s], preferred_element_type=F32)
                          for s in range(nslot)], axis=1)
    ri = jnp.concatenate([jnp.dot(xcb[:, s * LANES:(s + 1) * LANES], wx_ref[s], preferred_element_type=F32)
                          for s in range(nslot)], axis=1)
    r = _sigmoid(ra + ba_ref[...])
    i = _sigmoid(ri + bx_ref[...])
    log_a = (-LRU_C) * r * _softplus(-lam_ref[...])
    a = jnp.exp(log_a)
    y = -jnp.tanh(log_a) * (a * a + 1.0)
    return a, jnp.where(y > 0.0, y * lax.rsqrt(y), 0.0) * (i * xc)


def _lru_scan_groups(a, u):
    rows = a.shape[0]
    sub = jnp.bitwise_and(lax.broadcasted_iota(jnp.int32, (rows, 1), 0), SUBLANES - 1)
    acc_a, acc_h = a, u
    d = 1
    while d < SUBLANES:
        keep = sub >= d
        sh_a = pltpu.roll(acc_a, d, 0)
        sh_h = pltpu.roll(acc_h, d, 0)
        acc_h = jnp.where(keep, acc_a * sh_h + acc_h, acc_h)
        acc_a = jnp.where(keep, acc_a * sh_a, acc_a)
        d *= 2
    return acc_a, acc_h


def _lru_finish(acc_a, acc_h, z, g_ref, h_ref):
    rows = acc_a.shape[0]
    carry = h_ref[0:1, :]
    groups = []
    for g in range(rows // SUBLANES):
        hg = acc_h[g * SUBLANES:(g + 1) * SUBLANES] + acc_a[g * SUBLANES:(g + 1) * SUBLANES] * carry
        carry = hg[SUBLANES - 1:SUBLANES]
        groups.append(hg)
    h = jnp.concatenate(groups, axis=0)
    h_ref[0:1, :] = carry

    ms = jnp.mean(h * h, axis=-1, keepdims=True)
    return (h * lax.rsqrt(ms + EPS) * g_ref[...]) * (z * _sigmoid(z))


def _proj_kernel(cols, rows_t, x_ref, mod_ref, ng_ref, w_ref, wt_ref, gq_ref, gks_ref, gkw_ref,
                 cw_ref, cb_ref, wa_ref, ba_ref, wx_ref, bx_ref, lam_ref, gl_ref,
                 ylru_ref, kcmp_ref, vcmp_ref, kslc_ref, kwin_ref,
                 qt_ref, vst_ref, vwt_ref, gt_ref, znsat_ref, tail_ref, h_ref):
    rows = x_ref.shape[1]
    x = x_ref[0]
    ms = jnp.mean(x * x, axis=-1, keepdims=True)
    y = x * lax.rsqrt(ms + EPS) * ng_ref[...]
    h = y * (1.0 + mod_ref[0, 1:2, :]) + mod_ref[0, 0:1, :]
    hb = h.astype(BF16)

    def mm(name):
        c0, n = cols[name]
        return jnp.dot(hb, w_ref[:, c0:c0 + n], preferred_element_type=F32)

    def mmt(name):
        r0, n = rows_t[name]
        return _nt_dot(wt_ref[r0:r0 + n, :], hb)

    lru_a, lru_u = _lru_gates(mm("xlru"), pl.program_id(1) == 0, cw_ref, cb_ref, wa_ref, ba_ref,
                              wx_ref, bx_ref, lam_ref, tail_ref, h_ref)
    kcmp_ref[0] = mm("kcmp").astype(BF16)
    vcmp_ref[0] = mm("vcmp").astype(BF16)

    pos = pl.program_id(1) * rows + lax.broadcasted_iota(jnp.int32, (rows, 1), 0)
    kslc_ref[0, :, 0:LANES] = _half_rms(mm("kslc"), gks_ref[...]).astype(BF16)
    kslc_ref[0, :, LANES:2 * LANES] = _key_features(pos, True).astype(BF16)
    kwin_ref[0, :, 0:LANES] = _half_rms(mm("kwin"), gkw_ref[...]).astype(BF16)
    kwin_ref[0, :, LANES:2 * LANES] = _key_features(pos, False).astype(BF16)

    lru_a, lru_h = _lru_scan_groups(lru_a, lru_u)
    qt = mmt("q")
    for hd in range(NSA_HEADS):
        qh = qt[hd * HEAD_DIM:(hd + 1) * HEAD_DIM]
        r = lax.rsqrt(jnp.mean(qh * qh, axis=0, keepdims=True) + EPS)
        qt_ref[0, hd * HEAD_DIM:(hd + 1) * HEAD_DIM, :] = (qh * r * gq_ref[...]).astype(BF16)

    ylru_ref[0] = _lru_finish(lru_a, lru_h, mm("zlru"), gl_ref, h_ref).astype(BF16)
    vst = mmt("vslc").astype(BF16)
    vwt = mmt("vwin").astype(BF16)
    for j in range(rows // KEY_CHUNK):
        vst_ref[0, j] = vst[:, j * KEY_CHUNK:(j + 1) * KEY_CHUNK]
        vwt_ref[0, j] = vwt[:, j * KEY_CHUNK:(j + 1) * KEY_CHUNK]
    gt_ref[0] = jax.nn.sigmoid(mmt("gates"))
    znsat_ref[0] = mmt("znsa")


def _proj(x, mod3, norm_g, w_big, cols, w_t, rows_t, gq_b, gks2, gkw2, lru_params):
    bsz, seq, d = x.shape
    ts = PROJ_ROWS
    nsa_w = NSA_HEADS * HEAD_DIM
    lru_w = cols["xlru"][1]
    n_gate = rows_t["gates"][1]

    def tok_spec(n):
        return pl.BlockSpec((1, ts, n), lambda b, s: (b, s, 0))

    def feat_spec(n):
        return pl.BlockSpec((1, n, ts), lambda b, s: (b, 0, s))

    def chunk_spec():
        return pl.BlockSpec((1, ts // KEY_CHUNK, LANES, KEY_CHUNK), lambda b, s: (b, s, 0, 0))

    def const_spec(shape):
        return pl.BlockSpec(shape, lambda b, s: (0,) * len(shape))

    out_shape = (
        jax.ShapeDtypeStruct((bsz, seq, lru_w), BF16),
        jax.ShapeDtypeStruct((bsz, seq, LANES), BF16),
        jax.ShapeDtypeStruct((bsz, seq, LANES), BF16),
        jax.ShapeDtypeStruct((bsz, seq, 2 * LANES), BF16),
        jax.ShapeDtypeStruct((bsz, seq, 2 * LANES), BF16),
        jax.ShapeDtypeStruct((bsz, nsa_w, seq), BF16),
        jax.ShapeDtypeStruct((bsz, seq // KEY_CHUNK, LANES, KEY_CHUNK), BF16),
        jax.ShapeDtypeStruct((bsz, seq // KEY_CHUNK, LANES, KEY_CHUNK), BF16),
        jax.ShapeDtypeStruct((bsz, n_gate, seq), F32),
        jax.ShapeDtypeStruct((bsz, nsa_w, seq), F32),
    )
    out_specs = (tok_spec(lru_w), tok_spec(LANES), tok_spec(LANES),
                 tok_spec(2 * LANES), tok_spec(2 * LANES),
                 feat_spec(nsa_w), chunk_spec(), chunk_spec(), feat_spec(n_gate), feat_spec(nsa_w))
    return pl.pallas_call(
        functools.partial(_proj_kernel, cols, rows_t),
        out_shape=out_shape,
        grid=(bsz, seq // ts),
        in_specs=[tok_spec(d),
                  pl.BlockSpec((1, 3, d), lambda b, s: (b, 0, 0)),
                  const_spec((1, d)),
                  const_spec(w_big.shape), const_spec(w_t.shape),
                  const_spec((HEAD_DIM, ts)), const_spec((1, LANES)), const_spec((1, LANES))]
                 + [const_spec(p.shape) for p in lru_params],
        out_specs=out_specs,
        scratch_shapes=[pltpu.VMEM((SUBLANES, lru_w), F32), pltpu.VMEM((SUBLANES, lru_w), F32)],
        compiler_params=pltpu.CompilerParams(dimension_semantics=("parallel", "arbitrary"),
                                             vmem_limit_bytes=VMEM_LIMIT),
        name="in_proj_lru",
    )(x, mod3, norm_g, w_big, w_t, gq_b, gks2, gkw2, *lru_params)


def _compress_kernel(is_key, c_ref, w1x_ref, pos_ref, w1_ref, w2lo_ref, w2hi_ref, g_ref, o_ref):
    nblk = c_ref.shape[1]
    hid = w1_ref.shape[1]
    ab = jnp.dot(c_ref[0], w1x_ref[...], preferred_element_type=F32)
    const = jnp.dot(pos_ref[...], w1_ref[...], preferred_element_type=F32, precision=HIGHEST)[0:1, :]
    acts = []
    for hk in range(NSA_KV_HEADS):
        first = ab[:, (2 * hk) * hid:(2 * hk + 1) * hid]
        second = ab[:, (2 * hk + 1) * hid:(2 * hk + 2) * hid]
        hidden = first + pltpu.roll(second, nblk - 1, 0) + const
        acts.append(jax.nn.gelu(hidden).astype(BF16))
    if is_key:
        out = (jnp.dot(acts[0], w2lo_ref[...], preferred_element_type=F32)
               + jnp.dot(acts[1], w2hi_ref[...], preferred_element_type=F32))
        o_ref[0, :, 0:LANES] = _half_rms(out, g_ref[...]).astype(BF16)
        cend = CMP_STRIDE * lax.broadcasted_iota(jnp.int32, (nblk, 1), 0) + (CMP_BLOCK - 1)
        o_ref[0, :, LANES:2 * LANES] = _key_features(cend, False).astype(BF16)
    else:
        out_t = _nt_dot(w2lo_ref[...], acts[0]) + _nt_dot(w2hi_ref[...], acts[1])
        o_ref[0] = out_t.astype(BF16)


def _compress(chunks, w1x, pos_flat, w1, w2lo, w2hi, gain2, is_key):
    bsz, nblk, width = chunks.shape
    hid = w1.shape[1]

    def const_spec(shape):
        return pl.BlockSpec(shape, lambda b: (0,) * len(shape))

    out_tail = (nblk, 2 * LANES) if is_key else (LANES, nblk)
    return pl.pallas_call(
        functools.partial(_compress_kernel, is_key),
        out_shape=jax.ShapeDtypeStruct((bsz,) + out_tail, BF16),
        grid=(bsz,),
        in_specs=[pl.BlockSpec((1, nblk, width), lambda b: (b, 0, 0)),
                  const_spec((width, 4 * hid)), const_spec((8, w1.shape[0])), const_spec(w1.shape),
                  const_spec(w2lo.shape), const_spec(w2hi.shape), const_spec((1, LANES))],
        out_specs=pl.BlockSpec((1,) + out_tail, lambda b: (b, 0, 0)),
        compiler_params=pltpu.CompilerParams(dimension_semantics=("parallel",), vmem_limit_bytes=VMEM_LIMIT),
        name="compress_k" if is_key else "compress_v",
    )(chunks, w1x, pos_flat, w1, w2lo, w2hi, gain2)


ONES_ROWS = 16


def _with_ones(vt):
    return jnp.concatenate([vt, jnp.ones((ONES_ROWS, vt.shape[1]), BF16)], axis=0)


def _online_update(carry, s, top, vt):
    m, acc = carry
    m_new = jnp.maximum(m, top)
    p = jnp.exp2(s - m_new)
    acc = jnp.exp2(m - m_new) * acc + jnp.dot(_with_ones(vt), p.astype(BF16), preferred_element_type=F32)
    return m_new, acc


def _normalised(acc):
    return acc[0:HEAD_DIM] * (1.0 / acc[HEAD_DIM:HEAD_DIM + 1])


def _attn_kernel(qt_ref, kc_ref, vct_ref, ks_ref, vst_ref, kw_ref, vwt_ref, gt_ref, ovt_ref,
                 o_ref, impt_ref, s_ref):
    tq = qt_ref.shape[2]
    n_cmp = kc_ref.shape[1]
    n_sel = impt_ref.shape[1]
    ncol = NSA_GROUP * tq
    n_wc = (WINDOW + tq) // KEY_CHUNK
    per_wide = SLC_KEYS // KEY_CHUNK
    qi = pl.program_id(1)
    t0 = qi * tq
    tloc = lax.broadcasted_iota(jnp.int32, (1, tq), 1)
    tcol = jnp.concatenate([tloc] * NSA_GROUP, axis=1)
    tpos = t0 + tloc
    nrow = lax.broadcasted_iota(jnp.int32, (n_sel, 1), 0)
    cur = jnp.right_shift(tpos, SEL_SHIFT)
    forced = jnp.logical_or(jnp.logical_or(nrow == 0, nrow == cur), nrow == cur - 1)
    frow = lax.broadcasted_iota(jnp.int32, (HEAD_DIM, 1), 0)
    srow = lax.broadcasted_iota(jnp.int32, (SUBLANES, 1), 0)
    zeros_q = jnp.zeros((HEAD_DIM, tq), BF16)
    row_minus_col = lax.broadcasted_iota(jnp.int32, (n_wc * KEY_CHUNK, 1), 0) - tcol

    cidx = lax.broadcasted_iota(jnp.int32, (n_cmp, 1), 0)
    cend = CMP_STRIDE * cidx + (CMP_BLOCK - 1)
    cmp_ok = jnp.logical_and(cend <= t0 + tcol, cidx < n_cmp - 1)
    col_has_cmp = (t0 + tcol) >= CMP_BLOCK - 1
    gates = gt_ref[0]
    kv_heads = range(NSA_KV_HEADS)

    def values(ref, hk, first_chunk, n):
        return jnp.concatenate([ref[0, first_chunk + i, hk * HEAD_DIM:(hk + 1) * HEAD_DIM, :]
                                for i in range(n)], axis=1)

    not_past = jnp.logical_and(frow >= FEAT_CHUNK - SEL_BLOCK, frow - (FEAT_CHUNK - SEL_BLOCK) >= qi)

    q_plain, q_diag, q_past, o_cmp = [], [], [], []
    for hk in kv_heads:
        q_rows, feat_rows, feat_rows_past = [], [], []
        for g in range(NSA_GROUP):
            hd = hk * NSA_GROUP + g
            qh = qt_ref[0, hd * HEAD_DIM:(hd + 1) * HEAD_DIM, :]
            q_rows.append([qh, zeros_q] if hk == 0 else [zeros_q, qh])
            c = _alibi_coef(hd)
            a = jnp.zeros((HEAD_DIM, 1), F32)
            for i, term in enumerate(_bf16_terms(c)):
                a = jnp.where(frow == FEAT_BLK - SEL_BLOCK + i, SEL_BLOCK * term, a)
                a = jnp.where(frow == FEAT_OFF - SEL_BLOCK + i, term, a)
            a = jnp.where(frow == FEAT_ONE - SEL_BLOCK, -c * t0.astype(F32), a)
            feat_rows.append(jnp.broadcast_to(a, (HEAD_DIM, tq)).astype(BF16))
            feat_rows_past.append(jnp.broadcast_to(jnp.where(not_past, NEG_INF, a), (HEAD_DIM, tq)).astype(BF16))

        def q_operand(pen, feats, q_rows=q_rows):
            return jnp.concatenate([jnp.concatenate(q_rows[g] + [pen, feats[g]], axis=0)
                                    for g in range(NSA_GROUP)], axis=1)

        q_plain.append(q_operand(zeros_q, feat_rows))

        s = jnp.dot(kc_ref[0], q_plain[hk], preferred_element_type=F32)
        s = jnp.where(cmp_ok, s, NEG_INF)
        e = jnp.exp2(s - jnp.max(s, axis=0, keepdims=True))
        p = e * jnp.where(col_has_cmp, 1.0 / jnp.sum(e, axis=0, keepdims=True), 0.0)
        o_cmp.append(jnp.dot(vct_ref[0, hk * HEAD_DIM:(hk + 1) * HEAD_DIM, :], p.astype(BF16),
                             preferred_element_type=F32))
        p_sum = p[:, 0:tq]
        for g in range(1, NSA_GROUP):
            p_sum = p_sum + p[:, g * tq:(g + 1) * tq]
        imp = jnp.zeros((n_sel, tq), F32)
        rest = p_sum
        for _ in range(N_SPLIT):
            term = rest.astype(BF16)
            imp = imp + jnp.dot(ovt_ref[...], term, preferred_element_type=F32)
            rest = rest - term.astype(F32)
        imp = jnp.where(forced, FORCE, jnp.where(nrow > cur, -FORCE, imp))

        impt_ref[hk] = imp
        groups = [imp[b * SUBLANES:(b + 1) * SUBLANES] for b in range(n_sel // SUBLANES)]
        ranks = [jnp.zeros((SUBLANES, tq), F32) for _ in groups]
        for m in range(n_sel):
            other = impt_ref[hk, m:m + 1, :]
            for b, grp in enumerate(groups):
                if b * SUBLANES > m:
                    ahead = other >= grp
                elif (b + 1) * SUBLANES <= m:
                    ahead = other > grp
                else:
                    ahead = jnp.logical_or(other > grp, jnp.logical_and(other == grp, srow > m - b * SUBLANES))
                ranks[b] = ranks[b] + jnp.where(ahead, 1.0, 0.0)
        rank = jnp.concatenate(ranks, axis=0)
        pen = jnp.where(rank < float(SEL_TOPK), 0.0, NEG_INF).astype(BF16)
        q_diag.append(q_operand(pen, feat_rows))
        q_past.append(q_operand(pen, feat_rows_past))

    init = (jnp.full((1, ncol), NEG_INF, F32), jnp.zeros((HEAD_DIM + ONES_ROWS, ncol), F32))
    k_own = ks_ref[0, pl.ds(pl.multiple_of(t0, KEY_CHUNK), KEY_CHUNK), :]
    own_ok = row_minus_col[0:KEY_CHUNK] <= 0
    stats = []
    for hk in kv_heads:
        s = jnp.where(own_ok, jnp.dot(k_own, q_diag[hk], preferred_element_type=F32), NEG_INF)
        stats.append(_online_update(init, s, jnp.max(s, axis=0, keepdims=True), values(vst_ref, hk, qi, 1)))
    stats = tuple(stats)

    def produce(j, slot):
        k = ks_ref[0, pl.ds(pl.multiple_of(j * SLC_KEYS, SLC_KEYS), SLC_KEYS), :]
        tops = []
        for hk in kv_heads:
            s = jnp.dot(k, q_past[hk], preferred_element_type=F32)
            s_ref[slot, hk] = s
            tops.append(jnp.max(s, axis=0, keepdims=True))
        return tuple(tops)

    def consume(j, slot, tops, stats):
        return tuple(_online_update(stats[hk], s_ref[slot, hk], tops[hk],
                                    values(vst_ref, hk, j * per_wide, per_wide)) for hk in kv_heads)

    def slc_pair(i, carry):
        tops0, stats = carry
        tops1 = produce(2 * i + 1, 1)
        stats = consume(2 * i, 0, tops0, stats)
        tops0 = produce(2 * i + 2, 0)
        stats = consume(2 * i + 1, 1, tops1, stats)
        return tops0, stats

    n_pairs = (t0 // SLC_KEYS + 2) // 2
    tops0, stats = lax.fori_loop(0, n_pairs - 1, slc_pair, (produce(0, 0), stats))
    tops1 = produce(2 * n_pairs - 1, 1)
    stats = consume(2 * n_pairs - 2, 0, tops0, stats)
    stats = consume(2 * n_pairs - 1, 1, tops1, stats)
    o_slc = [_normalised(acc) for _, acc in stats]

    wc0 = jnp.maximum(qi - WINDOW // KEY_CHUNK, 0)
    lead = t0 - wc0 * KEY_CHUNK
    k_win = kw_ref[0, pl.ds(pl.multiple_of(wc0 * KEY_CHUNK, KEY_CHUNK), n_wc * KEY_CHUNK), :]
    causal = row_minus_col <= lead
    recent = row_minus_col[0:KEY_CHUNK] > lead - WINDOW
    o_win = []
    for hk in kv_heads:
        s = jnp.where(causal, jnp.dot(k_win, q_plain[hk], preferred_element_type=F32), NEG_INF)
        s = jnp.concatenate([jnp.where(recent, s[0:KEY_CHUNK], NEG_INF), s[KEY_CHUNK:]], axis=0)
        _, acc = _online_update(init, s, jnp.max(s, axis=0, keepdims=True), values(vwt_ref, hk, wc0, n_wc))
        o_win.append(_normalised(acc))

    for hk in kv_heads:
        for g in range(NSA_GROUP):
            hd = hk * NSA_GROUP + g
            tot = jnp.zeros((HEAD_DIM, tq), F32)
            for br, o_br in enumerate((o_cmp[hk], o_slc[hk], o_win[hk])):
                r = br * NSA_HEADS + hd
                tot = tot + gates[r:r + 1, :] * o_br[:, g * tq:(g + 1) * tq]
            o_ref[0, hd * HEAD_DIM:(hd + 1) * HEAD_DIM, :] = tot


def _attn(qt, kc, vct, kslc, vst, kwin, vwt, gt, overlap_t):
    bsz, nsa_w, seq = qt.shape
    tq = ATTN_ROWS
    n_cmp = kc.shape[1]
    n_sel = seq // SEL_BLOCK
    n_gate = gt.shape[1]
    assert seq % (2 * SLC_KEYS) == 0 and n_sel <= HALF and seq >= WINDOW + tq
    assert seq // KEY_CHUNK <= LANES - FEAT_CHUNK and n_sel % SUBLANES == 0

    def per_batch(shape):
        return pl.BlockSpec((1,) + shape, lambda b, i: (b,) + (0,) * len(shape))

    def const_spec(shape):
        return pl.BlockSpec(shape, lambda b, i: (0,) * len(shape))

    return pl.pallas_call(
        _attn_kernel,
        out_shape=jax.ShapeDtypeStruct((bsz, nsa_w, seq), F32),
        grid=(bsz, seq // tq),
        in_specs=[pl.BlockSpec((1, nsa_w, tq), lambda b, i: (b, 0, i)),
                  per_batch((n_cmp, 2 * LANES)), per_batch((LANES, n_cmp)),
                  per_batch((seq, 2 * LANES)), per_batch((seq // KEY_CHUNK, LANES, KEY_CHUNK)),
                  per_batch((seq, 2 * LANES)), per_batch((seq // KEY_CHUNK, LANES, KEY_CHUNK)),
                  pl.BlockSpec((1, n_gate, tq), lambda b, i: (b, 0, i)),
                  const_spec(overlap_t.shape)],
        out_specs=pl.BlockSpec((1, nsa_w, tq), lambda b, i: (b, 0, i)),
        scratch_shapes=[pltpu.VMEM((NSA_KV_HEADS, n_sel, tq), F32),
                        pltpu.VMEM((2, NSA_KV_HEADS, SLC_KEYS, NSA_GROUP * tq), F32)],
        compiler_params=pltpu.CompilerParams(dimension_semantics=("parallel", "arbitrary"),
                                             vmem_limit_bytes=VMEM_LIMIT),
        name="nsa_attn",
    )(qt, kc, vct, kslc, vst, kwin, vwt, gt, overlap_t)


def _out_kernel(x_ref, ylru_ref, onsat_ref, znsat_ref, gn_ref, gate_ref, wl_ref, wn_ref, o_ref):
    o = onsat_ref[0]
    ms = jnp.mean(o * o, axis=0, keepdims=True)
    z = znsat_ref[0]
    y_t = (o * lax.rsqrt(ms + EPS) * gn_ref[...]) * (z * jax.nn.sigmoid(z))
    y_nsa = jnp.transpose(y_t).astype(BF16)
    out = (jnp.dot(ylru_ref[0], wl_ref[...], preferred_element_type=F32)
           + jnp.dot(y_nsa, wn_ref[...], preferred_element_type=F32))
    o_ref[0] = x_ref[0] + gate_ref[0, 2:3, :] * out


def _out(x, ylru, onsat, znsat, g_nsa_b, mod3, w_lru, w_nsa):
    bsz, seq, d = x.shape
    ts = PROJ_ROWS
    nsa_w = onsat.shape[1]

    def tok_spec(n):
        return pl.BlockSpec((1, ts, n), lambda b, s: (b, s, 0))

    def feat_spec(n):
        return pl.BlockSpec((1, n, ts), lambda b, s: (b, 0, s))

    def const_spec(shape):
        return pl.BlockSpec(shape, lambda b, s: (0,) * len(shape))

    return pl.pallas_call(
        _out_kernel,
        out_shape=jax.ShapeDtypeStruct((bsz, seq, d), F32),
        grid=(bsz, seq // ts),
        in_specs=[tok_spec(d), tok_spec(ylru.shape[2]), feat_spec(nsa_w), feat_spec(nsa_w),
                  const_spec((nsa_w, ts)),
                  pl.BlockSpec((1, 3, d), lambda b, s: (b, 0, 0)),
                  const_spec(w_lru.shape), const_spec(w_nsa.shape)],
        out_specs=tok_spec(d),
        compiler_params=pltpu.CompilerParams(dimension_semantics=("parallel", "parallel"),
                                             vmem_limit_bytes=VMEM_LIMIT),
        name="out_proj",
    )(x, ylru, onsat, znsat, g_nsa_b, mod3, w_lru, w_nsa)


def _overlap_t(n_cmp_pad, n_sel):
    ratio = SEL_BLOCK // CMP_STRIDE
    ov = np.zeros((n_sel, n_cmp_pad), np.float32)
    for c in range(n_cmp_pad - 1):
        for n in (c // ratio, (c + 1) // ratio):
            if n < n_sel:
                ov[n, c] += 1.0
    return ov


def _block_diag_pairs(w):
    nb, bs, _ = w.shape
    z = jnp.zeros((bs, bs), w.dtype)
    return jnp.stack([jnp.block([[w[2 * s], z], [z, w[2 * s + 1]]]) for s in range(nb // 2)])


def _compress_weights(w1, w2, transposed):
    hid = w1.shape[1]
    w1r = w1.reshape(2, CMP_STRIDE, HEAD_DIM, hid)
    parts = []
    for hk in range(NSA_KV_HEADS):
        for half in range(2):
            slot = jnp.zeros((CMP_STRIDE, NSA_KV_HEADS, HEAD_DIM, hid), w1.dtype).at[:, hk].set(w1r[half])
            parts.append(slot.reshape(CMP_STRIDE * LANES, hid))
    w1x = jnp.concatenate(parts, axis=1).astype(BF16)
    zpad = jnp.zeros_like(w2)
    w2lo = jnp.concatenate([w2, zpad], axis=1).astype(BF16)
    w2hi = jnp.concatenate([zpad, w2], axis=1).astype(BF16)
    if transposed:
        w2lo, w2hi = w2lo.T, w2hi.T
    return w1x, w2lo, w2hi


def _pad_to(n, m):
    return -(-n // m) * m


def _layer(x, c, w_ada, b_ada, norm_g, w_in, conv_w, conv_b, w_rg_a, b_rg_a, w_rg_x, b_rg_x, lru_lambda,
           cmp_pos_k, cmp_w1_k, cmp_w2_k, cmp_pos_v, cmp_w1_v, cmp_w2_v, g_q, g_k_cmp, g_k_slc, g_k_win,
           g_out_lru, g_out_nsa, w_out):
    bsz, seq, d = x.shape
    lru_w = d // 2
    nsa_w = NSA_HEADS * HEAD_DIM
    kv_w = NSA_KV_HEADS * HEAD_DIM
    n_sel = seq // SEL_BLOCK

    splits = (lru_w, lru_w, nsa_w, kv_w, kv_w, kv_w, kv_w, kv_w, kv_w, N_BRANCH * NSA_HEADS, nsa_w)
    offs = np.concatenate([[0], np.cumsum(splits)])
    names = ("xlru", "zlru", "q", "kcmp", "vcmp", "kslc", "vslc", "kwin", "vwin", "gates", "znsa")
    src = {n: (int(offs[i]), int(offs[i + 1])) for i, n in enumerate(names)}

    def plan(group, align):
        pieces, where, at = [], {}, 0
        for n in group:
            piece = w_in[:, src[n][0]:src[n][1]]
            width = _pad_to(piece.shape[1], align)
            if width != piece.shape[1]:
                piece = jnp.pad(piece, ((0, 0), (0, width - piece.shape[1])))
            pieces.append(piece)
            where[n] = (at, width)
            at += width
        return jnp.concatenate(pieces, axis=1).astype(BF16), where

    w_big, cols = plan(("xlru", "zlru", "kcmp", "vcmp", "kslc", "kwin"), LANES)
    w_tt, rows_t = plan(("q", "vslc", "vwin", "gates", "znsa"), 32)
    w_t = w_tt.T

    def dup(g):
        return jnp.concatenate([g, g]).reshape(1, LANES)

    gq_b = jnp.broadcast_to((g_q * (ATTN_SCALE * LOG2E)).reshape(HEAD_DIM, 1), (HEAD_DIM, PROJ_ROWS))
    mod3 = _mod(c, w_ada, b_ada).reshape(bsz, 3, d)
    lru_params = (conv_w, conv_b.reshape(1, lru_w),
                  _block_diag_pairs(w_rg_a).astype(BF16), b_rg_a.reshape(1, lru_w),
                  _block_diag_pairs(w_rg_x).astype(BF16), b_rg_x.reshape(1, lru_w),
                  lru_lambda.reshape(1, lru_w), g_out_lru.reshape(1, lru_w))
    (ylru, kcmp, vcmp, kslc, kwin, qt, vst, vwt, gt, znsat) = _proj(
        x, mod3, norm_g.reshape(1, d), w_big, cols, w_t, rows_t, gq_b, dup(g_k_slc), dup(g_k_win), lru_params)

    n_chunk = seq // CMP_STRIDE
    w1x_k, w2lo_k, w2hi_k = _compress_weights(cmp_w1_k, cmp_w2_k, False)
    w1x_v, w2lo_v, w2hi_v = _compress_weights(cmp_w1_v, cmp_w2_v, True)

    def pos_rows(pos):
        return jnp.broadcast_to(pos.reshape(1, CMP_BLOCK * HEAD_DIM), (8, CMP_BLOCK * HEAD_DIM))

    kc = _compress(kcmp.reshape(bsz, n_chunk, CMP_STRIDE * LANES), w1x_k, pos_rows(cmp_pos_k), cmp_w1_k,
                   w2lo_k, w2hi_k, dup(g_k_cmp), True)
    vct = _compress(vcmp.reshape(bsz, n_chunk, CMP_STRIDE * LANES), w1x_v, pos_rows(cmp_pos_v), cmp_w1_v,
                    w2lo_v, w2hi_v, dup(g_k_cmp), False)

    onsat = _attn(qt, kc, vct, kslc, vst, kwin, vwt, gt, jnp.asarray(_overlap_t(n_chunk, n_sel), dtype=BF16))

    g_nsa_b = jnp.broadcast_to(g_out_nsa.reshape(nsa_w, 1), (nsa_w, PROJ_ROWS))
    return _out(x, ylru, onsat, znsat, g_nsa_b, mod3, w_out[:lru_w].astype(BF16), w_out[lru_w:].astype(BF16))


def kernel(x, c, w_ada, b_ada, norm_g, w_in, conv_w, conv_b, w_rg_a, b_rg_a, w_rg_x, b_rg_x, lru_lambda, cmp_pos_k, cmp_w1_k, cmp_w2_k, cmp_pos_v, cmp_w1_v, cmp_w2_v, g_q, g_k_cmp, g_k_slc, g_k_win, g_out_lru, g_out_nsa, w_out):
    params = (w_ada, b_ada, norm_g, w_in, conv_w, conv_b, w_rg_a, b_rg_a, w_rg_x, b_rg_x, lru_lambda,
              cmp_pos_k, cmp_w1_k, cmp_w2_k, cmp_pos_v, cmp_w1_v, cmp_w2_v, g_q, g_k_cmp, g_k_slc, g_k_win,
              g_out_lru, g_out_nsa, w_out)
    for layer in range(w_in.shape[0]):
        x = _layer(x, c, *(p[layer] for p in params))
    return x
```

```python
import functools
import math

import ml_dtypes
import numpy as np
import jax
import jax.numpy as jnp
from jax import lax
from jax.experimental import pallas as pl
from jax.experimental.pallas import tpu as pltpu

F32 = jnp.float32
BF16 = jnp.bfloat16
HIGHEST = lax.Precision.HIGHEST

LANES = 128
HALF = LANES // 2
SUBLANES = 8

LRU_BLOCKS = 8
CONV_WIDTH = 4
LRU_C = 8.0
NSA_HEADS = 8
HEAD_DIM = 64
NSA_KV_HEADS = 2
NSA_GROUP = NSA_HEADS // NSA_KV_HEADS
N_BRANCH = 3
CMP_STRIDE = 16
CMP_BLOCK = 2 * CMP_STRIDE
SEL_BLOCK = 64
SEL_SHIFT = SEL_BLOCK.bit_length() - 1
SEL_TOPK = 16
WINDOW = 512
ATTN_SCALE = HEAD_DIM ** -0.5
LOG2E = math.log2(math.e)
NEG_INF = -1e30
FORCE = 1e6
EPS = 1e-6

PROJ_ROWS = 512
ATTN_ROWS = 128
SLC_KEYS = 512
KEY_CHUNK = 128
VMEM_LIMIT = 56 * 1024 * 1024

FEAT_ONEHOT = 0
FEAT_BLK = SEL_BLOCK
FEAT_OFF = SEL_BLOCK + 3
FEAT_ONE = SEL_BLOCK + 6
FEAT_CHUNK = LANES - 32
N_SPLIT = 3

assert HEAD_DIM == HALF and NSA_KV_HEADS == 2 and ATTN_ROWS == LANES and KEY_CHUNK == LANES


def _bf16_terms(value):
    terms, rest = [], np.float32(value)
    for _ in range(N_SPLIT):
        t = np.float32(rest).astype(ml_dtypes.bfloat16).astype(np.float32)
        terms.append(float(t))
        rest = np.float32(rest - t)
    return terms


def _alibi_coef(head):
    return float(2.0 ** (-8.0 * (head + 1) / NSA_HEADS)) * LOG2E


def _nt_dot(a, b):
    return lax.dot_general(a, b, (((1,), (1,)), ((), ())), preferred_element_type=F32)


def _half_rms(x, gain2):
    lane = lax.broadcasted_iota(jnp.int32, (1, LANES), 1)
    lo = lane < HALF
    sq = x * x
    ss_lo = jnp.sum(jnp.where(lo, sq, 0.0), axis=-1, keepdims=True)
    ss_hi = jnp.sum(jnp.where(lo, 0.0, sq), axis=-1, keepdims=True)
    r = jnp.where(lo, lax.rsqrt(ss_lo * (1.0 / HALF) + EPS), lax.rsqrt(ss_hi * (1.0 / HALF) + EPS))
    return x * r * gain2


def _key_features(pos, onehot):
    lane = lax.broadcasted_iota(jnp.int32, (1, LANES), 1)
    blk = jnp.right_shift(pos, SEL_SHIFT)
    off = jnp.bitwise_and(pos, SEL_BLOCK - 1)
    f = jnp.where(jnp.logical_and(lane >= FEAT_BLK, lane < FEAT_OFF), blk.astype(F32),
                  jnp.where(jnp.logical_and(lane >= FEAT_OFF, lane < FEAT_ONE), off.astype(F32),
                            jnp.where(lane == FEAT_ONE, 1.0, 0.0)))
    if onehot:
        f = jnp.where(lane < SEL_BLOCK, jnp.where(lane == blk, 1.0, 0.0), f)
        chunk = jnp.right_shift(pos, KEY_CHUNK.bit_length() - 1)
        f = jnp.where(lane >= FEAT_CHUNK, jnp.where(lane - FEAT_CHUNK == chunk, 1.0, 0.0), f)
    return f


def _mod_kernel(c_ref, w_ref, b_ref, o_ref):
    c = c_ref[...]
    o_ref[...] = jnp.dot(c * jax.nn.sigmoid(c), w_ref[...], preferred_element_type=F32, precision=HIGHEST) + b_ref[...]


def _mod(c, w_ada, b_ada):
    bsz, d = c.shape
    n = w_ada.shape[1]
    return pl.pallas_call(
        _mod_kernel,
        out_shape=jax.ShapeDtypeStruct((bsz, n), F32),
        grid=(n // d,),
        in_specs=[pl.BlockSpec((bsz, d), lambda j: (0, 0)),
                  pl.BlockSpec((d, d), lambda j: (0, j)),
                  pl.BlockSpec((1, d), lambda j: (0, j))],
        out_specs=pl.BlockSpec((bsz, d), lambda j: (0, j)),
        name="adaln_mod",
    )(c, w_ada, b_ada.reshape(1, n))


def _log1p(y):
    w = 1.0 + y
    return jnp.where(w == 1.0, y, jnp.log(w) * (y / (w - 1.0)))


def _softplus(x):
    return jnp.maximum(x, 0.0) + _log1p(jnp.exp(-jnp.abs(x)))


def _sigmoid(x):
    return 0.5 * jnp.tanh(0.5 * x) + 0.5


def _lru_gates(x, first, cw_ref, cb_ref, wa_ref, ba_ref, wx_ref, bx_ref, lam_ref, tail_ref, h_ref):
    rows, width = x.shape

    @pl.when(first)
    def _():
        tail_ref[...] = jnp.zeros_like(tail_ref)
        h_ref[...] = jnp.zeros_like(h_ref)

    tail = tail_ref[...]
    tail_ref[...] = x[rows - SUBLANES:rows]
    row8 = lax.broadcasted_iota(jnp.int32, (SUBLANES, 1), 0)
    xc = cb_ref[...] + cw_ref[CONV_WIDTH - 1:CONV_WIDTH, :] * x
    for k in range(1, CONV_WIDTH):
        xs = pltpu.roll(x, k, 0)
        head = jnp.where(row8 < k, pltpu.roll(tail, k, 0), xs[0:SUBLANES])
        xs = jnp.concatenate([head, xs[SUBLANES:]], axis=0)
        xc = xc + cw_ref[CONV_WIDTH - 1 - k:CONV_WIDTH - k, :] * xs

    xcb = xc.astype(BF16)
    nslot = width // LANES
    ra = jnp.concatenate([jnp.dot(xcb[:, s * LANES:(s + 1) * LANES], wa_ref[s], preferred_element_type=F32)
                          for s in range(nslot)], axis=1)
    ri = jnp.concatenate([jnp.dot(xcb[:, s * LANES:(s + 1) * LANES], wx_ref[s], preferred_element_type=F32)
                          for s in range(nslot)], axis=1)
    r = _sigmoid(ra + ba_ref[...])
    i = _sigmoid(ri + bx_ref[...])
    log_a = (-LRU_C) * r * _softplus(-lam_ref[...])
    a = jnp.exp(log_a)
    y = -jnp.tanh(log_a) * (a * a + 1.0)
    return a, jnp.where(y > 0.0, y * lax.rsqrt(y), 0.0) * (i * xc)


def _lru_scan_groups(a, u):
    rows = a.shape[0]
    sub = jnp.bitwise_and(lax.broadcasted_iota(jnp.int32, (rows, 1), 0), SUBLANES - 1)
    acc_a, acc_h = a, u
    d = 1
    while d < SUBLANES:
        keep = sub >= d
        sh_a = pltpu.roll(acc_a, d, 0)
        sh_h = pltpu.roll(acc_h, d, 0)
        acc_h = jnp.where(keep, acc_a * sh_h + acc_h, acc_h)
        acc_a = jnp.where(keep, acc_a * sh_a, acc_a)
        d *= 2
    return acc_a, acc_h


def _lru_finish(acc_a, acc_h, z, g_ref, h_ref):
    rows = acc_a.shape[0]
    carry = h_ref[0:1, :]
    groups = []
    for g in range(rows // SUBLANES):
        hg = acc_h[g * SUBLANES:(g + 1) * SUBLANES] + acc_a[g * SUBLANES:(g + 1) * SUBLANES] * carry
        carry = hg[SUBLANES - 1:SUBLANES]
        groups.append(hg)
    h = jnp.concatenate(groups, axis=0)
    h_ref[0:1, :] = carry

    ms = jnp.mean(h * h, axis=-1, keepdims=True)
    return (h * lax.rsqrt(ms + EPS) * g_ref[...]) * (z * _sigmoid(z))


def _proj_kernel(cols, rows_t, x_ref, mod_ref, ng_ref, w_ref, wt_ref, gq_ref, gks_ref, gkw_ref,
                 cw_ref, cb_ref, wa_ref, ba_ref, wx_ref, bx_ref, lam_ref, gl_ref,
                 ylru_ref, kcmp_ref, vcmp_ref, kslc_ref, kwin_ref,
                 qt_ref, vst_ref, vwt_ref, gt_ref, znsat_ref, tail_ref, h_ref):
    rows = x_ref.shape[1]
    x = x_ref[0]
    ms = jnp.mean(x * x, axis=-1, keepdims=True)
    y = x * lax.rsqrt(ms + EPS) * ng_ref[...]
    h = y * (1.0 + mod_ref[0, 1:2, :]) + mod_ref[0, 0:1, :]
    hb = h.astype(BF16)

    def mm(name):
        c0, n = cols[name]
        return jnp.dot(hb, w_ref[:, c0:c0 + n], preferred_element_type=F32)

    def mmt(name):
        r0, n = rows_t[name]
        return _nt_dot(wt_ref[r0:r0 + n, :], hb)

    lru_a, lru_u = _lru_gates(mm("xlru"), pl.program_id(1) == 0, cw_ref, cb_ref, wa_ref, ba_ref,
                              wx_ref, bx_ref, lam_ref, tail_ref, h_ref)
    kcmp_ref[0] = mm("kcmp").astype(BF16)
    vcmp_ref[0] = mm("vcmp").astype(BF16)

    pos = pl.program_id(1) * rows + lax.broadcasted_iota(jnp.int32, (rows, 1), 0)
    kslc_ref[0, :, 0:LANES] = _half_rms(mm("kslc"), gks_ref[...]).astype(BF16)
    kslc_ref[0, :, LANES:2 * LANES] = _key_features(pos, True).astype(BF16)
    kwin_ref[0, :, 0:LANES] = _half_rms(mm("kwin"), gkw_ref[...]).astype(BF16)
    kwin_ref[0, :, LANES:2 * LANES] = _key_features(pos, True).astype(BF16)

    lru_a, lru_h = _lru_scan_groups(lru_a, lru_u)
    qt = mmt("q")
    for hd in range(NSA_HEADS):
        qh = qt[hd * HEAD_DIM:(hd + 1) * HEAD_DIM]
        r = lax.rsqrt(jnp.mean(qh * qh, axis=0, keepdims=True) + EPS)
        qt_ref[0, hd * HEAD_DIM:(hd + 1) * HEAD_DIM, :] = (qh * r * gq_ref[...]).astype(BF16)

    ylru_ref[0] = _lru_finish(lru_a, lru_h, mm("zlru"), gl_ref, h_ref).astype(BF16)
    vst = mmt("vslc").astype(BF16)
    vwt = mmt("vwin").astype(BF16)
    for j in range(rows // KEY_CHUNK):
        vst_ref[0, j] = vst[:, j * KEY_CHUNK:(j + 1) * KEY_CHUNK]
        vwt_ref[0, j] = vwt[:, j * KEY_CHUNK:(j + 1) * KEY_CHUNK]
    gt_ref[0] = jax.nn.sigmoid(mmt("gates"))
    znsat_ref[0] = mmt("znsa").astype(BF16)


def _proj(x, mod3, norm_g, w_big, cols, w_t, rows_t, gq_b, gks2, gkw2, lru_params):
    bsz, seq, d = x.shape
    ts = PROJ_ROWS
    nsa_w = NSA_HEADS * HEAD_DIM
    lru_w = cols["xlru"][1]
    n_gate = rows_t["gates"][1]

    def tok_spec(n):
        return pl.BlockSpec((1, ts, n), lambda b, s: (b, s, 0))

    def feat_spec(n):
        return pl.BlockSpec((1, n, ts), lambda b, s: (b, 0, s))

    def chunk_spec():
        return pl.BlockSpec((1, ts // KEY_CHUNK, LANES, KEY_CHUNK), lambda b, s: (b, s, 0, 0))

    def const_spec(shape):
        return pl.BlockSpec(shape, lambda b, s: (0,) * len(shape))

    out_shape = (
        jax.ShapeDtypeStruct((bsz, seq, lru_w), BF16),
        jax.ShapeDtypeStruct((bsz, seq, LANES), BF16),
        jax.ShapeDtypeStruct((bsz, seq, LANES), BF16),
        jax.ShapeDtypeStruct((bsz, seq, 2 * LANES), BF16),
        jax.ShapeDtypeStruct((bsz, seq, 2 * LANES), BF16),
        jax.ShapeDtypeStruct((bsz, nsa_w, seq), BF16),
        jax.ShapeDtypeStruct((bsz, seq // KEY_CHUNK, LANES, KEY_CHUNK), BF16),
        jax.ShapeDtypeStruct((bsz, seq // KEY_CHUNK, LANES, KEY_CHUNK), BF16),
        jax.ShapeDtypeStruct((bsz, n_gate, seq), F32),
        jax.ShapeDtypeStruct((bsz, nsa_w, seq), BF16),
    )
    out_specs = (tok_spec(lru_w), tok_spec(LANES), tok_spec(LANES),
                 tok_spec(2 * LANES), tok_spec(2 * LANES),
                 feat_spec(nsa_w), chunk_spec(), chunk_spec(), feat_spec(n_gate), feat_spec(nsa_w))
    return pl.pallas_call(
        functools.partial(_proj_kernel, cols, rows_t),
        out_shape=out_shape,
        grid=(bsz, seq // ts),
        in_specs=[tok_spec(d),
                  pl.BlockSpec((1, 3, d), lambda b, s: (b, 0, 0)),
                  const_spec((1, d)),
                  const_spec(w_big.shape), const_spec(w_t.shape),
                  const_spec((HEAD_DIM, ts)), const_spec((1, LANES)), const_spec((1, LANES))]
                 + [const_spec(p.shape) for p in lru_params],
        out_specs=out_specs,
        scratch_shapes=[pltpu.VMEM((SUBLANES, lru_w), F32), pltpu.VMEM((SUBLANES, lru_w), F32)],
        compiler_params=pltpu.CompilerParams(dimension_semantics=("parallel", "arbitrary"),
                                             vmem_limit_bytes=VMEM_LIMIT),
        name="in_proj_lru",
    )(x, mod3, norm_g, w_big, w_t, gq_b, gks2, gkw2, *lru_params)


def _compress_kernel(is_key, c_ref, w1x_ref, pos_ref, w1_ref, w2lo_ref, w2hi_ref, g_ref, o_ref):
    nblk = c_ref.shape[1]
    hid = w1_ref.shape[1]
    ab = jnp.dot(c_ref[0], w1x_ref[...], preferred_element_type=F32)
    const = jnp.dot(pos_ref[...], w1_ref[...], preferred_element_type=F32, precision=HIGHEST)[0:1, :]
    acts = []
    for hk in range(NSA_KV_HEADS):
        first = ab[:, (2 * hk) * hid:(2 * hk + 1) * hid]
        second = ab[:, (2 * hk + 1) * hid:(2 * hk + 2) * hid]
        hidden = first + pltpu.roll(second, nblk - 1, 0) + const
        acts.append(jax.nn.gelu(hidden).astype(BF16))
    if is_key:
        out = (jnp.dot(acts[0], w2lo_ref[...], preferred_element_type=F32)
               + jnp.dot(acts[1], w2hi_ref[...], preferred_element_type=F32))
        o_ref[0, :, 0:LANES] = _half_rms(out, g_ref[...]).astype(BF16)
        cend = CMP_STRIDE * lax.broadcasted_iota(jnp.int32, (nblk, 1), 0) + (CMP_BLOCK - 1)
        o_ref[0, :, LANES:2 * LANES] = _key_features(cend, False).astype(BF16)
    else:
        out_t = _nt_dot(w2lo_ref[...], acts[0]) + _nt_dot(w2hi_ref[...], acts[1])
        o_ref[0] = out_t.astype(BF16)


def _compress(chunks, w1x, pos_flat, w1, w2lo, w2hi, gain2, is_key):
    bsz, nblk, width = chunks.shape
    hid = w1.shape[1]

    def const_spec(shape):
        return pl.BlockSpec(shape, lambda b: (0,) * len(shape))

    out_tail = (nblk, 2 * LANES) if is_key else (LANES, nblk)
    return pl.pallas_call(
        functools.partial(_compress_kernel, is_key),
        out_shape=jax.ShapeDtypeStruct((bsz,) + out_tail, BF16),
        grid=(bsz,),
        in_specs=[pl.BlockSpec((1, nblk, width), lambda b: (b, 0, 0)),
                  const_spec((width, 4 * hid)), const_spec((8, w1.shape[0])), const_spec(w1.shape),
                  const_spec(w2lo.shape), const_spec(w2hi.shape), const_spec((1, LANES))],
        out_specs=pl.BlockSpec((1,) + out_tail, lambda b: (b, 0, 0)),
        compiler_params=pltpu.CompilerParams(dimension_semantics=("parallel",), vmem_limit_bytes=VMEM_LIMIT),
        name="compress_k" if is_key else "compress_v",
    )(chunks, w1x, pos_flat, w1, w2lo, w2hi, gain2)


ONES_ROWS = 16


def _with_ones(vt):
    return jnp.concatenate([vt, jnp.ones((ONES_ROWS, vt.shape[1]), BF16)], axis=0)


def _online_update(carry, s, top, vt):
    m, acc = carry
    m_new = jnp.maximum(m, top)
    p = jnp.exp2(s - m_new)
    acc = jnp.exp2(m - m_new) * acc + jnp.dot(_with_ones(vt), p.astype(BF16), preferred_element_type=F32)
    return m_new, acc


def _normalised(acc):
    return acc[0:HEAD_DIM] * (1.0 / acc[HEAD_DIM:HEAD_DIM + 1])


def _attn_kernel(qt_ref, kc_ref, vct_ref, ks_ref, vst_ref, kw_ref, vwt_ref, gt_ref, ovt_ref,
                 o_ref, impt_ref, s_ref):
    tq = qt_ref.shape[2]
    n_cmp = kc_ref.shape[1]
    n_sel = impt_ref.shape[1]
    ncol = NSA_GROUP * tq
    per_wide = SLC_KEYS // KEY_CHUNK
    qi = pl.program_id(1)
    t0 = qi * tq
    tloc = lax.broadcasted_iota(jnp.int32, (1, tq), 1)
    tcol = jnp.concatenate([tloc] * NSA_GROUP, axis=1)
    tpos = t0 + tloc
    nrow = lax.broadcasted_iota(jnp.int32, (n_sel, 1), 0)
    cur = jnp.right_shift(tpos, SEL_SHIFT)
    forced = jnp.logical_or(jnp.logical_or(nrow == 0, nrow == cur), nrow == cur - 1)
    frow = lax.broadcasted_iota(jnp.int32, (HEAD_DIM, 1), 0)
    srow = lax.broadcasted_iota(jnp.int32, (SUBLANES, 1), 0)
    zeros_q = jnp.zeros((HEAD_DIM, tq), BF16)
    row_minus_col = lax.broadcasted_iota(jnp.int32, (KEY_CHUNK, 1), 0) - tcol
    own_ok = row_minus_col <= 0

    cidx = lax.broadcasted_iota(jnp.int32, (n_cmp, 1), 0)
    cend = CMP_STRIDE * cidx + (CMP_BLOCK - 1)
    cmp_ok = jnp.logical_and(cend <= t0 + tcol, cidx < n_cmp - 1)
    col_has_cmp = (t0 + tcol) >= CMP_BLOCK - 1
    gates = gt_ref[0]
    kv_heads = range(NSA_KV_HEADS)

    def values(ref, hk, first_chunk, n):
        return jnp.concatenate([ref[0, first_chunk + i, hk * HEAD_DIM:(hk + 1) * HEAD_DIM, :]
                                for i in range(n)], axis=1)

    not_past = jnp.logical_and(frow >= FEAT_CHUNK - SEL_BLOCK, frow - (FEAT_CHUNK - SEL_BLOCK) >= qi)

    q_plain, q_plain_past, q_diag, q_past, o_cmp = [], [], [], [], []
    for hk in kv_heads:
        q_rows, feat_rows, feat_rows_past = [], [], []
        for g in range(NSA_GROUP):
            hd = hk * NSA_GROUP + g
            qh = qt_ref[0, hd * HEAD_DIM:(hd + 1) * HEAD_DIM, :]
            q_rows.append([qh, zeros_q] if hk == 0 else [zeros_q, qh])
            c = _alibi_coef(hd)
            a = jnp.zeros((HEAD_DIM, 1), F32)
            for i, term in enumerate(_bf16_terms(c)):
                a = jnp.where(frow == FEAT_BLK - SEL_BLOCK + i, SEL_BLOCK * term, a)
                a = jnp.where(frow == FEAT_OFF - SEL_BLOCK + i, term, a)
            a = jnp.where(frow == FEAT_ONE - SEL_BLOCK, -c * t0.astype(F32), a)
            feat_rows.append(jnp.broadcast_to(a, (HEAD_DIM, tq)).astype(BF16))
            feat_rows_past.append(jnp.broadcast_to(jnp.where(not_past, NEG_INF, a), (HEAD_DIM, tq)).astype(BF16))

        def q_operand(pen, feats, q_rows=q_rows):
            return jnp.concatenate([jnp.concatenate(q_rows[g] + [pen, feats[g]], axis=0)
                                    for g in range(NSA_GROUP)], axis=1)

        q_plain.append(q_operand(zeros_q, feat_rows))
        q_plain_past.append(q_operand(zeros_q, feat_rows_past))

        s = jnp.dot(kc_ref[0], q_plain[hk], preferred_element_type=F32)
        s = jnp.where(cmp_ok, s, NEG_INF)
        e = jnp.exp2(s - jnp.max(s, axis=0, keepdims=True))
        p = e * jnp.where(col_has_cmp, 1.0 / jnp.sum(e, axis=0, keepdims=True), 0.0)
        o_cmp.append(jnp.dot(vct_ref[0, hk * HEAD_DIM:(hk + 1) * HEAD_DIM, :], p.astype(BF16),
                             preferred_element_type=F32))
        p_sum = p[:, 0:tq]
        for g in range(1, NSA_GROUP):
            p_sum = p_sum + p[:, g * tq:(g + 1) * tq]
        imp = jnp.zeros((n_sel, tq), F32)
        rest = p_sum
        for _ in range(N_SPLIT):
            term = rest.astype(BF16)
            imp = imp + jnp.dot(ovt_ref[...], term, preferred_element_type=F32)
            rest = rest - term.astype(F32)
        imp = jnp.where(forced, FORCE, jnp.where(nrow > cur, -FORCE, imp))

        impt_ref[hk] = imp
        groups = [imp[b * SUBLANES:(b + 1) * SUBLANES] for b in range(n_sel // SUBLANES)]
        ranks = [jnp.zeros((SUBLANES, tq), F32) for _ in groups]
        for m in range(n_sel):
            other = impt_ref[hk, m:m + 1, :]
            for b, grp in enumerate(groups):
                if b * SUBLANES > m:
                    ahead = other >= grp
                elif (b + 1) * SUBLANES <= m:
                    ahead = other > grp
                else:
                    ahead = jnp.logical_or(other > grp, jnp.logical_and(other == grp, srow > m - b * SUBLANES))
                ranks[b] = ranks[b] + jnp.where(ahead, 1.0, 0.0)
        rank = jnp.concatenate(ranks, axis=0)
        pen = jnp.where(rank < float(SEL_TOPK), 0.0, NEG_INF).astype(BF16)
        q_diag.append(q_operand(pen, feat_rows))
        q_past.append(q_operand(pen, feat_rows_past))

    init = (jnp.full((1, ncol), NEG_INF, F32), jnp.zeros((HEAD_DIM + ONES_ROWS, ncol), F32))

    n_back = WINDOW // KEY_CHUNK
    wc0 = jnp.maximum(qi - n_back, 0)
    lead = t0 - wc0 * KEY_CHUNK
    k_back = kw_ref[0, pl.ds(pl.multiple_of(wc0 * KEY_CHUNK, KEY_CHUNK), WINDOW), :]
    k_tile = kw_ref[0, pl.ds(pl.multiple_of(t0, KEY_CHUNK), KEY_CHUNK), :]
    recent = row_minus_col > lead - WINDOW
    o_win = []
    for hk in kv_heads:
        s_back = jnp.dot(k_back, q_plain_past[hk], preferred_element_type=F32)
        s_own = jnp.where(own_ok, jnp.dot(k_tile, q_plain[hk], preferred_element_type=F32), NEG_INF)
        s = jnp.concatenate([jnp.where(recent, s_back[0:KEY_CHUNK], NEG_INF), s_back[KEY_CHUNK:], s_own], axis=0)
        vt = jnp.concatenate([values(vwt_ref, hk, wc0, n_back), values(vwt_ref, hk, qi, 1)], axis=1)
        _, acc = _online_update(init, s, jnp.max(s, axis=0, keepdims=True), vt)
        o_win.append(_normalised(acc))

    k_own = ks_ref[0, pl.ds(pl.multiple_of(t0, KEY_CHUNK), KEY_CHUNK), :]
    stats = []
    for hk in kv_heads:
        s = jnp.where(own_ok, jnp.dot(k_own, q_diag[hk], preferred_element_type=F32), NEG_INF)
        stats.append(_online_update(init, s, jnp.max(s, axis=0, keepdims=True), values(vst_ref, hk, qi, 1)))
    stats = tuple(stats)

    def produce(j, slot):
        k = ks_ref[0, j * SLC_KEYS:(j + 1) * SLC_KEYS, :]
        tops = []
        for hk in kv_heads:
            s = jnp.dot(k, q_past[hk], preferred_element_type=F32)
            s_ref[slot, hk] = s
            tops.append(jnp.max(s, axis=0, keepdims=True))
        return tuple(tops)

    def consume(j, slot, tops, stats):
        return tuple(_online_update(stats[hk], s_ref[slot, hk], tops[hk],
                                    values(vst_ref, hk, j * per_wide, per_wide)) for hk in kv_heads)

    def chain(n_blocks):
        def run(stats):
            tops = produce(0, 0)
            for j in range(n_blocks):
                nxt = produce(j + 1, (j + 1) % 2) if j + 1 < n_blocks else None
                stats = consume(j, j % 2, tops, stats)
                tops = nxt
            return stats
        return run

    stats = lax.switch(t0 // SLC_KEYS, [chain(n + 1) for n in range(ks_ref.shape[1] // SLC_KEYS)], stats)
    o_slc = [_normalised(acc) for _, acc in stats]

    for hk in kv_heads:
        for g in range(NSA_GROUP):
            hd = hk * NSA_GROUP + g
            tot = jnp.zeros((HEAD_DIM, tq), F32)
            for br, o_br in enumerate((o_cmp[hk], o_slc[hk], o_win[hk])):
                r = br * NSA_HEADS + hd
                tot = tot + gates[r:r + 1, :] * o_br[:, g * tq:(g + 1) * tq]
            o_ref[0, hd * HEAD_DIM:(hd + 1) * HEAD_DIM, :] = tot.astype(BF16)


def _attn(qt, kc, vct, kslc, vst, kwin, vwt, gt, overlap_t):
    bsz, nsa_w, seq = qt.shape
    tq = ATTN_ROWS
    n_cmp = kc.shape[1]
    n_sel = seq // SEL_BLOCK
    n_gate = gt.shape[1]
    assert seq % (2 * SLC_KEYS) == 0 and n_sel <= HALF and seq >= WINDOW + tq
    assert seq // KEY_CHUNK <= LANES - FEAT_CHUNK and n_sel % SUBLANES == 0

    def per_batch(shape):
        return pl.BlockSpec((1,) + shape, lambda b, i: (b,) + (0,) * len(shape))

    def const_spec(shape):
        return pl.BlockSpec(shape, lambda b, i: (0,) * len(shape))

    return pl.pallas_call(
        _attn_kernel,
        out_shape=jax.ShapeDtypeStruct((bsz, nsa_w, seq), BF16),
        grid=(bsz, seq // tq),
        in_specs=[pl.BlockSpec((1, nsa_w, tq), lambda b, i: (b, 0, i)),
                  per_batch((n_cmp, 2 * LANES)), per_batch((LANES, n_cmp)),
                  per_batch((seq, 2 * LANES)), per_batch((seq // KEY_CHUNK, LANES, KEY_CHUNK)),
                  per_batch((seq, 2 * LANES)), per_batch((seq // KEY_CHUNK, LANES, KEY_CHUNK)),
                  pl.BlockSpec((1, n_gate, tq), lambda b, i: (b, 0, i)),
                  const_spec(overlap_t.shape)],
        out_specs=pl.BlockSpec((1, nsa_w, tq), lambda b, i: (b, 0, i)),
        scratch_shapes=[pltpu.VMEM((NSA_KV_HEADS, n_sel, tq), F32),
                        pltpu.VMEM((2, NSA_KV_HEADS, SLC_KEYS, NSA_GROUP * tq), F32)],
        compiler_params=pltpu.CompilerParams(dimension_semantics=("parallel", "arbitrary"),
                                             vmem_limit_bytes=VMEM_LIMIT),
        name="nsa_attn",
    )(qt, kc, vct, kslc, vst, kwin, vwt, gt, overlap_t)


def _out_kernel(x_ref, ylru_ref, onsat_ref, znsat_ref, gn_ref, gate_ref, wl_ref, wn_ref, o_ref):
    o = onsat_ref[0].astype(F32)
    ms = jnp.mean(o * o, axis=0, keepdims=True)
    z = znsat_ref[0].astype(F32)
    y_t = (o * lax.rsqrt(ms + EPS) * gn_ref[...]) * (z * jax.nn.sigmoid(z))
    y_nsa = jnp.transpose(y_t).astype(BF16)
    out = (jnp.dot(ylru_ref[0], wl_ref[...], preferred_element_type=F32)
           + jnp.dot(y_nsa, wn_ref[...], preferred_element_type=F32))
    o_ref[0] = x_ref[0] + gate_ref[0, 2:3, :] * out


def _out(x, ylru, onsat, znsat, g_nsa_b, mod3, w_lru, w_nsa):
    bsz, seq, d = x.shape
    ts = PROJ_ROWS
    nsa_w = onsat.shape[1]

    def tok_spec(n):
        return pl.BlockSpec((1, ts, n), lambda b, s: (b, s, 0))

    def feat_spec(n):
        return pl.BlockSpec((1, n, ts), lambda b, s: (b, 0, s))

    def const_spec(shape):
        return pl.BlockSpec(shape, lambda b, s: (0,) * len(shape))

    return pl.pallas_call(
        _out_kernel,
        out_shape=jax.ShapeDtypeStruct((bsz, seq, d), F32),
        grid=(bsz, seq // ts),
        in_specs=[tok_spec(d), tok_spec(ylru.shape[2]), feat_spec(nsa_w), feat_spec(nsa_w),
                  const_spec((nsa_w, ts)),
                  pl.BlockSpec((1, 3, d), lambda b, s: (b, 0, 0)),
                  const_spec(w_lru.shape), const_spec(w_nsa.shape)],
        out_specs=tok_spec(d),
        compiler_params=pltpu.CompilerParams(dimension_semantics=("parallel", "parallel"),
                                             vmem_limit_bytes=VMEM_LIMIT),
        name="out_proj",
    )(x, ylru, onsat, znsat, g_nsa_b, mod3, w_lru, w_nsa)


def _overlap_t(n_cmp_pad, n_sel):
    ratio = SEL_BLOCK // CMP_STRIDE
    ov = np.zeros((n_sel, n_cmp_pad), np.float32)
    for c in range(n_cmp_pad - 1):
        for n in (c // ratio, (c + 1) // ratio):
            if n < n_sel:
                ov[n, c] += 1.0
    return ov


def _block_diag_pairs(w):
    nb, bs, _ = w.shape
    z = jnp.zeros((bs, bs), w.dtype)
    return jnp.stack([jnp.block([[w[2 * s], z], [z, w[2 * s + 1]]]) for s in range(nb // 2)])


def _compress_weights(w1, w2, transposed):
    hid = w1.shape[1]
    w1r = w1.reshape(2, CMP_STRIDE, HEAD_DIM, hid)
    parts = []
    for hk in range(NSA_KV_HEADS):
        for half in range(2):
            slot = jnp.zeros((CMP_STRIDE, NSA_KV_HEADS, HEAD_DIM, hid), w1.dtype).at[:, hk].set(w1r[half])
            parts.append(slot.reshape(CMP_STRIDE * LANES, hid))
    w1x = jnp.concatenate(parts, axis=1).astype(BF16)
    zpad = jnp.zeros_like(w2)
    w2lo = jnp.concatenate([w2, zpad], axis=1).astype(BF16)
    w2hi = jnp.concatenate([zpad, w2], axis=1).astype(BF16)
    if transposed:
        w2lo, w2hi = w2lo.T, w2hi.T
    return w1x, w2lo, w2hi


def _pad_to(n, m):
    return -(-n // m) * m


def _layer(x, c, w_ada, b_ada, norm_g, w_in, conv_w, conv_b, w_rg_a, b_rg_a, w_rg_x, b_rg_x, lru_lambda,
           cmp_pos_k, cmp_w1_k, cmp_w2_k, cmp_pos_v, cmp_w1_v, cmp_w2_v, g_q, g_k_cmp, g_k_slc, g_k_win,
           g_out_lru, g_out_nsa, w_out):
    bsz, seq, d = x.shape
    lru_w = d // 2
    nsa_w = NSA_HEADS * HEAD_DIM
    kv_w = NSA_KV_HEADS * HEAD_DIM
    n_sel = seq // SEL_BLOCK

    splits = (lru_w, lru_w, nsa_w, kv_w, kv_w, kv_w, kv_w, kv_w, kv_w, N_BRANCH * NSA_HEADS, nsa_w)
    offs = np.concatenate([[0], np.cumsum(splits)])
    names = ("xlru", "zlru", "q", "kcmp", "vcmp", "kslc", "vslc", "kwin", "vwin", "gates", "znsa")
    src = {n: (int(offs[i]), int(offs[i + 1])) for i, n in enumerate(names)}

    def plan(group, align):
        pieces, where, at = [], {}, 0
        for n in group:
            piece = w_in[:, src[n][0]:src[n][1]]
            width = _pad_to(piece.shape[1], align)
            if width != piece.shape[1]:
                piece = jnp.pad(piece, ((0, 0), (0, width - piece.shape[1])))
            pieces.append(piece)
            where[n] = (at, width)
            at += width
        return jnp.concatenate(pieces, axis=1).astype(BF16), where

    w_big, cols = plan(("xlru", "zlru", "kcmp", "vcmp", "kslc", "kwin"), LANES)
    w_tt, rows_t = plan(("q", "vslc", "vwin", "gates", "znsa"), 32)
    w_t = w_tt.T

    def dup(g):
        return jnp.concatenate([g, g]).reshape(1, LANES)

    gq_b = jnp.broadcast_to((g_q * (ATTN_SCALE * LOG2E)).reshape(HEAD_DIM, 1), (HEAD_DIM, PROJ_ROWS))
    mod3 = _mod(c, w_ada, b_ada).reshape(bsz, 3, d)
    lru_params = (conv_w, conv_b.reshape(1, lru_w),
                  _block_diag_pairs(w_rg_a).astype(BF16), b_rg_a.reshape(1, lru_w),
                  _block_diag_pairs(w_rg_x).astype(BF16), b_rg_x.reshape(1, lru_w),
                  lru_lambda.reshape(1, lru_w), g_out_lru.reshape(1, lru_w))
    (ylru, kcmp, vcmp, kslc, kwin, qt, vst, vwt, gt, znsat) = _proj(
        x, mod3, norm_g.reshape(1, d), w_big, cols, w_t, rows_t, gq_b, dup(g_k_slc), dup(g_k_win), lru_params)

    n_chunk = seq // CMP_STRIDE
    w1x_k, w2lo_k, w2hi_k = _compress_weights(cmp_w1_k, cmp_w2_k, False)
    w1x_v, w2lo_v, w2hi_v = _compress_weights(cmp_w1_v, cmp_w2_v, True)

    def pos_rows(pos):
        return jnp.broadcast_to(pos.reshape(1, CMP_BLOCK * HEAD_DIM), (8, CMP_BLOCK * HEAD_DIM))

    kc = _compress(kcmp.reshape(bsz, n_chunk, CMP_STRIDE * LANES), w1x_k, pos_rows(cmp_pos_k), cmp_w1_k,
                   w2lo_k, w2hi_k, dup(g_k_cmp), True)
    vct = _compress(vcmp.reshape(bsz, n_chunk, CMP_STRIDE * LANES), w1x_v, pos_rows(cmp_pos_v), cmp_w1_v,
                    w2lo_v, w2hi_v, dup(g_k_cmp), False)

    onsat = _attn(qt, kc, vct, kslc, vst, kwin, vwt, gt, jnp.asarray(_overlap_t(n_chunk, n_sel), dtype=BF16))

    g_nsa_b = jnp.broadcast_to(g_out_nsa.reshape(nsa_w, 1), (nsa_w, PROJ_ROWS))
    return _out(x, ylru, onsat, znsat, g_nsa_b, mod3, w_out[:lru_w].astype(BF16), w_out[lru_w:].astype(BF16))


def kernel(x, c, w_ada, b_ada, norm_g, w_in, conv_w, conv_b, w_rg_a, b_rg_a, w_rg_x, b_rg_x, lru_lambda, cmp_pos_k, cmp_w1_k, cmp_w2_k, cmp_pos_v, cmp_w1_v, cmp_w2_v, g_q, g_k_cmp, g_k_slc, g_k_win, g_out_lru, g_out_nsa, w_out):
    params = (w_ada, b_ada, norm_g, w_in, conv_w, conv_b, w_rg_a, b_rg_a, w_rg_x, b_rg_x, lru_lambda,
              cmp_pos_k, cmp_w1_k, cmp_w2_k, cmp_pos_v, cmp_w1_v, cmp_w2_v, g_q, g_k_cmp, g_k_slc, g_k_win,
              g_out_lru, g_out_nsa, w_out)
    for layer in range(w_in.shape[0]):
        x = _layer(x, c, *(p[layer] for p in params))
    return x
```

```python
import functools
import math

import ml_dtypes
import numpy as np
import jax
import jax.numpy as jnp
from jax import lax
from jax.experimental import pallas as pl
from jax.experimental.pallas import tpu as pltpu

F32 = jnp.float32
BF16 = jnp.bfloat16
HIGHEST = lax.Precision.HIGHEST

LANES = 128
HALF = LANES // 2
SUBLANES = 8

LRU_BLOCKS = 8
CONV_WIDTH = 4
LRU_C = 8.0
NSA_HEADS = 8
HEAD_DIM = 64
NSA_KV_HEADS = 2
NSA_GROUP = NSA_HEADS // NSA_KV_HEADS
N_BRANCH = 3
CMP_STRIDE = 16
CMP_BLOCK = 2 * CMP_STRIDE
SEL_BLOCK = 64
SEL_SHIFT = SEL_BLOCK.bit_length() - 1
SEL_TOPK = 16
WINDOW = 512
ATTN_SCALE = HEAD_DIM ** -0.5
LOG2E = math.log2(math.e)
NEG_INF = -1e30
FORCE = 1e6
EPS = 1e-6

PROJ_ROWS = 1024
OUT_ROWS = 1024
ATTN_ROWS = 128
SLC_KEYS = 512
KEY_CHUNK = 128
VMEM_LIMIT = 56 * 1024 * 1024

FEAT_ONEHOT = 0
FEAT_BLK = SEL_BLOCK
FEAT_OFF = SEL_BLOCK + 3
FEAT_ONE = SEL_BLOCK + 6
FEAT_CHUNK = LANES - 32
N_SPLIT = 3

assert HEAD_DIM == HALF and NSA_KV_HEADS == 2 and ATTN_ROWS == LANES and KEY_CHUNK == LANES


def _bf16_terms(value):
    terms, rest = [], np.float32(value)
    for _ in range(N_SPLIT):
        t = np.float32(rest).astype(ml_dtypes.bfloat16).astype(np.float32)
        terms.append(float(t))
        rest = np.float32(rest - t)
    return terms


def _alibi_coef(head):
    return float(2.0 ** (-8.0 * (head + 1) / NSA_HEADS)) * LOG2E


def _nt_dot(a, b):
    return lax.dot_general(a, b, (((1,), (1,)), ((), ())), preferred_element_type=F32)


def _half_rms(x, gain2):
    lane = lax.broadcasted_iota(jnp.int32, (1, LANES), 1)
    lo = lane < HALF
    sq = x * x
    ss_lo = jnp.sum(jnp.where(lo, sq, 0.0), axis=-1, keepdims=True)
    ss_hi = jnp.sum(jnp.where(lo, 0.0, sq), axis=-1, keepdims=True)
    r = jnp.where(lo, lax.rsqrt(ss_lo * (1.0 / HALF) + EPS), lax.rsqrt(ss_hi * (1.0 / HALF) + EPS))
    return x * r * gain2


def _key_features(pos, onehot):
    lane = lax.broadcasted_iota(jnp.int32, (1, LANES), 1)
    blk = jnp.right_shift(pos, SEL_SHIFT)
    off = jnp.bitwise_and(pos, SEL_BLOCK - 1)
    f = jnp.where(jnp.logical_and(lane >= FEAT_BLK, lane < FEAT_OFF), blk.astype(F32),
                  jnp.where(jnp.logical_and(lane >= FEAT_OFF, lane < FEAT_ONE), off.astype(F32),
                            jnp.where(lane == FEAT_ONE, 1.0, 0.0)))
    if onehot:
        f = jnp.where(lane < SEL_BLOCK, jnp.where(lane == blk, 1.0, 0.0), f)
        chunk = jnp.right_shift(pos, KEY_CHUNK.bit_length() - 1)
        f = jnp.where(lane >= FEAT_CHUNK, jnp.where(lane - FEAT_CHUNK == chunk, 1.0, 0.0), f)
    return f


def _mod_kernel(c_ref, w_ref, b_ref, o_ref):
    c = c_ref[...]
    o_ref[...] = jnp.dot(c * jax.nn.sigmoid(c), w_ref[...], preferred_element_type=F32, precision=HIGHEST) + b_ref[...]


def _mod(c, w_ada, b_ada):
    bsz, d = c.shape
    n = w_ada.shape[1]
    return pl.pallas_call(
        _mod_kernel,
        out_shape=jax.ShapeDtypeStruct((bsz, n), F32),
        grid=(n // d,),
        in_specs=[pl.BlockSpec((bsz, d), lambda j: (0, 0)),
                  pl.BlockSpec((d, d), lambda j: (0, j)),
                  pl.BlockSpec((1, d), lambda j: (0, j))],
        out_specs=pl.BlockSpec((bsz, d), lambda j: (0, j)),
        name="adaln_mod",
    )(c, w_ada, b_ada.reshape(1, n))


def _log1p(y):
    w = 1.0 + y
    return jnp.where(w == 1.0, y, jnp.log(w) * (y / (w - 1.0)))


def _softplus(x):
    return jnp.maximum(x, 0.0) + _log1p(jnp.exp(-jnp.abs(x)))


def _sigmoid(x):
    return 0.5 * jnp.tanh(0.5 * x) + 0.5


def _lru_gates(x, first, cw_ref, cb_ref, wa_ref, ba_ref, wx_ref, bx_ref, lam_ref, tail_ref, h_ref):
    rows, width = x.shape

    @pl.when(first)
    def _():
        tail_ref[...] = jnp.zeros_like(tail_ref)
        h_ref[...] = jnp.zeros_like(h_ref)

    tail = tail_ref[...]
    tail_ref[...] = x[rows - SUBLANES:rows]
    row8 = lax.broadcasted_iota(jnp.int32, (SUBLANES, 1), 0)
    xc = cb_ref[...] + cw_ref[CONV_WIDTH - 1:CONV_WIDTH, :] * x
    for k in range(1, CONV_WIDTH):
        xs = pltpu.roll(x, k, 0)
        head = jnp.where(row8 < k, pltpu.roll(tail, k, 0), xs[0:SUBLANES])
        xs = jnp.concatenate([head, xs[SUBLANES:]], axis=0)
        xc = xc + cw_ref[CONV_WIDTH - 1 - k:CONV_WIDTH - k, :] * xs

    xcb = xc.astype(BF16)
    nslot = width // LANES
    ra = jnp.concatenate([jnp.dot(xcb[:, s * LANES:(s + 1) * LANES], wa_ref[s], preferred_element_type=F32)
                          for s in range(nslot)], axis=1)
    ri = jnp.concatenate([jnp.dot(xcb[:, s * LANES:(s + 1) * LANES], wx_ref[s], preferred_element_type=F32)
                          for s in range(nslot)], axis=1)
    r = _sigmoid(ra + ba_ref[...])
    i = _sigmoid(ri + bx_ref[...])
    log_a = (-LRU_C) * r * _softplus(-lam_ref[...])
    a = jnp.exp(log_a)
    y = -jnp.tanh(log_a) * (a * a + 1.0)
    return a, jnp.where(y > 0.0, y * lax.rsqrt(y), 0.0) * (i * xc)


def _lru_scan_groups(a, u):
    rows = a.shape[0]
    sub = jnp.bitwise_and(lax.broadcasted_iota(jnp.int32, (rows, 1), 0), SUBLANES - 1)
    acc_a, acc_h = a, u
    d = 1
    while d < SUBLANES:
        keep = sub >= d
        sh_a = pltpu.roll(acc_a, d, 0)
        sh_h = pltpu.roll(acc_h, d, 0)
        acc_h = jnp.where(keep, acc_a * sh_h + acc_h, acc_h)
        acc_a = jnp.where(keep, acc_a * sh_a, acc_a)
        d *= 2
    return acc_a, acc_h


def _lru_carry(acc_a, acc_h, h_ref):
    rows = acc_a.shape[0]
    carry = h_ref[0:1, :]
    groups = []
    for g in range(rows // SUBLANES):
        hg = acc_h[g * SUBLANES:(g + 1) * SUBLANES] + acc_a[g * SUBLANES:(g + 1) * SUBLANES] * carry
        carry = hg[SUBLANES - 1:SUBLANES]
        groups.append(hg)
    h_ref[0:1, :] = carry
    return jnp.concatenate(groups, axis=0)


def _lru_out(h, z, g_ref):
    ms = jnp.mean(h * h, axis=-1, keepdims=True)
    return (h * lax.rsqrt(ms + EPS) * g_ref[...]) * (z * _sigmoid(z))


def _proj_kernel(cols, rows_t, x_ref, mod_ref, ng_ref, w_ref, wt_ref, gq_ref, gks_ref, gkw_ref,
                 cw_ref, cb_ref, wa_ref, ba_ref, wx_ref, bx_ref, lam_ref, gl_ref,
                 ylru_ref, kcmp_ref, vcmp_ref, kslc_ref, kwin_ref,
                 qt_ref, vst_ref, vwt_ref, gt_ref, znsat_ref, tail_ref, h_ref):
    rows = x_ref.shape[1]
    x = x_ref[0]
    ms = jnp.mean(x * x, axis=-1, keepdims=True)
    y = x * lax.rsqrt(ms + EPS) * ng_ref[...]
    h = y * (1.0 + mod_ref[0, 1:2, :]) + mod_ref[0, 0:1, :]
    hb = h.astype(BF16)

    def mm(name):
        c0, n = cols[name]
        return jnp.dot(hb, w_ref[:, c0:c0 + n], preferred_element_type=F32)

    def mmt(name):
        r0, n = rows_t[name]
        return _nt_dot(wt_ref[r0:r0 + n, :], hb)

    def after(value, done):
        return value + jnp.minimum(jnp.abs(done[done.shape[0] - 1:, 0:1]), 0.0)

    lru_a, lru_u = _lru_gates(mm("xlru"), pl.program_id(1) == 0, cw_ref, cb_ref, wa_ref, ba_ref,
                              wx_ref, bx_ref, lam_ref, tail_ref, h_ref)
    z_lru = mm("zlru")
    kcmp_ref[0] = mm("kcmp").astype(BF16)
    vcmp_ref[0] = mm("vcmp").astype(BF16)

    pos = pl.program_id(1) * rows + lax.broadcasted_iota(jnp.int32, (rows, 1), 0)
    kslc_ref[0, :, 0:LANES] = _half_rms(mm("kslc"), gks_ref[...]).astype(BF16)
    kslc_ref[0, :, LANES:2 * LANES] = _key_features(pos, True).astype(BF16)
    k_win = mm("kwin")
    kwin_ref[0, :, 0:LANES] = _half_rms(k_win, gkw_ref[...]).astype(BF16)
    kwin_ref[0, :, LANES:2 * LANES] = _key_features(pos, True).astype(BF16)

    lru_a, lru_h = _lru_scan_groups(lru_a, after(lru_u, k_win))
    qt = mmt("q")
    for hd in range(NSA_HEADS):
        qh = qt[hd * HEAD_DIM:(hd + 1) * HEAD_DIM]
        r = lax.rsqrt(jnp.mean(qh * qh, axis=0, keepdims=True) + EPS)
        qt_ref[0, hd * HEAD_DIM:(hd + 1) * HEAD_DIM, :] = (qh * r * gq_ref[...]).astype(BF16)

    h_lru = _lru_carry(lru_a, after(lru_h, qt), h_ref)
    vst = mmt("vslc").astype(BF16)
    vwt = mmt("vwin").astype(BF16)
    for j in range(rows // KEY_CHUNK):
        vst_ref[0, j] = vst[:, j * KEY_CHUNK:(j + 1) * KEY_CHUNK]
        vwt_ref[0, j] = vwt[:, j * KEY_CHUNK:(j + 1) * KEY_CHUNK]
    gt_ref[0] = jax.nn.sigmoid(mmt("gates"))
    z_nsa = mmt("znsa")
    znsat_ref[0] = z_nsa.astype(BF16)
    ylru_ref[0] = _lru_out(after(h_lru, z_nsa), z_lru, gl_ref).astype(BF16)


def _proj(x, mod3, norm_g, w_big, cols, w_t, rows_t, gq_b, gks2, gkw2, lru_params):
    bsz, seq, d = x.shape
    ts = PROJ_ROWS
    nsa_w = NSA_HEADS * HEAD_DIM
    lru_w = cols["xlru"][1]
    n_gate = rows_t["gates"][1]

    def tok_spec(n):
        return pl.BlockSpec((1, ts, n), lambda b, s: (b, s, 0))

    def feat_spec(n):
        return pl.BlockSpec((1, n, ts), lambda b, s: (b, 0, s))

    def chunk_spec():
        return pl.BlockSpec((1, ts // KEY_CHUNK, LANES, KEY_CHUNK), lambda b, s: (b, s, 0, 0))

    def const_spec(shape):
        return pl.BlockSpec(shape, lambda b, s: (0,) * len(shape))

    out_shape = (
        jax.ShapeDtypeStruct((bsz, seq, lru_w), BF16),
        jax.ShapeDtypeStruct((bsz, seq, LANES), BF16),
        jax.ShapeDtypeStruct((bsz, seq, LANES), BF16),
        jax.ShapeDtypeStruct((bsz, seq, 2 * LANES), BF16),
        jax.ShapeDtypeStruct((bsz, seq, 2 * LANES), BF16),
        jax.ShapeDtypeStruct((bsz, nsa_w, seq), BF16),
        jax.ShapeDtypeStruct((bsz, seq // KEY_CHUNK, LANES, KEY_CHUNK), BF16),
        jax.ShapeDtypeStruct((bsz, seq // KEY_CHUNK, LANES, KEY_CHUNK), BF16),
        jax.ShapeDtypeStruct((bsz, n_gate, seq), F32),
        jax.ShapeDtypeStruct((bsz, nsa_w, seq), BF16),
    )
    out_specs = (tok_spec(lru_w), tok_spec(LANES), tok_spec(LANES),
                 tok_spec(2 * LANES), tok_spec(2 * LANES),
                 feat_spec(nsa_w), chunk_spec(), chunk_spec(), feat_spec(n_gate), feat_spec(nsa_w))
    return pl.pallas_call(
        functools.partial(_proj_kernel, cols, rows_t),
        out_shape=out_shape,
        grid=(bsz, seq // ts),
        in_specs=[tok_spec(d),
                  pl.BlockSpec((1, 3, d), lambda b, s: (b, 0, 0)),
                  const_spec((1, d)),
                  const_spec(w_big.shape), const_spec(w_t.shape),
                  const_spec((HEAD_DIM, ts)), const_spec((1, LANES)), const_spec((1, LANES))]
                 + [const_spec(p.shape) for p in lru_params],
        out_specs=out_specs,
        scratch_shapes=[pltpu.VMEM((SUBLANES, lru_w), F32), pltpu.VMEM((SUBLANES, lru_w), F32)],
        compiler_params=pltpu.CompilerParams(dimension_semantics=("parallel", "arbitrary"),
                                             vmem_limit_bytes=VMEM_LIMIT),
        name="in_proj_lru",
    )(x, mod3, norm_g, w_big, w_t, gq_b, gks2, gkw2, *lru_params)


def _compress_kernel(is_key, c_ref, w1x_ref, pos_ref, w1_ref, w2lo_ref, w2hi_ref, g_ref, o_ref):
    nblk = c_ref.shape[1]
    hid = w1_ref.shape[1]
    ab = jnp.dot(c_ref[0], w1x_ref[...], preferred_element_type=F32)
    const = jnp.dot(pos_ref[...], w1_ref[...], preferred_element_type=F32, precision=HIGHEST)[0:1, :]
    acts = []
    for hk in range(NSA_KV_HEADS):
        first = ab[:, (2 * hk) * hid:(2 * hk + 1) * hid]
        second = ab[:, (2 * hk + 1) * hid:(2 * hk + 2) * hid]
        hidden = first + pltpu.roll(second, nblk - 1, 0) + const
        acts.append(jax.nn.gelu(hidden).astype(BF16))
    if is_key:
        out = (jnp.dot(acts[0], w2lo_ref[...], preferred_element_type=F32)
               + jnp.dot(acts[1], w2hi_ref[...], preferred_element_type=F32))
        o_ref[0, :, 0:LANES] = _half_rms(out, g_ref[...]).astype(BF16)
        cend = CMP_STRIDE * lax.broadcasted_iota(jnp.int32, (nblk, 1), 0) + (CMP_BLOCK - 1)
        o_ref[0, :, LANES:2 * LANES] = _key_features(cend, False).astype(BF16)
    else:
        out_t = _nt_dot(w2lo_ref[...], acts[0]) + _nt_dot(w2hi_ref[...], acts[1])
        o_ref[0] = out_t.astype(BF16)


def _compress(chunks, w1x, pos_flat, w1, w2lo, w2hi, gain2, is_key):
    bsz, nblk, width = chunks.shape
    hid = w1.shape[1]

    def const_spec(shape):
        return pl.BlockSpec(shape, lambda b: (0,) * len(shape))

    out_tail = (nblk, 2 * LANES) if is_key else (LANES, nblk)
    return pl.pallas_call(
        functools.partial(_compress_kernel, is_key),
        out_shape=jax.ShapeDtypeStruct((bsz,) + out_tail, BF16),
        grid=(bsz,),
        in_specs=[pl.BlockSpec((1, nblk, width), lambda b: (b, 0, 0)),
                  const_spec((width, 4 * hid)), const_spec((8, w1.shape[0])), const_spec(w1.shape),
                  const_spec(w2lo.shape), const_spec(w2hi.shape), const_spec((1, LANES))],
        out_specs=pl.BlockSpec((1,) + out_tail, lambda b: (b, 0, 0)),
        compiler_params=pltpu.CompilerParams(dimension_semantics=("parallel",), vmem_limit_bytes=VMEM_LIMIT),
        name="compress_k" if is_key else "compress_v",
    )(chunks, w1x, pos_flat, w1, w2lo, w2hi, gain2)


ONES_ROWS = 16


def _with_ones(vt):
    return jnp.concatenate([vt, jnp.ones((ONES_ROWS, vt.shape[1]), BF16)], axis=0)


def _online_update(carry, s, top, vt):
    m, acc = carry
    m_new = jnp.maximum(m, top)
    p = jnp.exp2(s - m_new)
    acc = jnp.exp2(m - m_new) * acc + jnp.dot(_with_ones(vt), p.astype(BF16), preferred_element_type=F32)
    return m_new, acc


def _normalised(acc):
    return acc[0:HEAD_DIM] * (1.0 / acc[HEAD_DIM:HEAD_DIM + 1])


def _attn_kernel(qt_ref, kc_ref, vct_ref, ks_ref, vst_ref, kw_ref, vwt_ref, gt_ref, ovt_ref,
                 o_ref, impt_ref, s_ref):
    tq = qt_ref.shape[2]
    n_cmp = kc_ref.shape[1]
    n_sel = impt_ref.shape[1]
    ncol = NSA_GROUP * tq
    per_wide = SLC_KEYS // KEY_CHUNK
    qi = pl.program_id(1)
    t0 = qi * tq
    tloc = lax.broadcasted_iota(jnp.int32, (1, tq), 1)
    tcol = jnp.concatenate([tloc] * NSA_GROUP, axis=1)
    tpos = t0 + tloc
    nrow = lax.broadcasted_iota(jnp.int32, (n_sel, 1), 0)
    cur = jnp.right_shift(tpos, SEL_SHIFT)
    forced = jnp.logical_or(jnp.logical_or(nrow == 0, nrow == cur), nrow == cur - 1)
    frow = lax.broadcasted_iota(jnp.int32, (HEAD_DIM, 1), 0)
    srow = lax.broadcasted_iota(jnp.int32, (SUBLANES, 1), 0)
    zeros_q = jnp.zeros((HEAD_DIM, tq), BF16)
    row_minus_col = lax.broadcasted_iota(jnp.int32, (KEY_CHUNK, 1), 0) - tcol
    own_ok = row_minus_col <= 0

    cidx = lax.broadcasted_iota(jnp.int32, (n_cmp, 1), 0)
    cend = CMP_STRIDE * cidx + (CMP_BLOCK - 1)
    cmp_ok = jnp.logical_and(cend <= t0 + tcol, cidx < n_cmp - 1)
    col_has_cmp = (t0 + tcol) >= CMP_BLOCK - 1
    gates = gt_ref[0]
    kv_heads = range(NSA_KV_HEADS)

    def values(ref, hk, first_chunk, n):
        return jnp.concatenate([ref[0, first_chunk + i, hk * HEAD_DIM:(hk + 1) * HEAD_DIM, :]
                                for i in range(n)], axis=1)

    not_past = jnp.logical_and(frow >= FEAT_CHUNK - SEL_BLOCK, frow - (FEAT_CHUNK - SEL_BLOCK) >= qi)

    q_plain, q_plain_past, q_diag, q_past, o_cmp = [], [], [], [], []
    for hk in kv_heads:
        q_rows, feat_rows, feat_rows_past = [], [], []
        for g in range(NSA_GROUP):
            hd = hk * NSA_GROUP + g
            qh = qt_ref[0, hd * HEAD_DIM:(hd + 1) * HEAD_DIM, :]
            q_rows.append([qh, zeros_q] if hk == 0 else [zeros_q, qh])
            c = _alibi_coef(hd)
            a = jnp.zeros((HEAD_DIM, 1), F32)
            for i, term in enumerate(_bf16_terms(c)):
                a = jnp.where(frow == FEAT_BLK - SEL_BLOCK + i, SEL_BLOCK * term, a)
                a = jnp.where(frow == FEAT_OFF - SEL_BLOCK + i, term, a)
            a = jnp.where(frow == FEAT_ONE - SEL_BLOCK, -c * t0.astype(F32), a)
            feat_rows.append(jnp.broadcast_to(a, (HEAD_DIM, tq)).astype(BF16))
            feat_rows_past.append(jnp.broadcast_to(jnp.where(not_past, NEG_INF, a), (HEAD_DIM, tq)).astype(BF16))

        def q_operand(pen, feats, q_rows=q_rows):
            return jnp.concatenate([jnp.concatenate(q_rows[g] + [pen, feats[g]], axis=0)
                                    for g in range(NSA_GROUP)], axis=1)

        q_plain.append(q_operand(zeros_q, feat_rows))
        q_plain_past.append(q_operand(zeros_q, feat_rows_past))

        s = jnp.dot(kc_ref[0], q_plain[hk], preferred_element_type=F32)
        s = jnp.where(cmp_ok, s, NEG_INF)
        e = jnp.exp2(s - jnp.max(s, axis=0, keepdims=True))
        p = e * jnp.where(col_has_cmp, 1.0 / jnp.sum(e, axis=0, keepdims=True), 0.0)
        o_cmp.append(jnp.dot(vct_ref[0, hk * HEAD_DIM:(hk + 1) * HEAD_DIM, :], p.astype(BF16),
                             preferred_element_type=F32))
        p_sum = p[:, 0:tq]
        for g in range(1, NSA_GROUP):
            p_sum = p_sum + p[:, g * tq:(g + 1) * tq]
        imp = jnp.zeros((n_sel, tq), F32)
        rest = p_sum
        for _ in range(N_SPLIT):
            term = rest.astype(BF16)
            imp = imp + jnp.dot(ovt_ref[...], term, preferred_element_type=F32)
            rest = rest - term.astype(F32)
        imp = jnp.where(forced, FORCE, jnp.where(nrow > cur, -FORCE, imp))

        impt_ref[hk] = imp
        groups = [imp[b * SUBLANES:(b + 1) * SUBLANES] for b in range(n_sel // SUBLANES)]
        ranks = [jnp.zeros((SUBLANES, tq), F32) for _ in groups]
        for m in range(n_sel):
            other = impt_ref[hk, m:m + 1, :]
            for b, grp in enumerate(groups):
                if b * SUBLANES > m:
                    ahead = other >= grp
                elif (b + 1) * SUBLANES <= m:
                    ahead = other > grp
                else:
                    ahead = jnp.logical_or(other > grp, jnp.logical_and(other == grp, srow > m - b * SUBLANES))
                ranks[b] = ranks[b] + jnp.where(ahead, 1.0, 0.0)
        rank = jnp.concatenate(ranks, axis=0)
        pen = jnp.where(rank < float(SEL_TOPK), 0.0, NEG_INF).astype(BF16)
        q_diag.append(q_operand(pen, feat_rows))
        q_past.append(q_operand(pen, feat_rows_past))

    init = (jnp.full((1, ncol), NEG_INF, F32), jnp.zeros((HEAD_DIM + ONES_ROWS, ncol), F32))

    n_back = WINDOW // KEY_CHUNK
    wc0 = jnp.maximum(qi - n_back, 0)
    lead = t0 - wc0 * KEY_CHUNK
    k_back = kw_ref[0, pl.ds(pl.multiple_of(wc0 * KEY_CHUNK, KEY_CHUNK), WINDOW), :]
    k_tile = kw_ref[0, pl.ds(pl.multiple_of(t0, KEY_CHUNK), KEY_CHUNK), :]
    recent = row_minus_col > lead - WINDOW
    o_win = []
    for hk in kv_heads:
        s_back = jnp.dot(k_back, q_plain_past[hk], preferred_element_type=F32)
        s_own = jnp.where(own_ok, jnp.dot(k_tile, q_plain[hk], preferred_element_type=F32), NEG_INF)
        s = jnp.concatenate([jnp.where(recent, s_back[0:KEY_CHUNK], NEG_INF), s_back[KEY_CHUNK:], s_own], axis=0)
        vt = jnp.concatenate([values(vwt_ref, hk, wc0, n_back), values(vwt_ref, hk, qi, 1)], axis=1)
        _, acc = _online_update(init, s, jnp.max(s, axis=0, keepdims=True), vt)
        o_win.append(_normalised(acc))

    k_own = ks_ref[0, pl.ds(pl.multiple_of(t0, KEY_CHUNK), KEY_CHUNK), :]
    stats = []
    for hk in kv_heads:
        s = jnp.where(own_ok, jnp.dot(k_own, q_diag[hk], preferred_element_type=F32), NEG_INF)
        stats.append(_online_update(init, s, jnp.max(s, axis=0, keepdims=True), values(vst_ref, hk, qi, 1)))
    stats = tuple(stats)

    def produce(j, slot):
        k = ks_ref[0, j * SLC_KEYS:(j + 1) * SLC_KEYS, :]
        tops = []
        for hk in kv_heads:
            s = jnp.dot(k, q_past[hk], preferred_element_type=F32)
            s_ref[slot, hk] = s
            tops.append(jnp.max(s, axis=0, keepdims=True))
        return tuple(tops)

    def consume(j, slot, tops, stats):
        return tuple(_online_update(stats[hk], s_ref[slot, hk], tops[hk],
                                    values(vst_ref, hk, j * per_wide, per_wide)) for hk in kv_heads)

    def chain(n_blocks):
        def run(stats):
            tops = produce(0, 0) if n_blocks else None
            for j in range(n_blocks):
                nxt = produce(j + 1, (j + 1) % 2) if j + 1 < n_blocks else None
                stats = consume(j, j % 2, tops, stats)
                tops = nxt
            return stats
        return run

    stats = lax.switch((t0 + SLC_KEYS - 1) // SLC_KEYS,
                       [chain(n) for n in range(ks_ref.shape[1] // SLC_KEYS + 1)], stats)
    o_slc = [_normalised(acc) for _, acc in stats]

    for hk in kv_heads:
        for g in range(NSA_GROUP):
            hd = hk * NSA_GROUP + g
            tot = jnp.zeros((HEAD_DIM, tq), F32)
            for br, o_br in enumerate((o_cmp[hk], o_slc[hk], o_win[hk])):
                r = br * NSA_HEADS + hd
                tot = tot + gates[r:r + 1, :] * o_br[:, g * tq:(g + 1) * tq]
            o_ref[0, hd * HEAD_DIM:(hd + 1) * HEAD_DIM, :] = tot.astype(BF16)


def _attn(qt, kc, vct, kslc, vst, kwin, vwt, gt, overlap_t):
    bsz, nsa_w, seq = qt.shape
    tq = ATTN_ROWS
    n_cmp = kc.shape[1]
    n_sel = seq // SEL_BLOCK
    n_gate = gt.shape[1]
    assert seq % (2 * SLC_KEYS) == 0 and n_sel <= HALF and seq >= WINDOW + tq
    assert seq // KEY_CHUNK <= LANES - FEAT_CHUNK and n_sel % SUBLANES == 0

    def per_batch(shape):
        return pl.BlockSpec((1,) + shape, lambda b, i: (b,) + (0,) * len(shape))

    def const_spec(shape):
        return pl.BlockSpec(shape, lambda b, i: (0,) * len(shape))

    return pl.pallas_call(
        _attn_kernel,
        out_shape=jax.ShapeDtypeStruct((bsz, nsa_w, seq), BF16),
        grid=(bsz, seq // tq),
        in_specs=[pl.BlockSpec((1, nsa_w, tq), lambda b, i: (b, 0, i)),
                  per_batch((n_cmp, 2 * LANES)), per_batch((LANES, n_cmp)),
                  per_batch((seq, 2 * LANES)), per_batch((seq // KEY_CHUNK, LANES, KEY_CHUNK)),
                  per_batch((seq, 2 * LANES)), per_batch((seq // KEY_CHUNK, LANES, KEY_CHUNK)),
                  pl.BlockSpec((1, n_gate, tq), lambda b, i: (b, 0, i)),
                  const_spec(overlap_t.shape)],
        out_specs=pl.BlockSpec((1, nsa_w, tq), lambda b, i: (b, 0, i)),
        scratch_shapes=[pltpu.VMEM((NSA_KV_HEADS, n_sel, tq), F32),
                        pltpu.VMEM((2, NSA_KV_HEADS, SLC_KEYS, NSA_GROUP * tq), F32)],
        compiler_params=pltpu.CompilerParams(dimension_semantics=("parallel", "arbitrary"),
                                             vmem_limit_bytes=VMEM_LIMIT),
        name="nsa_attn",
    )(qt, kc, vct, kslc, vst, kwin, vwt, gt, overlap_t)


def _out_kernel(x_ref, ylru_ref, onsat_ref, znsat_ref, gn_ref, gate_ref, wl_ref, wn_ref, o_ref):
    o = onsat_ref[0].astype(F32)
    ms = jnp.mean(o * o, axis=0, keepdims=True)
    z = znsat_ref[0].astype(F32)
    y_t = (o * lax.rsqrt(ms + EPS) * gn_ref[...]) * (z * jax.nn.sigmoid(z))
    y_nsa = jnp.transpose(y_t).astype(BF16)
    out = (jnp.dot(ylru_ref[0], wl_ref[...], preferred_element_type=F32)
           + jnp.dot(y_nsa, wn_ref[...], preferred_element_type=F32))
    o_ref[0] = x_ref[0] + gate_ref[0, 2:3, :] * out


def _out(x, ylru, onsat, znsat, g_nsa_b, mod3, w_lru, w_nsa):
    bsz, seq, d = x.shape
    ts = OUT_ROWS
    nsa_w = onsat.shape[1]

    def tok_spec(n):
        return pl.BlockSpec((1, ts, n), lambda b, s: (b, s, 0))

    def feat_spec(n):
        return pl.BlockSpec((1, n, ts), lambda b, s: (b, 0, s))

    def const_spec(shape):
        return pl.BlockSpec(shape, lambda b, s: (0,) * len(shape))

    return pl.pallas_call(
        _out_kernel,
        out_shape=jax.ShapeDtypeStruct((bsz, seq, d), F32),
        grid=(bsz, seq // ts),
        in_specs=[tok_spec(d), tok_spec(ylru.shape[2]), feat_spec(nsa_w), feat_spec(nsa_w),
                  const_spec((nsa_w, ts)),
                  pl.BlockSpec((1, 3, d), lambda b, s: (b, 0, 0)),
                  const_spec(w_lru.shape), const_spec(w_nsa.shape)],
        out_specs=tok_spec(d),
        compiler_params=pltpu.CompilerParams(dimension_semantics=("parallel", "parallel"),
                                             vmem_limit_bytes=VMEM_LIMIT),
        name="out_proj",
    )(x, ylru, onsat, znsat, g_nsa_b, mod3, w_lru, w_nsa)


def _overlap_t(n_cmp_pad, n_sel):
    ratio = SEL_BLOCK // CMP_STRIDE
    ov = np.zeros((n_sel, n_cmp_pad), np.float32)
    for c in range(n_cmp_pad - 1):
        for n in (c // ratio, (c + 1) // ratio):
            if n < n_sel:
                ov[n, c] += 1.0
    return ov


def _block_diag_pairs(w):
    nb, bs, _ = w.shape
    z = jnp.zeros((bs, bs), w.dtype)
    return jnp.stack([jnp.block([[w[2 * s], z], [z, w[2 * s + 1]]]) for s in range(nb // 2)])


def _compress_weights(w1, w2, transposed):
    hid = w1.shape[1]
    w1r = w1.reshape(2, CMP_STRIDE, HEAD_DIM, hid)
    parts = []
    for hk in range(NSA_KV_HEADS):
        for half in range(2):
            slot = jnp.zeros((CMP_STRIDE, NSA_KV_HEADS, HEAD_DIM, hid), w1.dtype).at[:, hk].set(w1r[half])
            parts.append(slot.reshape(CMP_STRIDE * LANES, hid))
    w1x = jnp.concatenate(parts, axis=1).astype(BF16)
    zpad = jnp.zeros_like(w2)
    w2lo = jnp.concatenate([w2, zpad], axis=1).astype(BF16)
    w2hi = jnp.concatenate([zpad, w2], axis=1).astype(BF16)
    if transposed:
        w2lo, w2hi = w2lo.T, w2hi.T
    return w1x, w2lo, w2hi


def _pad_to(n, m):
    return -(-n // m) * m


def _layer(x, c, w_ada, b_ada, norm_g, w_in, conv_w, conv_b, w_rg_a, b_rg_a, w_rg_x, b_rg_x, lru_lambda,
           cmp_pos_k, cmp_w1_k, cmp_w2_k, cmp_pos_v, cmp_w1_v, cmp_w2_v, g_q, g_k_cmp, g_k_slc, g_k_win,
           g_out_lru, g_out_nsa, w_out):
    bsz, seq, d = x.shape
    lru_w = d // 2
    nsa_w = NSA_HEADS * HEAD_DIM
    kv_w = NSA_KV_HEADS * HEAD_DIM
    n_sel = seq // SEL_BLOCK

    splits = (lru_w, lru_w, nsa_w, kv_w, kv_w, kv_w, kv_w, kv_w, kv_w, N_BRANCH * NSA_HEADS, nsa_w)
    offs = np.concatenate([[0], np.cumsum(splits)])
    names = ("xlru", "zlru", "q", "kcmp", "vcmp", "kslc", "vslc", "kwin", "vwin", "gates", "znsa")
    src = {n: (int(offs[i]), int(offs[i + 1])) for i, n in enumerate(names)}

    def plan(group, align):
        pieces, where, at = [], {}, 0
        for n in group:
            piece = w_in[:, src[n][0]:src[n][1]]
            width = _pad_to(piece.shape[1], align)
            if width != piece.shape[1]:
                piece = jnp.pad(piece, ((0, 0), (0, width - piece.shape[1])))
            pieces.append(piece)
            where[n] = (at, width)
            at += width
        return jnp.concatenate(pieces, axis=1).astype(BF16), where

    w_big, cols = plan(("xlru", "zlru", "kcmp", "vcmp", "kslc", "kwin"), LANES)
    w_tt, rows_t = plan(("q", "vslc", "vwin", "gates", "znsa"), 32)
    w_t = w_tt.T

    def dup(g):
        return jnp.concatenate([g, g]).reshape(1, LANES)

    gq_b = jnp.broadcast_to((g_q * (ATTN_SCALE * LOG2E)).reshape(HEAD_DIM, 1), (HEAD_DIM, PROJ_ROWS))
    mod3 = _mod(c, w_ada, b_ada).reshape(bsz, 3, d)
    lru_params = (conv_w, conv_b.reshape(1, lru_w),
                  _block_diag_pairs(w_rg_a).astype(BF16), b_rg_a.reshape(1, lru_w),
                  _block_diag_pairs(w_rg_x).astype(BF16), b_rg_x.reshape(1, lru_w),
                  lru_lambda.reshape(1, lru_w), g_out_lru.reshape(1, lru_w))
    (ylru, kcmp, vcmp, kslc, kwin, qt, vst, vwt, gt, znsat) = _proj(
        x, mod3, norm_g.reshape(1, d), w_big, cols, w_t, rows_t, gq_b, dup(g_k_slc), dup(g_k_win), lru_params)

    n_chunk = seq // CMP_STRIDE
    w1x_k, w2lo_k, w2hi_k = _compress_weights(cmp_w1_k, cmp_w2_k, False)
    w1x_v, w2lo_v, w2hi_v = _compress_weights(cmp_w1_v, cmp_w2_v, True)

    def pos_rows(pos):
        return jnp.broadcast_to(pos.reshape(1, CMP_BLOCK * HEAD_DIM), (8, CMP_BLOCK * HEAD_DIM))

    kc = _compress(kcmp.reshape(bsz, n_chunk, CMP_STRIDE * LANES), w1x_k, pos_rows(cmp_pos_k), cmp_w1_k,
                   w2lo_k, w2hi_k, dup(g_k_cmp), True)
    vct = _compress(vcmp.reshape(bsz, n_chunk, CMP_STRIDE * LANES), w1x_v, pos_rows(cmp_pos_v), cmp_w1_v,
                    w2lo_v, w2hi_v, dup(g_k_cmp), False)

    onsat = _attn(qt, kc, vct, kslc, vst, kwin, vwt, gt, jnp.asarray(_overlap_t(n_chunk, n_sel), dtype=BF16))

    g_nsa_b = jnp.broadcast_to(g_out_nsa.reshape(nsa_w, 1), (nsa_w, OUT_ROWS))
    return _out(x, ylru, onsat, znsat, g_nsa_b, mod3, w_out[:lru_w].astype(BF16), w_out[lru_w:].astype(BF16))


def kernel(x, c, w_ada, b_ada, norm_g, w_in, conv_w, conv_b, w_rg_a, b_rg_a, w_rg_x, b_rg_x, lru_lambda, cmp_pos_k, cmp_w1_k, cmp_w2_k, cmp_pos_v, cmp_w1_v, cmp_w2_v, g_q, g_k_cmp, g_k_slc, g_k_win, g_out_lru, g_out_nsa, w_out):
    params = (w_ada, b_ada, norm_g, w_in, conv_w, conv_b, w_rg_a, b_rg_a, w_rg_x, b_rg_x, lru_lambda,
              cmp_pos_k, cmp_w1_k, cmp_w2_k, cmp_pos_v, cmp_w1_v, cmp_w2_v, g_q, g_k_cmp, g_k_slc, g_k_win,
              g_out_lru, g_out_nsa, w_out)
    for layer in range(w_in.shape[0]):
        x = _layer(x, c, *(p[layer] for p in params))
    return x
```

```python
import functools
import math

import ml_dtypes
import numpy as np
import jax
import jax.numpy as jnp
from jax import lax
from jax.experimental import pallas as pl
from jax.experimental.pallas import tpu as pltpu

F32 = jnp.float32
BF16 = jnp.bfloat16
HIGHEST = lax.Precision.HIGHEST

LANES = 128
HALF = LANES // 2
SUBLANES = 8

LRU_BLOCKS = 8
CONV_WIDTH = 4
LRU_C = 8.0
NSA_HEADS = 8
HEAD_DIM = 64
NSA_KV_HEADS = 2
NSA_GROUP = NSA_HEADS // NSA_KV_HEADS
N_BRANCH = 3
CMP_STRIDE = 16
CMP_BLOCK = 2 * CMP_STRIDE
SEL_BLOCK = 64
SEL_SHIFT = SEL_BLOCK.bit_length() - 1
SEL_TOPK = 16
WINDOW = 512
ATTN_SCALE = HEAD_DIM ** -0.5
LOG2E = math.log2(math.e)
NEG_INF = -1e30
FORCE = 1e6
EPS = 1e-6

PROJ_ROWS = 1024
OUT_ROWS = 1024
ATTN_ROWS = 128
SLC_KEYS = 512
KEY_CHUNK = 128
VMEM_LIMIT = 56 * 1024 * 1024

FEAT_ONEHOT = 0
FEAT_BLK = SEL_BLOCK
FEAT_OFF = SEL_BLOCK + 3
FEAT_ONE = SEL_BLOCK + 6
FEAT_CHUNK = LANES - 32
N_SPLIT = 3

assert HEAD_DIM == HALF and NSA_KV_HEADS == 2 and ATTN_ROWS == LANES and KEY_CHUNK == LANES


def _bf16_terms(value):
    terms, rest = [], np.float32(value)
    for _ in range(N_SPLIT):
        t = np.float32(rest).astype(ml_dtypes.bfloat16).astype(np.float32)
        terms.append(float(t))
        rest = np.float32(rest - t)
    return terms


def _alibi_coef(head):
    return float(2.0 ** (-8.0 * (head + 1) / NSA_HEADS)) * LOG2E


def _nt_dot(a, b):
    return lax.dot_general(a, b, (((1,), (1,)), ((), ())), preferred_element_type=F32)


def _half_rms(x, gain2):
    lane = lax.broadcasted_iota(jnp.int32, (1, LANES), 1)
    lo = lane < HALF
    sq = x * x
    ss_lo = jnp.sum(jnp.where(lo, sq, 0.0), axis=-1, keepdims=True)
    ss_hi = jnp.sum(jnp.where(lo, 0.0, sq), axis=-1, keepdims=True)
    r = jnp.where(lo, lax.rsqrt(ss_lo * (1.0 / HALF) + EPS), lax.rsqrt(ss_hi * (1.0 / HALF) + EPS))
    return x * r * gain2


def _key_features(pos, onehot):
    lane = lax.broadcasted_iota(jnp.int32, (1, LANES), 1)
    blk = jnp.right_shift(pos, SEL_SHIFT)
    off = jnp.bitwise_and(pos, SEL_BLOCK - 1)
    f = jnp.where(jnp.logical_and(lane >= FEAT_BLK, lane < FEAT_OFF), blk.astype(F32),
                  jnp.where(jnp.logical_and(lane >= FEAT_OFF, lane < FEAT_ONE), off.astype(F32),
                            jnp.where(lane == FEAT_ONE, 1.0, 0.0)))
    if onehot:
        f = jnp.where(lane < SEL_BLOCK, jnp.where(lane == blk, 1.0, 0.0), f)
        chunk = jnp.right_shift(pos, KEY_CHUNK.bit_length() - 1)
        f = jnp.where(lane >= FEAT_CHUNK, jnp.where(lane - FEAT_CHUNK == chunk, 1.0, 0.0), f)
    return f


def _mod_kernel(c_ref, w_ref, b_ref, o_ref):
    c = c_ref[...]
    o_ref[...] = jnp.dot(c * jax.nn.sigmoid(c), w_ref[...], preferred_element_type=F32, precision=HIGHEST) + b_ref[...]


def _mod(c, w_ada, b_ada):
    bsz, d = c.shape
    n = w_ada.shape[1]
    return pl.pallas_call(
        _mod_kernel,
        out_shape=jax.ShapeDtypeStruct((bsz, n), F32),
        grid=(n // d,),
        in_specs=[pl.BlockSpec((bsz, d), lambda j: (0, 0)),
                  pl.BlockSpec((d, d), lambda j: (0, j)),
                  pl.BlockSpec((1, d), lambda j: (0, j))],
        out_specs=pl.BlockSpec((bsz, d), lambda j: (0, j)),
        name="adaln_mod",
    )(c, w_ada, b_ada.reshape(1, n))


def _log1p(y):
    w = 1.0 + y
    return jnp.where(w == 1.0, y, jnp.log(w) * (y / (w - 1.0)))


def _softplus(x):
    return jnp.maximum(x, 0.0) + _log1p(jnp.exp(-jnp.abs(x)))


def _sigmoid(x):
    return 0.5 * jnp.tanh(0.5 * x) + 0.5


def _lru_gates(x, first, cw_ref, cb_ref, wa_ref, ba_ref, wx_ref, bx_ref, lam_ref, tail_ref, h_ref):
    rows, width = x.shape

    @pl.when(first)
    def _():
        tail_ref[...] = jnp.zeros_like(tail_ref)
        h_ref[...] = jnp.zeros_like(h_ref)

    tail = tail_ref[...]
    tail_ref[...] = x[rows - SUBLANES:rows]
    row8 = lax.broadcasted_iota(jnp.int32, (SUBLANES, 1), 0)
    xc = cb_ref[...] + cw_ref[CONV_WIDTH - 1:CONV_WIDTH, :] * x
    for k in range(1, CONV_WIDTH):
        xs = pltpu.roll(x, k, 0)
        head = jnp.where(row8 < k, pltpu.roll(tail, k, 0), xs[0:SUBLANES])
        xs = jnp.concatenate([head, xs[SUBLANES:]], axis=0)
        xc = xc + cw_ref[CONV_WIDTH - 1 - k:CONV_WIDTH - k, :] * xs

    xcb = xc.astype(BF16)
    nslot = width // LANES
    ra = jnp.concatenate([jnp.dot(xcb[:, s * LANES:(s + 1) * LANES], wa_ref[s], preferred_element_type=F32)
                          for s in range(nslot)], axis=1)
    ri = jnp.concatenate([jnp.dot(xcb[:, s * LANES:(s + 1) * LANES], wx_ref[s], preferred_element_type=F32)
                          for s in range(nslot)], axis=1)
    r = _sigmoid(ra + ba_ref[...])
    i = _sigmoid(ri + bx_ref[...])
    log_a = (-LRU_C) * r * _softplus(-lam_ref[...])
    a = jnp.exp(log_a)
    y = -jnp.tanh(log_a) * (a * a + 1.0)
    return a, jnp.where(y > 0.0, y * lax.rsqrt(y), 0.0) * (i * xc)


def _lru_scan_groups(a, u):
    rows = a.shape[0]
    sub = jnp.bitwise_and(lax.broadcasted_iota(jnp.int32, (rows, 1), 0), SUBLANES - 1)
    acc_a, acc_h = a, u
    d = 1
    while d < SUBLANES:
        keep = sub >= d
        sh_a = pltpu.roll(acc_a, d, 0)
        sh_h = pltpu.roll(acc_h, d, 0)
        acc_h = jnp.where(keep, acc_a * sh_h + acc_h, acc_h)
        acc_a = jnp.where(keep, acc_a * sh_a, acc_a)
        d *= 2
    return acc_a, acc_h


def _lru_carry(acc_a, acc_h, h_ref):
    rows = acc_a.shape[0]
    carry = h_ref[0:1, :]
    groups = []
    for g in range(rows // SUBLANES):
        hg = acc_h[g * SUBLANES:(g + 1) * SUBLANES] + acc_a[g * SUBLANES:(g + 1) * SUBLANES] * carry
        carry = hg[SUBLANES - 1:SUBLANES]
        groups.append(hg)
    h_ref[0:1, :] = carry
    return jnp.concatenate(groups, axis=0)


def _lru_out(h, z, g_ref):
    ms = jnp.mean(h * h, axis=-1, keepdims=True)
    return (h * lax.rsqrt(ms + EPS) * g_ref[...]) * (z * _sigmoid(z))


def _proj_kernel(cols, rows_t, x_ref, mod_ref, ng_ref, w_ref, wt_ref, gq_ref, gks_ref, gkw_ref,
                 cw_ref, cb_ref, wa_ref, ba_ref, wx_ref, bx_ref, lam_ref, gl_ref,
                 ylru_ref, kcmp_ref, vcmp_ref, kslc_ref, kwin_ref,
                 qt_ref, vst_ref, vwt_ref, gt_ref, znsat_ref, tail_ref, h_ref):
    rows = x_ref.shape[1]
    x = x_ref[0]
    ms = jnp.mean(x * x, axis=-1, keepdims=True)
    gain = ng_ref[...] * (1.0 + mod_ref[0, 1:2, :])
    h = (x * lax.rsqrt(ms + EPS)) * gain + mod_ref[0, 0:1, :]
    hb = h.astype(BF16)

    def mm(name):
        c0, n = cols[name]
        return jnp.dot(hb, w_ref[:, c0:c0 + n], preferred_element_type=F32)

    def mmt(name):
        r0, n = rows_t[name]
        return _nt_dot(wt_ref[r0:r0 + n, :], hb)

    def after(value, done):
        return value + jnp.minimum(jnp.abs(done[done.shape[0] - 1:, 0:1]), 0.0)

    lru_a, lru_u = _lru_gates(mm("xlru"), pl.program_id(1) == 0, cw_ref, cb_ref, wa_ref, ba_ref,
                              wx_ref, bx_ref, lam_ref, tail_ref, h_ref)
    z_lru = mm("zlru")
    kcmp_ref[0] = mm("kcmp")
    vcmp_ref[0] = mm("vcmp")

    pos = pl.program_id(1) * rows + lax.broadcasted_iota(jnp.int32, (rows, 1), 0)
    kslc_ref[0, :, 0:LANES] = _half_rms(mm("kslc"), gks_ref[...]).astype(BF16)
    kslc_ref[0, :, LANES:2 * LANES] = _key_features(pos, True).astype(BF16)
    k_win = mm("kwin")
    kwin_ref[0, :, 0:LANES] = _half_rms(k_win, gkw_ref[...]).astype(BF16)
    kwin_ref[0, :, LANES:2 * LANES] = _key_features(pos, True).astype(BF16)

    lru_a, lru_h = _lru_scan_groups(lru_a, after(lru_u, k_win))
    qt = mmt("q")
    for hd in range(NSA_HEADS):
        qh = qt[hd * HEAD_DIM:(hd + 1) * HEAD_DIM]
        r = lax.rsqrt(jnp.mean(qh * qh, axis=0, keepdims=True) + EPS)
        qt_ref[0, hd * HEAD_DIM:(hd + 1) * HEAD_DIM, :] = (qh * r * gq_ref[...]).astype(BF16)

    h_lru = _lru_carry(lru_a, after(lru_h, qt), h_ref)
    vst = mmt("vslc").astype(BF16)
    vwt = mmt("vwin").astype(BF16)
    for j in range(rows // KEY_CHUNK):
        vst_ref[0, j] = vst[:, j * KEY_CHUNK:(j + 1) * KEY_CHUNK]
        vwt_ref[0, j] = vwt[:, j * KEY_CHUNK:(j + 1) * KEY_CHUNK]
    gt_ref[0] = jax.nn.sigmoid(mmt("gates"))
    z_nsa = mmt("znsa")
    znsat_ref[0] = z_nsa.astype(BF16)
    ylru_ref[0] = _lru_out(after(h_lru, z_nsa), z_lru, gl_ref).astype(BF16)


def _proj(x, mod3, norm_g, w_big, cols, w_t, rows_t, gq_b, gks2, gkw2, lru_params):
    bsz, seq, d = x.shape
    ts = PROJ_ROWS
    nsa_w = NSA_HEADS * HEAD_DIM
    lru_w = cols["xlru"][1]
    n_gate = rows_t["gates"][1]

    def tok_spec(n):
        return pl.BlockSpec((1, ts, n), lambda b, s: (b, s, 0))

    def feat_spec(n):
        return pl.BlockSpec((1, n, ts), lambda b, s: (b, 0, s))

    def chunk_spec():
        return pl.BlockSpec((1, ts // KEY_CHUNK, LANES, KEY_CHUNK), lambda b, s: (b, s, 0, 0))

    def const_spec(shape):
        return pl.BlockSpec(shape, lambda b, s: (0,) * len(shape))

    out_shape = (
        jax.ShapeDtypeStruct((bsz, seq, lru_w), BF16),
        jax.ShapeDtypeStruct((bsz, seq, LANES), F32),
        jax.ShapeDtypeStruct((bsz, seq, LANES), F32),
        jax.ShapeDtypeStruct((bsz, seq, 2 * LANES), BF16),
        jax.ShapeDtypeStruct((bsz, seq, 2 * LANES), BF16),
        jax.ShapeDtypeStruct((bsz, nsa_w, seq), BF16),
        jax.ShapeDtypeStruct((bsz, seq // KEY_CHUNK, LANES, KEY_CHUNK), BF16),
        jax.ShapeDtypeStruct((bsz, seq // KEY_CHUNK, LANES, KEY_CHUNK), BF16),
        jax.ShapeDtypeStruct((bsz, n_gate, seq), F32),
        jax.ShapeDtypeStruct((bsz, nsa_w, seq), BF16),
    )
    out_specs = (tok_spec(lru_w), tok_spec(LANES), tok_spec(LANES),
                 tok_spec(2 * LANES), tok_spec(2 * LANES),
                 feat_spec(nsa_w), chunk_spec(), chunk_spec(), feat_spec(n_gate), feat_spec(nsa_w))
    return pl.pallas_call(
        functools.partial(_proj_kernel, cols, rows_t),
        out_shape=out_shape,
        grid=(bsz, seq // ts),
        in_specs=[tok_spec(d),
                  pl.BlockSpec((1, 3, d), lambda b, s: (b, 0, 0)),
                  const_spec((1, d)),
                  const_spec(w_big.shape), const_spec(w_t.shape),
                  const_spec((HEAD_DIM, ts)), const_spec((1, LANES)), const_spec((1, LANES))]
                 + [const_spec(p.shape) for p in lru_params],
        out_specs=out_specs,
        scratch_shapes=[pltpu.VMEM((SUBLANES, lru_w), F32), pltpu.VMEM((SUBLANES, lru_w), F32)],
        compiler_params=pltpu.CompilerParams(dimension_semantics=("parallel", "arbitrary"),
                                             vmem_limit_bytes=VMEM_LIMIT),
        name="in_proj_lru",
    )(x, mod3, norm_g, w_big, w_t, gq_b, gks2, gkw2, *lru_params)


def _compress_kernel(is_key, c_ref, w1x_ref, pos_ref, w1_ref, w2lo_ref, w2hi_ref, g_ref, o_ref):
    nblk = c_ref.shape[1] // CMP_STRIDE
    hid = w1_ref.shape[1]
    ab = jnp.zeros((nblk, 4 * hid), F32)
    for j in range(0, CMP_STRIDE, 2):
        pair = jnp.concatenate([c_ref[0, pl.ds(j + i, nblk, stride=CMP_STRIDE), :].astype(BF16) for i in range(2)],
                               axis=1)
        ab = ab + jnp.dot(pair, w1x_ref[j * LANES:(j + 2) * LANES, :], preferred_element_type=F32)
    const = jnp.dot(pos_ref[...], w1_ref[...], preferred_element_type=F32, precision=HIGHEST)[0:1, :]
    acts = []
    for hk in range(NSA_KV_HEADS):
        first = ab[:, (2 * hk) * hid:(2 * hk + 1) * hid]
        second = ab[:, (2 * hk + 1) * hid:(2 * hk + 2) * hid]
        hidden = first + pltpu.roll(second, nblk - 1, 0) + const
        acts.append(jax.nn.gelu(hidden).astype(BF16))
    if is_key:
        out = (jnp.dot(acts[0], w2lo_ref[...], preferred_element_type=F32)
               + jnp.dot(acts[1], w2hi_ref[...], preferred_element_type=F32))
        o_ref[0, :, 0:LANES] = _half_rms(out, g_ref[...]).astype(BF16)
        cend = CMP_STRIDE * lax.broadcasted_iota(jnp.int32, (nblk, 1), 0) + (CMP_BLOCK - 1)
        o_ref[0, :, LANES:2 * LANES] = _key_features(cend, False).astype(BF16)
    else:
        out_t = _nt_dot(w2lo_ref[...], acts[0]) + _nt_dot(w2hi_ref[...], acts[1])
        o_ref[0] = out_t.astype(BF16)


def _compress(tokens, w1x, pos_flat, w1, w2lo, w2hi, gain2, is_key):
    bsz, seq, width = tokens.shape
    nblk = seq // CMP_STRIDE
    hid = w1.shape[1]

    def const_spec(shape):
        return pl.BlockSpec(shape, lambda b: (0,) * len(shape))

    out_tail = (nblk, 2 * LANES) if is_key else (LANES, nblk)
    return pl.pallas_call(
        functools.partial(_compress_kernel, is_key),
        out_shape=jax.ShapeDtypeStruct((bsz,) + out_tail, BF16),
        grid=(bsz,),
        in_specs=[pl.BlockSpec((1, seq, width), lambda b: (b, 0, 0)),
                  const_spec(w1x.shape), const_spec((8, w1.shape[0])), const_spec(w1.shape),
                  const_spec(w2lo.shape), const_spec(w2hi.shape), const_spec((1, LANES))],
        out_specs=pl.BlockSpec((1,) + out_tail, lambda b: (b, 0, 0)),
        compiler_params=pltpu.CompilerParams(dimension_semantics=("parallel",), vmem_limit_bytes=VMEM_LIMIT),
        name="compress_k" if is_key else "compress_v",
    )(tokens, w1x, pos_flat, w1, w2lo, w2hi, gain2)


ONES_ROWS = 16


def _with_ones(vt):
    return jnp.concatenate([vt, jnp.ones((ONES_ROWS, vt.shape[1]), BF16)], axis=0)


def _online_update(carry, s, top, vt):
    m, acc = carry
    m_new = jnp.maximum(m, top)
    p = jnp.exp2(s - m_new)
    acc = jnp.exp2(m - m_new) * acc + jnp.dot(_with_ones(vt), p.astype(BF16), preferred_element_type=F32)
    return m_new, acc


def _normalised(acc):
    return acc[0:HEAD_DIM] * (1.0 / acc[HEAD_DIM:HEAD_DIM + 1])


def _attn_kernel(qt_ref, kc_ref, vct_ref, ks_ref, vst_ref, kw_ref, vwt_ref, gt_ref, ovt_ref,
                 o_ref, impt_ref, s_ref):
    tq = qt_ref.shape[2]
    n_cmp = kc_ref.shape[1]
    n_sel = impt_ref.shape[1]
    ncol = NSA_GROUP * tq
    per_wide = SLC_KEYS // KEY_CHUNK
    qi = pl.program_id(1)
    t0 = qi * tq
    tloc = lax.broadcasted_iota(jnp.int32, (1, tq), 1)
    tcol = jnp.concatenate([tloc] * NSA_GROUP, axis=1)
    tpos = t0 + tloc
    nrow = lax.broadcasted_iota(jnp.int32, (n_sel, 1), 0)
    cur = jnp.right_shift(tpos, SEL_SHIFT)
    forced = jnp.logical_or(jnp.logical_or(nrow == 0, nrow == cur), nrow == cur - 1)
    frow = lax.broadcasted_iota(jnp.int32, (HEAD_DIM, 1), 0)
    srow = lax.broadcasted_iota(jnp.int32, (SUBLANES, 1), 0)
    zeros_q = jnp.zeros((HEAD_DIM, tq), BF16)
    row_minus_col = lax.broadcasted_iota(jnp.int32, (KEY_CHUNK, 1), 0) - tcol
    own_ok = row_minus_col <= 0

    cidx = lax.broadcasted_iota(jnp.int32, (n_cmp, 1), 0)
    cend = CMP_STRIDE * cidx + (CMP_BLOCK - 1)
    cmp_ok = jnp.logical_and(cend <= t0 + tcol, cidx < n_cmp - 1)
    col_has_cmp = (t0 + tcol) >= CMP_BLOCK - 1
    gates = gt_ref[0]
    kv_heads = range(NSA_KV_HEADS)

    def values(ref, hk, first_chunk, n):
        return jnp.concatenate([ref[0, first_chunk + i, hk * HEAD_DIM:(hk + 1) * HEAD_DIM, :]
                                for i in range(n)], axis=1)

    not_past = jnp.logical_and(frow >= FEAT_CHUNK - SEL_BLOCK, frow - (FEAT_CHUNK - SEL_BLOCK) >= qi)

    q_plain, q_plain_past, q_diag, q_past, o_cmp = [], [], [], [], []
    for hk in kv_heads:
        q_rows, feat_rows, feat_rows_past = [], [], []
        for g in range(NSA_GROUP):
            hd = hk * NSA_GROUP + g
            qh = qt_ref[0, hd * HEAD_DIM:(hd + 1) * HEAD_DIM, :]
            q_rows.append([qh, zeros_q] if hk == 0 else [zeros_q, qh])
            c = _alibi_coef(hd)
            a = jnp.zeros((HEAD_DIM, 1), F32)
            for i, term in enumerate(_bf16_terms(c)):
                a = jnp.where(frow == FEAT_BLK - SEL_BLOCK + i, SEL_BLOCK * term, a)
                a = jnp.where(frow == FEAT_OFF - SEL_BLOCK + i, term, a)
            a = jnp.where(frow == FEAT_ONE - SEL_BLOCK, -c * t0.astype(F32), a)
            feat_rows.append(jnp.broadcast_to(a, (HEAD_DIM, tq)).astype(BF16))
            feat_rows_past.append(jnp.broadcast_to(jnp.where(not_past, NEG_INF, a), (HEAD_DIM, tq)).astype(BF16))

        def q_operand(pen, feats, q_rows=q_rows):
            return jnp.concatenate([jnp.concatenate(q_rows[g] + [pen, feats[g]], axis=0)
                                    for g in range(NSA_GROUP)], axis=1)

        q_plain.append(q_operand(zeros_q, feat_rows))
        q_plain_past.append(q_operand(zeros_q, feat_rows_past))

        s = jnp.dot(kc_ref[0], q_plain[hk], preferred_element_type=F32)
        s = jnp.where(cmp_ok, s, NEG_INF)
        e = jnp.exp2(s - jnp.max(s, axis=0, keepdims=True))
        p = e * jnp.where(col_has_cmp, 1.0 / jnp.sum(e, axis=0, keepdims=True), 0.0)
        o_cmp.append(jnp.dot(vct_ref[0, hk * HEAD_DIM:(hk + 1) * HEAD_DIM, :], p.astype(BF16),
                             preferred_element_type=F32))
        p_sum = p[:, 0:tq]
        for g in range(1, NSA_GROUP):
            p_sum = p_sum + p[:, g * tq:(g + 1) * tq]
        imp = jnp.zeros((n_sel, tq), F32)
        rest = p_sum
        for _ in range(N_SPLIT):
            term = rest.astype(BF16)
            imp = imp + jnp.dot(ovt_ref[...], term, preferred_element_type=F32)
            rest = rest - term.astype(F32)
        imp = jnp.where(forced, FORCE, jnp.where(nrow > cur, -FORCE, imp))

        impt_ref[hk] = imp
        groups = [imp[b * SUBLANES:(b + 1) * SUBLANES] for b in range(n_sel // SUBLANES)]
        ranks = [jnp.zeros((SUBLANES, tq), F32) for _ in groups]
        for m in range(n_sel):
            other = impt_ref[hk, m:m + 1, :]
            for b, grp in enumerate(groups):
                if b * SUBLANES > m:
                    ahead = other >= grp
                elif (b + 1) * SUBLANES <= m:
                    ahead = other > grp
                else:
                    ahead = jnp.logical_or(other > grp, jnp.logical_and(other == grp, srow > m - b * SUBLANES))
                ranks[b] = ranks[b] + jnp.where(ahead, 1.0, 0.0)
        rank = jnp.concatenate(ranks, axis=0)
        pen = jnp.where(rank < float(SEL_TOPK), 0.0, NEG_INF).astype(BF16)
        q_diag.append(q_operand(pen, feat_rows))
        q_past.append(q_operand(pen, feat_rows_past))

    init = (jnp.full((1, ncol), NEG_INF, F32), jnp.zeros((HEAD_DIM + ONES_ROWS, ncol), F32))

    n_back = WINDOW // KEY_CHUNK
    wc0 = jnp.maximum(qi - n_back, 0)
    lead = t0 - wc0 * KEY_CHUNK
    k_back = kw_ref[0, pl.ds(pl.multiple_of(wc0 * KEY_CHUNK, KEY_CHUNK), WINDOW), :]
    k_tile = kw_ref[0, pl.ds(pl.multiple_of(t0, KEY_CHUNK), KEY_CHUNK), :]
    recent = row_minus_col > lead - WINDOW
    o_win = []
    for hk in kv_heads:
        s_back = jnp.dot(k_back, q_plain_past[hk], preferred_element_type=F32)
        s_own = jnp.where(own_ok, jnp.dot(k_tile, q_plain[hk], preferred_element_type=F32), NEG_INF)
        s = jnp.concatenate([jnp.where(recent, s_back[0:KEY_CHUNK], NEG_INF), s_back[KEY_CHUNK:], s_own], axis=0)
        vt = jnp.concatenate([values(vwt_ref, hk, wc0, n_back), values(vwt_ref, hk, qi, 1)], axis=1)
        _, acc = _online_update(init, s, jnp.max(s, axis=0, keepdims=True), vt)
        o_win.append(_normalised(acc))

    k_own = ks_ref[0, pl.ds(pl.multiple_of(t0, KEY_CHUNK), KEY_CHUNK), :]
    stats = []
    for hk in kv_heads:
        s = jnp.where(own_ok, jnp.dot(k_own, q_diag[hk], preferred_element_type=F32), NEG_INF)
        stats.append(_online_update(init, s, jnp.max(s, axis=0, keepdims=True), values(vst_ref, hk, qi, 1)))
    stats = tuple(stats)

    def produce(j, slot):
        k = ks_ref[0, j * SLC_KEYS:(j + 1) * SLC_KEYS, :]
        tops = []
        for hk in kv_heads:
            s = jnp.dot(k, q_past[hk], preferred_element_type=F32)
            s_ref[slot, hk] = s
            tops.append(jnp.max(s, axis=0, keepdims=True))
        return tuple(tops)

    def consume(j, slot, tops, stats):
        return tuple(_online_update(stats[hk], s_ref[slot, hk], tops[hk],
                                    values(vst_ref, hk, j * per_wide, per_wide)) for hk in kv_heads)

    def chain(n_blocks):
        def run(stats):
            tops = produce(0, 0) if n_blocks else None
            for j in range(n_blocks):
                nxt = produce(j + 1, (j + 1) % 2) if j + 1 < n_blocks else None
                stats = consume(j, j % 2, tops, stats)
                tops = nxt
            return stats
        return run

    stats = lax.switch((t0 + SLC_KEYS - 1) // SLC_KEYS,
                       [chain(n) for n in range(ks_ref.shape[1] // SLC_KEYS + 1)], stats)
    o_slc = [_normalised(acc) for _, acc in stats]

    for hk in kv_heads:
        for g in range(NSA_GROUP):
            hd = hk * NSA_GROUP + g
            tot = jnp.zeros((HEAD_DIM, tq), F32)
            for br, o_br in enumerate((o_cmp[hk], o_slc[hk], o_win[hk])):
                r = br * NSA_HEADS + hd
                tot = tot + gates[r:r + 1, :] * o_br[:, g * tq:(g + 1) * tq]
            o_ref[0, hd * HEAD_DIM:(hd + 1) * HEAD_DIM, :] = tot.astype(BF16)


def _attn(qt, kc, vct, kslc, vst, kwin, vwt, gt, overlap_t):
    bsz, nsa_w, seq = qt.shape
    tq = ATTN_ROWS
    n_cmp = kc.shape[1]
    n_sel = seq // SEL_BLOCK
    n_gate = gt.shape[1]
    assert seq % (2 * SLC_KEYS) == 0 and n_sel <= HALF and seq >= WINDOW + tq
    assert seq // KEY_CHUNK <= LANES - FEAT_CHUNK and n_sel % SUBLANES == 0

    def per_batch(shape):
        return pl.BlockSpec((1,) + shape, lambda b, i: (b,) + (0,) * len(shape))

    def const_spec(shape):
        return pl.BlockSpec(shape, lambda b, i: (0,) * len(shape))

    return pl.pallas_call(
        _attn_kernel,
        out_shape=jax.ShapeDtypeStruct((bsz, nsa_w, seq), BF16),
        grid=(bsz, seq // tq),
        in_specs=[pl.BlockSpec((1, nsa_w, tq), lambda b, i: (b, 0, i)),
                  per_batch((n_cmp, 2 * LANES)), per_batch((LANES, n_cmp)),
                  per_batch((seq, 2 * LANES)), per_batch((seq // KEY_CHUNK, LANES, KEY_CHUNK)),
                  per_batch((seq, 2 * LANES)), per_batch((seq // KEY_CHUNK, LANES, KEY_CHUNK)),
                  pl.BlockSpec((1, n_gate, tq), lambda b, i: (b, 0, i)),
                  const_spec(overlap_t.shape)],
        out_specs=pl.BlockSpec((1, nsa_w, tq), lambda b, i: (b, 0, i)),
        scratch_shapes=[pltpu.VMEM((NSA_KV_HEADS, n_sel, tq), F32),
                        pltpu.VMEM((2, NSA_KV_HEADS, SLC_KEYS, NSA_GROUP * tq), F32)],
        compiler_params=pltpu.CompilerParams(dimension_semantics=("parallel", "arbitrary"),
                                             vmem_limit_bytes=VMEM_LIMIT),
        name="nsa_attn",
    )(qt, kc, vct, kslc, vst, kwin, vwt, gt, overlap_t)


def _out_kernel(x_ref, ylru_ref, onsat_ref, znsat_ref, gn_ref, gate_ref, wl_ref, wn_ref, o_ref):
    o = onsat_ref[0].astype(F32)
    ms = jnp.mean(o * o, axis=0, keepdims=True)
    z = znsat_ref[0].astype(F32)
    y_t = (o * lax.rsqrt(ms + EPS) * gn_ref[...]) * (z * jax.nn.sigmoid(z))
    y_nsa = jnp.transpose(y_t).astype(BF16)
    out = (jnp.dot(ylru_ref[0], wl_ref[...], preferred_element_type=F32)
           + jnp.dot(y_nsa, wn_ref[...], preferred_element_type=F32))
    o_ref[0] = x_ref[0] + gate_ref[0, 2:3, :] * out


def _out(x, ylru, onsat, znsat, g_nsa_b, mod3, w_lru, w_nsa):
    bsz, seq, d = x.shape
    ts = OUT_ROWS
    nsa_w = onsat.shape[1]

    def tok_spec(n):
        return pl.BlockSpec((1, ts, n), lambda b, s: (b, s, 0))

    def feat_spec(n):
        return pl.BlockSpec((1, n, ts), lambda b, s: (b, 0, s))

    def const_spec(shape):
        return pl.BlockSpec(shape, lambda b, s: (0,) * len(shape))

    return pl.pallas_call(
        _out_kernel,
        out_shape=jax.ShapeDtypeStruct((bsz, seq, d), F32),
        grid=(bsz, seq // ts),
        in_specs=[tok_spec(d), tok_spec(ylru.shape[2]), feat_spec(nsa_w), feat_spec(nsa_w),
                  const_spec((nsa_w, ts)),
                  pl.BlockSpec((1, 3, d), lambda b, s: (b, 0, 0)),
                  const_spec(w_lru.shape), const_spec(w_nsa.shape)],
        out_specs=tok_spec(d),
        compiler_params=pltpu.CompilerParams(dimension_semantics=("parallel", "parallel"),
                                             vmem_limit_bytes=VMEM_LIMIT),
        name="out_proj",
    )(x, ylru, onsat, znsat, g_nsa_b, mod3, w_lru, w_nsa)


def _overlap_t(n_cmp_pad, n_sel):
    ratio = SEL_BLOCK // CMP_STRIDE
    ov = np.zeros((n_sel, n_cmp_pad), np.float32)
    for c in range(n_cmp_pad - 1):
        for n in (c // ratio, (c + 1) // ratio):
            if n < n_sel:
                ov[n, c] += 1.0
    return ov


def _block_diag_pairs(w):
    nb, bs, _ = w.shape
    z = jnp.zeros((bs, bs), w.dtype)
    return jnp.stack([jnp.block([[w[2 * s], z], [z, w[2 * s + 1]]]) for s in range(nb // 2)])


def _compress_weights(w1, w2, transposed):
    hid = w1.shape[1]
    w1r = w1.reshape(2, CMP_STRIDE, HEAD_DIM, hid)
    parts = []
    for hk in range(NSA_KV_HEADS):
        for half in range(2):
            slot = jnp.zeros((CMP_STRIDE, NSA_KV_HEADS, HEAD_DIM, hid), w1.dtype).at[:, hk].set(w1r[half])
            parts.append(slot.reshape(CMP_STRIDE * LANES, hid))
    w1x = jnp.concatenate(parts, axis=1).astype(BF16)
    zpad = jnp.zeros_like(w2)
    w2lo = jnp.concatenate([w2, zpad], axis=1).astype(BF16)
    w2hi = jnp.concatenate([zpad, w2], axis=1).astype(BF16)
    if transposed:
        w2lo, w2hi = w2lo.T, w2hi.T
    return w1x, w2lo, w2hi


def _pad_to(n, m):
    return -(-n // m) * m


def _layer(x, c, w_ada, b_ada, norm_g, w_in, conv_w, conv_b, w_rg_a, b_rg_a, w_rg_x, b_rg_x, lru_lambda,
           cmp_pos_k, cmp_w1_k, cmp_w2_k, cmp_pos_v, cmp_w1_v, cmp_w2_v, g_q, g_k_cmp, g_k_slc, g_k_win,
           g_out_lru, g_out_nsa, w_out):
    bsz, seq, d = x.shape
    lru_w = d // 2
    nsa_w = NSA_HEADS * HEAD_DIM
    kv_w = NSA_KV_HEADS * HEAD_DIM
    n_sel = seq // SEL_BLOCK

    splits = (lru_w, lru_w, nsa_w, kv_w, kv_w, kv_w, kv_w, kv_w, kv_w, N_BRANCH * NSA_HEADS, nsa_w)
    offs = np.concatenate([[0], np.cumsum(splits)])
    names = ("xlru", "zlru", "q", "kcmp", "vcmp", "kslc", "vslc", "kwin", "vwin", "gates", "znsa")
    src = {n: (int(offs[i]), int(offs[i + 1])) for i, n in enumerate(names)}

    def plan(group, align):
        pieces, where, at = [], {}, 0
        for n in group:
            piece = w_in[:, src[n][0]:src[n][1]]
            width = _pad_to(piece.shape[1], align)
            if width != piece.shape[1]:
                piece = jnp.pad(piece, ((0, 0), (0, width - piece.shape[1])))
            pieces.append(piece)
            where[n] = (at, width)
            at += width
        return jnp.concatenate(pieces, axis=1).astype(BF16), where

    w_big, cols = plan(("xlru", "zlru", "kcmp", "vcmp", "kslc", "kwin"), LANES)
    w_tt, rows_t = plan(("q", "vslc", "vwin", "gates", "znsa"), 32)
    w_t = w_tt.T

    def dup(g):
        return jnp.concatenate([g, g]).reshape(1, LANES)

    gq_b = jnp.broadcast_to((g_q * (ATTN_SCALE * LOG2E)).reshape(HEAD_DIM, 1), (HEAD_DIM, PROJ_ROWS))
    mod3 = _mod(c, w_ada, b_ada).reshape(bsz, 3, d)
    lru_params = (conv_w, conv_b.reshape(1, lru_w),
                  _block_diag_pairs(w_rg_a).astype(BF16), b_rg_a.reshape(1, lru_w),
                  _block_diag_pairs(w_rg_x).astype(BF16), b_rg_x.reshape(1, lru_w),
                  lru_lambda.reshape(1, lru_w), g_out_lru.reshape(1, lru_w))
    (ylru, kcmp, vcmp, kslc, kwin, qt, vst, vwt, gt, znsat) = _proj(
        x, mod3, norm_g.reshape(1, d), w_big, cols, w_t, rows_t, gq_b, dup(g_k_slc), dup(g_k_win), lru_params)

    n_chunk = seq // CMP_STRIDE
    w1x_k, w2lo_k, w2hi_k = _compress_weights(cmp_w1_k, cmp_w2_k, False)
    w1x_v, w2lo_v, w2hi_v = _compress_weights(cmp_w1_v, cmp_w2_v, True)

    def pos_rows(pos):
        return jnp.broadcast_to(pos.reshape(1, CMP_BLOCK * HEAD_DIM), (8, CMP_BLOCK * HEAD_DIM))

    kc = _compress(kcmp, w1x_k, pos_rows(cmp_pos_k), cmp_w1_k,
                   w2lo_k, w2hi_k, dup(g_k_cmp), True)
    vct = _compress(vcmp, w1x_v, pos_rows(cmp_pos_v), cmp_w1_v,
                    w2lo_v, w2hi_v, dup(g_k_cmp), False)

    onsat = _attn(qt, kc, vct, kslc, vst, kwin, vwt, gt, jnp.asarray(_overlap_t(n_chunk, n_sel), dtype=BF16))

    g_nsa_b = jnp.broadcast_to(g_out_nsa.reshape(nsa_w, 1), (nsa_w, OUT_ROWS))
    return _out(x, ylru, onsat, znsat, g_nsa_b, mod3, w_out[:lru_w].astype(BF16), w_out[lru_w:].astype(BF16))


def kernel(x, c, w_ada, b_ada, norm_g, w_in, conv_w, conv_b, w_rg_a, b_rg_a, w_rg_x, b_rg_x, lru_lambda, cmp_pos_k, cmp_w1_k, cmp_w2_k, cmp_pos_v, cmp_w1_v, cmp_w2_v, g_q, g_k_cmp, g_k_slc, g_k_win, g_out_lru, g_out_nsa, w_out):
    params = (w_ada, b_ada, norm_g, w_in, conv_w, conv_b, w_rg_a, b_rg_a, w_rg_x, b_rg_x, lru_lambda,
              cmp_pos_k, cmp_w1_k, cmp_w2_k, cmp_pos_v, cmp_w1_v, cmp_w2_v, g_q, g_k_cmp, g_k_slc, g_k_win,
              g_out_lru, g_out_nsa, w_out)
    for layer in range(w_in.shape[0]):
        x = _layer(x, c, *(p[layer] for p in params))
    return x
```

```python
import functools
import math

import ml_dtypes
import numpy as np
import jax
import jax.numpy as jnp
from jax import lax
from jax.experimental import pallas as pl
from jax.experimental.pallas import tpu as pltpu

F32 = jnp.float32
BF16 = jnp.bfloat16
HIGHEST = lax.Precision.HIGHEST

LANES = 128
HALF = LANES // 2
SUBLANES = 8

LRU_BLOCKS = 8
CONV_WIDTH = 4
LRU_C = 8.0
NSA_HEADS = 8
HEAD_DIM = 64
NSA_KV_HEADS = 2
NSA_GROUP = NSA_HEADS // NSA_KV_HEADS
N_BRANCH = 3
CMP_STRIDE = 16
CMP_BLOCK = 2 * CMP_STRIDE
SEL_BLOCK = 64
SEL_SHIFT = SEL_BLOCK.bit_length() - 1
SEL_TOPK = 16
WINDOW = 512
ATTN_SCALE = HEAD_DIM ** -0.5
LOG2E = math.log2(math.e)
NEG_INF = -1e30
FORCE = 1e6
EPS = 1e-6

PROJ_ROWS = 1024
OUT_ROWS = 1024
ATTN_ROWS = 128
SLC_KEYS = 512
KEY_CHUNK = 128
VMEM_LIMIT = 56 * 1024 * 1024

FEAT_ONEHOT = 0
FEAT_BLK = SEL_BLOCK
FEAT_OFF = SEL_BLOCK + 3
FEAT_ONE = SEL_BLOCK + 6
FEAT_CHUNK = LANES - 32
N_SPLIT = 3

assert HEAD_DIM == HALF and NSA_KV_HEADS == 2 and KEY_CHUNK == LANES
assert ATTN_ROWS % KEY_CHUNK == 0 and ATTN_ROWS <= WINDOW


def _bf16_terms(value):
    terms, rest = [], np.float32(value)
    for _ in range(N_SPLIT):
        t = np.float32(rest).astype(ml_dtypes.bfloat16).astype(np.float32)
        terms.append(float(t))
        rest = np.float32(rest - t)
    return terms


def _alibi_coef(head):
    return float(2.0 ** (-8.0 * (head + 1) / NSA_HEADS)) * LOG2E


def _nt_dot(a, b):
    return lax.dot_general(a, b, (((1,), (1,)), ((), ())), preferred_element_type=F32)


def _half_rms(x, gain2):
    lane = lax.broadcasted_iota(jnp.int32, (1, LANES), 1)
    lo = lane < HALF
    sq = x * x
    ss_lo = jnp.sum(jnp.where(lo, sq, 0.0), axis=-1, keepdims=True)
    ss_hi = jnp.sum(jnp.where(lo, 0.0, sq), axis=-1, keepdims=True)
    r = jnp.where(lo, lax.rsqrt(ss_lo * (1.0 / HALF) + EPS), lax.rsqrt(ss_hi * (1.0 / HALF) + EPS))
    return x * r * gain2


def _key_features(pos, onehot):
    lane = lax.broadcasted_iota(jnp.int32, (1, LANES), 1)
    blk = jnp.right_shift(pos, SEL_SHIFT)
    off = jnp.bitwise_and(pos, SEL_BLOCK - 1)
    f = jnp.where(jnp.logical_and(lane >= FEAT_BLK, lane < FEAT_OFF), blk.astype(F32),
                  jnp.where(jnp.logical_and(lane >= FEAT_OFF, lane < FEAT_ONE), off.astype(F32),
                            jnp.where(lane == FEAT_ONE, 1.0, 0.0)))
    if onehot:
        f = jnp.where(lane < SEL_BLOCK, jnp.where(lane == blk, 1.0, 0.0), f)
        chunk = jnp.right_shift(pos, KEY_CHUNK.bit_length() - 1)
        f = jnp.where(lane >= FEAT_CHUNK, jnp.where(lane - FEAT_CHUNK == chunk, 1.0, 0.0), f)
    return f


def _mod_kernel(c_ref, w_ref, b_ref, o_ref):
    c = c_ref[...]
    o_ref[...] = jnp.dot(c * jax.nn.sigmoid(c), w_ref[...], preferred_element_type=F32, precision=HIGHEST) + b_ref[...]


def _mod(c, w_ada, b_ada):
    bsz, d = c.shape
    n = w_ada.shape[1]
    return pl.pallas_call(
        _mod_kernel,
        out_shape=jax.ShapeDtypeStruct((bsz, n), F32),
        grid=(n // d,),
        in_specs=[pl.BlockSpec((bsz, d), lambda j: (0, 0)),
                  pl.BlockSpec((d, d), lambda j: (0, j)),
                  pl.BlockSpec((1, d), lambda j: (0, j))],
        out_specs=pl.BlockSpec((bsz, d), lambda j: (0, j)),
        name="adaln_mod",
    )(c, w_ada, b_ada.reshape(1, n))


def _log1p(y):
    w = 1.0 + y
    return jnp.where(w == 1.0, y, jnp.log(w) * (y / (w - 1.0)))


def _softplus(x):
    return jnp.maximum(x, 0.0) + _log1p(jnp.exp(-jnp.abs(x)))


def _sigmoid(x):
    return 0.5 * jnp.tanh(0.5 * x) + 0.5


def _lru_gates(x, first, cw_ref, cb_ref, wa_ref, ba_ref, wx_ref, bx_ref, lam_ref, tail_ref, h_ref):
    rows, width = x.shape

    @pl.when(first)
    def _():
        tail_ref[...] = jnp.zeros_like(tail_ref)
        h_ref[...] = jnp.zeros_like(h_ref)

    tail = tail_ref[...]
    tail_ref[...] = x[rows - SUBLANES:rows]
    row8 = lax.broadcasted_iota(jnp.int32, (SUBLANES, 1), 0)
    xc = cb_ref[...] + cw_ref[CONV_WIDTH - 1:CONV_WIDTH, :] * x
    for k in range(1, CONV_WIDTH):
        xs = pltpu.roll(x, k, 0)
        head = jnp.where(row8 < k, pltpu.roll(tail, k, 0), xs[0:SUBLANES])
        xs = jnp.concatenate([head, xs[SUBLANES:]], axis=0)
        xc = xc + cw_ref[CONV_WIDTH - 1 - k:CONV_WIDTH - k, :] * xs

    xcb = xc.astype(BF16)
    nslot = width // LANES
    ra = jnp.concatenate([jnp.dot(xcb[:, s * LANES:(s + 1) * LANES], wa_ref[s], preferred_element_type=F32)
                          for s in range(nslot)], axis=1)
    ri = jnp.concatenate([jnp.dot(xcb[:, s * LANES:(s + 1) * LANES], wx_ref[s], preferred_element_type=F32)
                          for s in range(nslot)], axis=1)
    r = _sigmoid(ra + ba_ref[...])
    i = _sigmoid(ri + bx_ref[...])
    log_a = (-LRU_C) * r * _softplus(-lam_ref[...])
    a = jnp.exp(log_a)
    y = -jnp.tanh(log_a) * (a * a + 1.0)
    return a, jnp.where(y > 0.0, y * lax.rsqrt(y), 0.0) * (i * xc)


def _lru_scan_groups(a, u):
    rows = a.shape[0]
    sub = jnp.bitwise_and(lax.broadcasted_iota(jnp.int32, (rows, 1), 0), SUBLANES - 1)
    acc_a, acc_h = a, u
    d = 1
    while d < SUBLANES:
        keep = sub >= d
        sh_a = pltpu.roll(acc_a, d, 0)
        sh_h = pltpu.roll(acc_h, d, 0)
        acc_h = jnp.where(keep, acc_a * sh_h + acc_h, acc_h)
        acc_a = jnp.where(keep, acc_a * sh_a, acc_a)
        d *= 2
    return acc_a, acc_h


def _lru_carry(acc_a, acc_h, h_ref):
    rows = acc_a.shape[0]
    carry = h_ref[0:1, :]
    groups = []
    for g in range(rows // SUBLANES):
        hg = acc_h[g * SUBLANES:(g + 1) * SUBLANES] + acc_a[g * SUBLANES:(g + 1) * SUBLANES] * carry
        carry = hg[SUBLANES - 1:SUBLANES]
        groups.append(hg)
    h_ref[0:1, :] = carry
    return jnp.concatenate(groups, axis=0)


def _lru_out(h, z, g_ref):
    ms = jnp.mean(h * h, axis=-1, keepdims=True)
    return (h * lax.rsqrt(ms + EPS) * g_ref[...]) * (z * _sigmoid(z))


def _proj_kernel(cols, rows_t, x_ref, mod_ref, ng_ref, w_ref, wt_ref, gq_ref, gks_ref, gkw_ref,
                 cw_ref, cb_ref, wa_ref, ba_ref, wx_ref, bx_ref, lam_ref, gl_ref,
                 ylru_ref, kcmp_ref, vcmp_ref, kslc_ref, kwin_ref,
                 qt_ref, vst_ref, vwt_ref, gt_ref, znsat_ref, tail_ref, h_ref):
    rows = x_ref.shape[1]
    x = x_ref[0]
    ms = jnp.mean(x * x, axis=-1, keepdims=True)
    gain = ng_ref[...] * (1.0 + mod_ref[0, 1:2, :])
    h = (x * lax.rsqrt(ms + EPS)) * gain + mod_ref[0, 0:1, :]
    hb = h.astype(BF16)

    def mm(name):
        c0, n = cols[name]
        return jnp.dot(hb, w_ref[:, c0:c0 + n], preferred_element_type=F32)

    def mmt(name):
        r0, n = rows_t[name]
        return _nt_dot(wt_ref[r0:r0 + n, :], hb)

    def after(value, done):
        return value + jnp.minimum(jnp.abs(done[done.shape[0] - 1:, 0:1]), 0.0)

    lru_a, lru_u = _lru_gates(mm("xlru"), pl.program_id(1) == 0, cw_ref, cb_ref, wa_ref, ba_ref,
                              wx_ref, bx_ref, lam_ref, tail_ref, h_ref)
    z_lru = mm("zlru")
    kcmp_ref[0] = mm("kcmp")
    vcmp_ref[0] = mm("vcmp")

    pos = pl.program_id(1) * rows + lax.broadcasted_iota(jnp.int32, (rows, 1), 0)
    kslc_ref[0, :, 0:LANES] = _half_rms(mm("kslc"), gks_ref[...]).astype(BF16)
    kslc_ref[0, :, LANES:2 * LANES] = _key_features(pos, True).astype(BF16)
    k_win = mm("kwin")
    kwin_ref[0, :, 0:LANES] = _half_rms(k_win, gkw_ref[...]).astype(BF16)
    kwin_ref[0, :, LANES:2 * LANES] = _key_features(pos, True).astype(BF16)

    lru_a, lru_h = _lru_scan_groups(lru_a, after(lru_u, k_win))
    qt = mmt("q")
    for hd in range(NSA_HEADS):
        qh = qt[hd * HEAD_DIM:(hd + 1) * HEAD_DIM]
        r = lax.rsqrt(jnp.mean(qh * qh, axis=0, keepdims=True) + EPS)
        qt_ref[0, hd * HEAD_DIM:(hd + 1) * HEAD_DIM, :] = (qh * r * gq_ref[...]).astype(BF16)

    h_lru = _lru_carry(lru_a, after(lru_h, qt), h_ref)
    vst = mmt("vslc").astype(BF16)
    vwt = mmt("vwin").astype(BF16)
    for j in range(rows // KEY_CHUNK):
        vst_ref[0, j] = vst[:, j * KEY_CHUNK:(j + 1) * KEY_CHUNK]
        vwt_ref[0, j] = vwt[:, j * KEY_CHUNK:(j + 1) * KEY_CHUNK]
    gt_ref[0] = jax.nn.sigmoid(mmt("gates"))
    z_nsa = mmt("znsa")
    znsat_ref[0] = z_nsa.astype(BF16)
    ylru_ref[0] = _lru_out(after(h_lru, z_nsa), z_lru, gl_ref).astype(BF16)


def _proj(x, mod3, norm_g, w_big, cols, w_t, rows_t, gq_b, gks2, gkw2, lru_params):
    bsz, seq, d = x.shape
    ts = PROJ_ROWS
    nsa_w = NSA_HEADS * HEAD_DIM
    lru_w = cols["xlru"][1]
    n_gate = rows_t["gates"][1]

    def tok_spec(n):
        return pl.BlockSpec((1, ts, n), lambda b, s: (b, s, 0))

    def feat_spec(n):
        return pl.BlockSpec((1, n, ts), lambda b, s: (b, 0, s))

    def chunk_spec():
        return pl.BlockSpec((1, ts // KEY_CHUNK, LANES, KEY_CHUNK), lambda b, s: (b, s, 0, 0))

    def const_spec(shape):
        return pl.BlockSpec(shape, lambda b, s: (0,) * len(shape))

    out_shape = (
        jax.ShapeDtypeStruct((bsz, seq, lru_w), BF16),
        jax.ShapeDtypeStruct((bsz, seq, LANES), F32),
        jax.ShapeDtypeStruct((bsz, seq, LANES), F32),
        jax.ShapeDtypeStruct((bsz, seq, 2 * LANES), BF16),
        jax.ShapeDtypeStruct((bsz, seq, 2 * LANES), BF16),
        jax.ShapeDtypeStruct((bsz, nsa_w, seq), BF16),
        jax.ShapeDtypeStruct((bsz, seq // KEY_CHUNK, LANES, KEY_CHUNK), BF16),
        jax.ShapeDtypeStruct((bsz, seq // KEY_CHUNK, LANES, KEY_CHUNK), BF16),
        jax.ShapeDtypeStruct((bsz, n_gate, seq), F32),
        jax.ShapeDtypeStruct((bsz, nsa_w, seq), BF16),
    )
    out_specs = (tok_spec(lru_w), tok_spec(LANES), tok_spec(LANES),
                 tok_spec(2 * LANES), tok_spec(2 * LANES),
                 feat_spec(nsa_w), chunk_spec(), chunk_spec(), feat_spec(n_gate), feat_spec(nsa_w))
    return pl.pallas_call(
        functools.partial(_proj_kernel, cols, rows_t),
        out_shape=out_shape,
        grid=(bsz, seq // ts),
        in_specs=[tok_spec(d),
                  pl.BlockSpec((1, 3, d), lambda b, s: (b, 0, 0)),
                  const_spec((1, d)),
                  const_spec(w_big.shape), const_spec(w_t.shape),
                  const_spec((HEAD_DIM, ts)), const_spec((1, LANES)), const_spec((1, LANES))]
                 + [const_spec(p.shape) for p in lru_params],
        out_specs=out_specs,
        scratch_shapes=[pltpu.VMEM((SUBLANES, lru_w), F32), pltpu.VMEM((SUBLANES, lru_w), F32)],
        compiler_params=pltpu.CompilerParams(dimension_semantics=("parallel", "arbitrary"),
                                             vmem_limit_bytes=VMEM_LIMIT),
        name="in_proj_lru",
    )(x, mod3, norm_g, w_big, w_t, gq_b, gks2, gkw2, *lru_params)


def _compress_kernel(is_key, c_ref, w1x_ref, pos_ref, w1_ref, w2lo_ref, w2hi_ref, g_ref, o_ref, const_ref):
    nblk = c_ref.shape[1] // CMP_STRIDE
    hid = w1_ref.shape[1]

    @pl.when(pl.program_id(0) == 0)
    def _():
        const_ref[...] = jnp.dot(pos_ref[...], w1_ref[...], preferred_element_type=F32, precision=HIGHEST)

    ab = jnp.zeros((nblk, 4 * hid), F32)
    for j in range(0, CMP_STRIDE, 2):
        pair = jnp.concatenate([c_ref[0, pl.ds(j + i, nblk, stride=CMP_STRIDE), :].astype(BF16) for i in range(2)],
                               axis=1)
        ab = ab + jnp.dot(pair, w1x_ref[j * LANES:(j + 2) * LANES, :], preferred_element_type=F32)
    const = const_ref[0:1, :]
    acts = []
    for hk in range(NSA_KV_HEADS):
        first = ab[:, (2 * hk) * hid:(2 * hk + 1) * hid]
        second = ab[:, (2 * hk + 1) * hid:(2 * hk + 2) * hid]
        hidden = first + pltpu.roll(second, nblk - 1, 0) + const
        acts.append(jax.nn.gelu(hidden).astype(BF16))
    if is_key:
        out = (jnp.dot(acts[0], w2lo_ref[...], preferred_element_type=F32)
               + jnp.dot(acts[1], w2hi_ref[...], preferred_element_type=F32))
        o_ref[0, :, 0:LANES] = _half_rms(out, g_ref[...]).astype(BF16)
        cend = CMP_STRIDE * lax.broadcasted_iota(jnp.int32, (nblk, 1), 0) + (CMP_BLOCK - 1)
        o_ref[0, :, LANES:2 * LANES] = _key_features(cend, False).astype(BF16)
    else:
        out_t = _nt_dot(w2lo_ref[...], acts[0]) + _nt_dot(w2hi_ref[...], acts[1])
        o_ref[0] = out_t.astype(BF16)


def _compress(tokens, w1x, pos_flat, w1, w2lo, w2hi, gain2, is_key):
    bsz, seq, width = tokens.shape
    nblk = seq // CMP_STRIDE
    hid = w1.shape[1]

    def const_spec(shape):
        return pl.BlockSpec(shape, lambda b: (0,) * len(shape))

    out_tail = (nblk, 2 * LANES) if is_key else (LANES, nblk)
    return pl.pallas_call(
        functools.partial(_compress_kernel, is_key),
        out_shape=jax.ShapeDtypeStruct((bsz,) + out_tail, BF16),
        grid=(bsz,),
        in_specs=[pl.BlockSpec((1, seq, width), lambda b: (b, 0, 0)),
                  const_spec(w1x.shape), const_spec((8, w1.shape[0])), const_spec(w1.shape),
                  const_spec(w2lo.shape), const_spec(w2hi.shape), const_spec((1, LANES))],
        out_specs=pl.BlockSpec((1,) + out_tail, lambda b: (b, 0, 0)),
        scratch_shapes=[pltpu.VMEM((SUBLANES, hid), F32)],
        compiler_params=pltpu.CompilerParams(dimension_semantics=("arbitrary",), vmem_limit_bytes=VMEM_LIMIT),
        name="compress_k" if is_key else "compress_v",
    )(tokens, w1x, pos_flat, w1, w2lo, w2hi, gain2)


ONES_ROWS = 16


def _with_ones(vt):
    return jnp.concatenate([vt, jnp.ones((ONES_ROWS, vt.shape[1]), BF16)], axis=0)


def _online_update(carry, s, top, vt):
    m, acc = carry
    m_new = jnp.maximum(m, top)
    p = jnp.exp2(s - m_new)
    acc = jnp.exp2(m - m_new) * acc + jnp.dot(_with_ones(vt), p.astype(BF16), preferred_element_type=F32)
    return m_new, acc


def _normalised(acc):
    return acc[0:HEAD_DIM] * (1.0 / acc[HEAD_DIM:HEAD_DIM + 1])


def _attn_kernel(qt_ref, kc_ref, vct_ref, ks_ref, vst_ref, kw_ref, vwt_ref, gt_ref, ovt_ref,
                 o_ref, impt_ref, s_ref, sel_ref):
    tq = qt_ref.shape[2]
    n_cmp = kc_ref.shape[1]
    n_sel = impt_ref.shape[1]
    ncol = NSA_GROUP * tq
    per_wide = SLC_KEYS // KEY_CHUNK
    per_tile = tq // KEY_CHUNK
    qi = pl.program_id(1)
    t0 = qi * tq
    c0 = qi * per_tile
    tloc = lax.broadcasted_iota(jnp.int32, (1, tq), 1)
    tcol = jnp.concatenate([tloc] * NSA_GROUP, axis=1)
    tpos = t0 + tloc
    nrow = lax.broadcasted_iota(jnp.int32, (n_sel, 1), 0)
    cur = jnp.right_shift(tpos, SEL_SHIFT)
    forced = jnp.logical_or(jnp.logical_or(nrow == 0, nrow == cur), nrow == cur - 1)
    frow = lax.broadcasted_iota(jnp.int32, (HEAD_DIM, 1), 0)
    srow = lax.broadcasted_iota(jnp.int32, (SUBLANES, 1), 0)
    zeros_q = jnp.zeros((HEAD_DIM, tq), BF16)
    row_minus_col = lax.broadcasted_iota(jnp.int32, (tq, 1), 0) - tcol
    own_ok = row_minus_col <= 0

    cidx = lax.broadcasted_iota(jnp.int32, (n_cmp, 1), 0)
    cend = CMP_STRIDE * cidx + (CMP_BLOCK - 1)
    cmp_ok = jnp.logical_and(cend <= t0 + tcol, cidx < n_cmp - 1)
    col_has_cmp = (t0 + tcol) >= CMP_BLOCK - 1
    gates = gt_ref[0]
    kv_heads = range(NSA_KV_HEADS)

    def values(ref, hk, first_chunk, n):
        return jnp.concatenate([ref[0, first_chunk + i, hk * HEAD_DIM:(hk + 1) * HEAD_DIM, :]
                                for i in range(n)], axis=1)

    not_past = jnp.logical_and(frow >= FEAT_CHUNK - SEL_BLOCK, frow - (FEAT_CHUNK - SEL_BLOCK) >= c0)

    q_plain, q_plain_past, o_cmp = [], [], []
    for hk in kv_heads:
        q_rows, feat_rows, feat_rows_past = [], [], []
        for g in range(NSA_GROUP):
            hd = hk * NSA_GROUP + g
            qh = qt_ref[0, hd * HEAD_DIM:(hd + 1) * HEAD_DIM, :]
            q_rows.append([qh, zeros_q] if hk == 0 else [zeros_q, qh])
            c = _alibi_coef(hd)
            a = jnp.zeros((HEAD_DIM, 1), F32)
            for i, term in enumerate(_bf16_terms(c)):
                a = jnp.where(frow == FEAT_BLK - SEL_BLOCK + i, SEL_BLOCK * term, a)
                a = jnp.where(frow == FEAT_OFF - SEL_BLOCK + i, term, a)
            a = jnp.where(frow == FEAT_ONE - SEL_BLOCK, -c * t0.astype(F32), a)
            feat_rows.append(jnp.broadcast_to(a, (HEAD_DIM, tq)).astype(BF16))
            feat_rows_past.append(jnp.broadcast_to(jnp.where(not_past, NEG_INF, a), (HEAD_DIM, tq)).astype(BF16))

        def q_operand(pen, feats, q_rows=q_rows):
            return jnp.concatenate([jnp.concatenate(q_rows[g] + [pen, feats[g]], axis=0)
                                    for g in range(NSA_GROUP)], axis=1)

        q_plain.append(q_operand(zeros_q, feat_rows))
        q_plain_past.append(q_operand(zeros_q, feat_rows_past))
        sel_ref[0, hk] = q_plain[hk]
        sel_ref[1, hk] = q_plain_past[hk]

        s = jnp.dot(kc_ref[0], q_plain[hk], preferred_element_type=F32)
        s = jnp.where(cmp_ok, s, NEG_INF)
        e = jnp.exp2(s - jnp.max(s, axis=0, keepdims=True))
        p = e * jnp.where(col_has_cmp, 1.0 / jnp.sum(e, axis=0, keepdims=True), 0.0)
        o_cmp.append(jnp.dot(vct_ref[0, hk * HEAD_DIM:(hk + 1) * HEAD_DIM, :], p.astype(BF16),
                             preferred_element_type=F32))
        p_sum = p[:, 0:tq]
        for g in range(1, NSA_GROUP):
            p_sum = p_sum + p[:, g * tq:(g + 1) * tq]
        imp = jnp.zeros((n_sel, tq), F32)
        rest = p_sum
        for _ in range(N_SPLIT):
            term = rest.astype(BF16)
            imp = imp + jnp.dot(ovt_ref[...], term, preferred_element_type=F32)
            rest = rest - term.astype(F32)
        imp = jnp.where(forced, FORCE, jnp.where(nrow > cur, -FORCE, imp))

        impt_ref[hk] = imp
        groups = [imp[b * SUBLANES:(b + 1) * SUBLANES] for b in range(n_sel // SUBLANES)]
        ranks = [jnp.zeros((SUBLANES, tq), F32) for _ in groups]
        for m in range(n_sel):
            other = impt_ref[hk, m:m + 1, :]
            for b, grp in enumerate(groups):
                if b * SUBLANES > m:
                    ahead = other >= grp
                elif (b + 1) * SUBLANES <= m:
                    ahead = other > grp
                else:
                    ahead = jnp.logical_or(other > grp, jnp.logical_and(other == grp, srow > m - b * SUBLANES))
                ranks[b] = ranks[b] + jnp.where(ahead, 1.0, 0.0)
        rank = jnp.concatenate(ranks, axis=0)
        pen = jnp.where(rank < float(SEL_TOPK), 0.0, NEG_INF).astype(BF16)
        pen_rows = jnp.concatenate([pen] * NSA_GROUP, axis=1)
        sel_ref[0, hk, 2 * HEAD_DIM:3 * HEAD_DIM, :] = pen_rows
        sel_ref[1, hk, 2 * HEAD_DIM:3 * HEAD_DIM, :] = pen_rows

    init = (jnp.full((1, ncol), NEG_INF, F32), jnp.zeros((HEAD_DIM + ONES_ROWS, ncol), F32))

    n_back = WINDOW // KEY_CHUNK
    wc0 = jnp.maximum(c0 - n_back, 0)
    lead = t0 - wc0 * KEY_CHUNK
    k_back = kw_ref[0, pl.ds(pl.multiple_of(wc0 * KEY_CHUNK, KEY_CHUNK), WINDOW), :]
    k_tile = kw_ref[0, pl.ds(pl.multiple_of(t0, KEY_CHUNK), tq), :]
    recent = row_minus_col > lead - WINDOW
    o_win = []
    for hk in kv_heads:
        s_back = jnp.dot(k_back, q_plain_past[hk], preferred_element_type=F32)
        s_own = jnp.where(own_ok, jnp.dot(k_tile, q_plain[hk], preferred_element_type=F32), NEG_INF)
        s = jnp.concatenate([jnp.where(recent, s_back[0:tq], NEG_INF), s_back[tq:], s_own], axis=0)
        vt = jnp.concatenate([values(vwt_ref, hk, wc0, n_back), values(vwt_ref, hk, c0, per_tile)], axis=1)
        _, acc = _online_update(init, s, jnp.max(s, axis=0, keepdims=True), vt)
        o_win.append(_normalised(acc))

    k_own = ks_ref[0, pl.ds(pl.multiple_of(t0, KEY_CHUNK), tq), :]
    stats = []
    for hk in kv_heads:
        s = jnp.where(own_ok, jnp.dot(k_own, sel_ref[0, hk], preferred_element_type=F32), NEG_INF)
        stats.append(_online_update(init, s, jnp.max(s, axis=0, keepdims=True), values(vst_ref, hk, c0, per_tile)))
    stats = tuple(stats)

    def produce(j, slot):
        k = ks_ref[0, j * SLC_KEYS:(j + 1) * SLC_KEYS, :]
        tops = []
        for hk in kv_heads:
            s = jnp.dot(k, sel_ref[1, hk], preferred_element_type=F32)
            s_ref[slot, hk] = s
            tops.append(jnp.max(s, axis=0, keepdims=True))
        return tuple(tops)

    def consume(j, slot, tops, stats):
        return tuple(_online_update(stats[hk], s_ref[slot, hk], tops[hk],
                                    values(vst_ref, hk, j * per_wide, per_wide)) for hk in kv_heads)

    def chain(n_blocks):
        def run(stats):
            tops = produce(0, 0) if n_blocks else None
            for j in range(n_blocks):
                nxt = produce(j + 1, (j + 1) % 2) if j + 1 < n_blocks else None
                stats = consume(j, j % 2, tops, stats)
                tops = nxt
            return stats
        return run

    stats = lax.switch((t0 + SLC_KEYS - 1) // SLC_KEYS,
                       [chain(n) for n in range(ks_ref.shape[1] // SLC_KEYS + 1)], stats)
    o_slc = [_normalised(acc) for _, acc in stats]

    for hk in kv_heads:
        for g in range(NSA_GROUP):
            hd = hk * NSA_GROUP + g
            tot = jnp.zeros((HEAD_DIM, tq), F32)
            for br, o_br in enumerate((o_cmp[hk], o_slc[hk], o_win[hk])):
                r = br * NSA_HEADS + hd
                tot = tot + gates[r:r + 1, :] * o_br[:, g * tq:(g + 1) * tq]
            o_ref[0, hd * HEAD_DIM:(hd + 1) * HEAD_DIM, :] = tot.astype(BF16)


def _attn(qt, kc, vct, kslc, vst, kwin, vwt, gt, overlap_t):
    bsz, nsa_w, seq = qt.shape
    tq = ATTN_ROWS
    n_cmp = kc.shape[1]
    n_sel = seq // SEL_BLOCK
    n_gate = gt.shape[1]
    assert seq % (2 * SLC_KEYS) == 0 and n_sel <= HALF and seq >= WINDOW + tq
    assert seq // KEY_CHUNK <= LANES - FEAT_CHUNK and n_sel % SUBLANES == 0

    def per_batch(shape):
        return pl.BlockSpec((1,) + shape, lambda b, i: (b,) + (0,) * len(shape))

    def const_spec(shape):
        return pl.BlockSpec(shape, lambda b, i: (0,) * len(shape))

    return pl.pallas_call(
        _attn_kernel,
        out_shape=jax.ShapeDtypeStruct((bsz, nsa_w, seq), BF16),
        grid=(bsz, seq // tq),
        in_specs=[pl.BlockSpec((1, nsa_w, tq), lambda b, i: (b, 0, i)),
                  per_batch((n_cmp, 2 * LANES)), per_batch((LANES, n_cmp)),
                  per_batch((seq, 2 * LANES)), per_batch((seq // KEY_CHUNK, LANES, KEY_CHUNK)),
                  per_batch((seq, 2 * LANES)), per_batch((seq // KEY_CHUNK, LANES, KEY_CHUNK)),
                  pl.BlockSpec((1, n_gate, tq), lambda b, i: (b, 0, i)),
                  const_spec(overlap_t.shape)],
        out_specs=pl.BlockSpec((1, nsa_w, tq), lambda b, i: (b, 0, i)),
        scratch_shapes=[pltpu.VMEM((NSA_KV_HEADS, n_sel, tq), F32),
                        pltpu.VMEM((2, NSA_KV_HEADS, SLC_KEYS, NSA_GROUP * tq), F32),
                        pltpu.VMEM((2, NSA_KV_HEADS, 2 * LANES, NSA_GROUP * tq), BF16)],
        compiler_params=pltpu.CompilerParams(dimension_semantics=("parallel", "arbitrary"),
                                             vmem_limit_bytes=VMEM_LIMIT),
        name="nsa_attn",
    )(qt, kc, vct, kslc, vst, kwin, vwt, gt, overlap_t)


def _out_kernel(x_ref, ylru_ref, onsat_ref, znsat_ref, gn_ref, gate_ref, wl_ref, wn_ref, o_ref):
    o = onsat_ref[0].astype(F32)
    ms = jnp.mean(o * o, axis=0, keepdims=True)
    z = znsat_ref[0].astype(F32)
    y_t = (o * lax.rsqrt(ms + EPS) * gn_ref[...]) * (z * jax.nn.sigmoid(z))
    y_nsa = jnp.transpose(y_t).astype(BF16)
    out = (jnp.dot(ylru_ref[0], wl_ref[...], preferred_element_type=F32)
           + jnp.dot(y_nsa, wn_ref[...], preferred_element_type=F32))
    o_ref[0] = x_ref[0] + gate_ref[0, 2:3, :] * out


def _out(x, ylru, onsat, znsat, g_nsa_b, mod3, w_lru, w_nsa):
    bsz, seq, d = x.shape
    ts = OUT_ROWS
    nsa_w = onsat.shape[1]

    def tok_spec(n):
        return pl.BlockSpec((1, ts, n), lambda b, s: (b, s, 0))

    def feat_spec(n):
        return pl.BlockSpec((1, n, ts), lambda b, s: (b, 0, s))

    def const_spec(shape):
        return pl.BlockSpec(shape, lambda b, s: (0,) * len(shape))

    return pl.pallas_call(
        _out_kernel,
        out_shape=jax.ShapeDtypeStruct((bsz, seq, d), F32),
        grid=(bsz, seq // ts),
        in_specs=[tok_spec(d), tok_spec(ylru.shape[2]), feat_spec(nsa_w), feat_spec(nsa_w),
                  const_spec((nsa_w, ts)),
                  pl.BlockSpec((1, 3, d), lambda b, s: (b, 0, 0)),
                  const_spec(w_lru.shape), const_spec(w_nsa.shape)],
        out_specs=tok_spec(d),
        compiler_params=pltpu.CompilerParams(dimension_semantics=("parallel", "parallel"),
                                             vmem_limit_bytes=VMEM_LIMIT),
        name="out_proj",
    )(x, ylru, onsat, znsat, g_nsa_b, mod3, w_lru, w_nsa)


def _overlap_t(n_cmp_pad, n_sel):
    ratio = SEL_BLOCK // CMP_STRIDE
    ov = np.zeros((n_sel, n_cmp_pad), np.float32)
    for c in range(n_cmp_pad - 1):
        for n in (c // ratio, (c + 1) // ratio):
            if n < n_sel:
                ov[n, c] += 1.0
    return ov


def _block_diag_pairs(w):
    nb, bs, _ = w.shape
    z = jnp.zeros((bs, bs), w.dtype)
    return jnp.stack([jnp.block([[w[2 * s], z], [z, w[2 * s + 1]]]) for s in range(nb // 2)])


def _compress_weights(w1, w2, transposed):
    hid = w1.shape[1]
    w1r = w1.reshape(2, CMP_STRIDE, HEAD_DIM, hid)
    parts = []
    for hk in range(NSA_KV_HEADS):
        for half in range(2):
            slot = jnp.zeros((CMP_STRIDE, NSA_KV_HEADS, HEAD_DIM, hid), w1.dtype).at[:, hk].set(w1r[half])
            parts.append(slot.reshape(CMP_STRIDE * LANES, hid))
    w1x = jnp.concatenate(parts, axis=1).astype(BF16)
    zpad = jnp.zeros_like(w2)
    w2lo = jnp.concatenate([w2, zpad], axis=1).astype(BF16)
    w2hi = jnp.concatenate([zpad, w2], axis=1).astype(BF16)
    if transposed:
        w2lo, w2hi = w2lo.T, w2hi.T
    return w1x, w2lo, w2hi


def _pad_to(n, m):
    return -(-n // m) * m


def _layer(x, c, w_ada, b_ada, norm_g, w_in, conv_w, conv_b, w_rg_a, b_rg_a, w_rg_x, b_rg_x, lru_lambda,
           cmp_pos_k, cmp_w1_k, cmp_w2_k, cmp_pos_v, cmp_w1_v, cmp_w2_v, g_q, g_k_cmp, g_k_slc, g_k_win,
           g_out_lru, g_out_nsa, w_out):
    bsz, seq, d = x.shape
    lru_w = d // 2
    nsa_w = NSA_HEADS * HEAD_DIM
    kv_w = NSA_KV_HEADS * HEAD_DIM
    n_sel = seq // SEL_BLOCK

    splits = (lru_w, lru_w, nsa_w, kv_w, kv_w, kv_w, kv_w, kv_w, kv_w, N_BRANCH * NSA_HEADS, nsa_w)
    offs = np.concatenate([[0], np.cumsum(splits)])
    names = ("xlru", "zlru", "q", "kcmp", "vcmp", "kslc", "vslc", "kwin", "vwin", "gates", "znsa")
    src = {n: (int(offs[i]), int(offs[i + 1])) for i, n in enumerate(names)}

    def plan(group, align):
        pieces, where, at = [], {}, 0
        for n in group:
            piece = w_in[:, src[n][0]:src[n][1]]
            width = _pad_to(piece.shape[1], align)
            if width != piece.shape[1]:
                piece = jnp.pad(piece, ((0, 0), (0, width - piece.shape[1])))
            pieces.append(piece)
            where[n] = (at, width)
            at += width
        return jnp.concatenate(pieces, axis=1).astype(BF16), where

    w_big, cols = plan(("xlru", "zlru", "kcmp", "vcmp", "kslc", "kwin"), LANES)
    w_tt, rows_t = plan(("q", "vslc", "vwin", "gates", "znsa"), 32)
    w_t = w_tt.T

    def dup(g):
        return jnp.concatenate([g, g]).reshape(1, LANES)

    gq_b = jnp.broadcast_to((g_q * (ATTN_SCALE * LOG2E)).reshape(HEAD_DIM, 1), (HEAD_DIM, PROJ_ROWS))
    mod3 = _mod(c, w_ada, b_ada).reshape(bsz, 3, d)
    lru_params = (conv_w, conv_b.reshape(1, lru_w),
                  _block_diag_pairs(w_rg_a).astype(BF16), b_rg_a.reshape(1, lru_w),
                  _block_diag_pairs(w_rg_x).astype(BF16), b_rg_x.reshape(1, lru_w),
                  lru_lambda.reshape(1, lru_w), g_out_lru.reshape(1, lru_w))
    (ylru, kcmp, vcmp, kslc, kwin, qt, vst, vwt, gt, znsat) = _proj(
        x, mod3, norm_g.reshape(1, d), w_big, cols, w_t, rows_t, gq_b, dup(g_k_slc), dup(g_k_win), lru_params)

    n_chunk = seq // CMP_STRIDE
    w1x_k, w2lo_k, w2hi_k = _compress_weights(cmp_w1_k, cmp_w2_k, False)
    w1x_v, w2lo_v, w2hi_v = _compress_weights(cmp_w1_v, cmp_w2_v, True)

    def pos_rows(pos):
        return jnp.broadcast_to(pos.reshape(1, CMP_BLOCK * HEAD_DIM), (8, CMP_BLOCK * HEAD_DIM))

    kc = _compress(kcmp, w1x_k, pos_rows(cmp_pos_k), cmp_w1_k,
                   w2lo_k, w2hi_k, dup(g_k_cmp), True)
    vct = _compress(vcmp, w1x_v, pos_rows(cmp_pos_v), cmp_w1_v,
                    w2lo_v, w2hi_v, dup(g_k_cmp), False)

    onsat = _attn(qt, kc, vct, kslc, vst, kwin, vwt, gt, jnp.asarray(_overlap_t(n_chunk, n_sel), dtype=BF16))

    g_nsa_b = jnp.broadcast_to(g_out_nsa.reshape(nsa_w, 1), (nsa_w, OUT_ROWS))
    return _out(x, ylru, onsat, znsat, g_nsa_b, mod3, w_out[:lru_w].astype(BF16), w_out[lru_w:].astype(BF16))


def kernel(x, c, w_ada, b_ada, norm_g, w_in, conv_w, conv_b, w_rg_a, b_rg_a, w_rg_x, b_rg_x, lru_lambda, cmp_pos_k, cmp_w1_k, cmp_w2_k, cmp_pos_v, cmp_w1_v, cmp_w2_v, g_q, g_k_cmp, g_k_slc, g_k_win, g_out_lru, g_out_nsa, w_out):
    params = (w_ada, b_ada, norm_g, w_in, conv_w, conv_b, w_rg_a, b_rg_a, w_rg_x, b_rg_x, lru_lambda,
              cmp_pos_k, cmp_w1_k, cmp_w2_k, cmp_pos_v, cmp_w1_v, cmp_w2_v, g_q, g_k_cmp, g_k_slc, g_k_win,
              g_out_lru, g_out_nsa, w_out)
    for layer in range(w_in.shape[0]):
        x = _layer(x, c, *(p[layer] for p in params))
    return x
```

```python
import functools
import math

import ml_dtypes
import numpy as np
import jax
import jax.numpy as jnp
from jax import lax
from jax.experimental import pallas as pl
from jax.experimental.pallas import tpu as pltpu

F32 = jnp.float32
BF16 = jnp.bfloat16
HIGHEST = lax.Precision.HIGHEST

LANES = 128
HALF = LANES // 2
SUBLANES = 8
BF16_SUBLANES = 2 * SUBLANES

CONV_WIDTH = 4
LRU_C = 8.0
NSA_HEADS = 8
HEAD_DIM = 64
NSA_KV_HEADS = 2
NSA_GROUP = NSA_HEADS // NSA_KV_HEADS
N_BRANCH = 3
CMP_STRIDE = 16
CMP_BLOCK = 2 * CMP_STRIDE
SEL_BLOCK = 64
SEL_SHIFT = SEL_BLOCK.bit_length() - 1
SEL_TOPK = 16
WINDOW = 512
ATTN_SCALE = HEAD_DIM ** -0.5
LOG2E = math.log2(math.e)
NEG_INF = -1e30
FORCE = 1e6
EPS = 1e-6

PROJ_ROWS = 1024
OUT_ROWS = 1024
ATTN_ROWS = 128
SLC_KEYS = 512
KEY_CHUNK = 128
VMEM_LIMIT = 56 * 1024 * 1024

FEAT_BLK = SEL_BLOCK
FEAT_OFF = SEL_BLOCK + 3
FEAT_ONE = SEL_BLOCK + 6
MAX_KEY_CHUNKS = 32
FEAT_CHUNK = LANES - MAX_KEY_CHUNKS
N_SPLIT = 3

assert HEAD_DIM == HALF and NSA_KV_HEADS == 2 and KEY_CHUNK == LANES
assert ATTN_ROWS % KEY_CHUNK == 0 and ATTN_ROWS <= WINDOW


def _bf16_terms(value):
    terms, rest = [], np.float32(value)
    for _ in range(N_SPLIT):
        t = np.float32(rest).astype(ml_dtypes.bfloat16).astype(np.float32)
        terms.append(float(t))
        rest = np.float32(rest - t)
    return terms


def _alibi_coef(head):
    return float(2.0 ** (-8.0 * (head + 1) / NSA_HEADS)) * LOG2E


def _nt_dot(a, b):
    return lax.dot_general(a, b, (((1,), (1,)), ((), ())), preferred_element_type=F32)


def _half_rms(x, gain2):
    lane = lax.broadcasted_iota(jnp.int32, (1, LANES), 1)
    lo = lane < HALF
    sq = x * x
    ss_lo = jnp.sum(jnp.where(lo, sq, 0.0), axis=-1, keepdims=True)
    ss_hi = jnp.sum(jnp.where(lo, 0.0, sq), axis=-1, keepdims=True)
    r = jnp.where(lo, lax.rsqrt(ss_lo * (1.0 / HALF) + EPS), lax.rsqrt(ss_hi * (1.0 / HALF) + EPS))
    return x * r * gain2


def _key_features(pos, onehot):
    lane = lax.broadcasted_iota(jnp.int32, (1, LANES), 1)
    blk = jnp.right_shift(pos, SEL_SHIFT)
    off = jnp.bitwise_and(pos, SEL_BLOCK - 1)
    f = jnp.where(jnp.logical_and(lane >= FEAT_BLK, lane < FEAT_OFF), blk.astype(F32),
                  jnp.where(jnp.logical_and(lane >= FEAT_OFF, lane < FEAT_ONE), off.astype(F32),
                            jnp.where(lane == FEAT_ONE, 1.0, 0.0)))
    if onehot:
        f = jnp.where(lane < SEL_BLOCK, jnp.where(lane == blk, 1.0, 0.0), f)
        chunk = jnp.right_shift(pos, KEY_CHUNK.bit_length() - 1)
        f = jnp.where(lane >= FEAT_CHUNK, jnp.where(lane - FEAT_CHUNK == chunk, 1.0, 0.0), f)
    return f


def _mod_kernel(c_ref, w_ref, b_ref, o_ref):
    c = c_ref[...]
    o_ref[...] = jnp.dot(c * jax.nn.sigmoid(c), w_ref[...], preferred_element_type=F32, precision=HIGHEST) + b_ref[...]


def _mod(c, w_ada, b_ada):
    bsz, d = c.shape
    n = w_ada.shape[1]
    return pl.pallas_call(
        _mod_kernel,
        out_shape=jax.ShapeDtypeStruct((bsz, n), F32),
        grid=(n // d,),
        in_specs=[pl.BlockSpec((bsz, d), lambda j: (0, 0)),
                  pl.BlockSpec((d, d), lambda j: (0, j)),
                  pl.BlockSpec((1, d), lambda j: (0, j))],
        out_specs=pl.BlockSpec((bsz, d), lambda j: (0, j)),
        name="adaln_mod",
    )(c, w_ada, b_ada.reshape(1, n))


def _log1p(y):
    w = 1.0 + y
    return jnp.where(w == 1.0, y, jnp.log(w) * (y / (w - 1.0)))


def _softplus(x):
    return jnp.maximum(x, 0.0) + _log1p(jnp.exp(-jnp.abs(x)))


def _sigmoid(x):
    return 0.5 * jnp.tanh(0.5 * x) + 0.5


def _lru_gates(x, first, cw_ref, cb_ref, wa_ref, ba_ref, wx_ref, bx_ref, lam_ref, tail_ref, h_ref):
    rows, width = x.shape

    @pl.when(first)
    def _():
        tail_ref[...] = jnp.zeros_like(tail_ref)
        h_ref[...] = jnp.zeros_like(h_ref)

    tail = tail_ref[...]
    tail_ref[...] = x[rows - SUBLANES:rows]
    row8 = lax.broadcasted_iota(jnp.int32, (SUBLANES, 1), 0)
    xc = cb_ref[...] + cw_ref[CONV_WIDTH - 1:CONV_WIDTH, :] * x
    for k in range(1, CONV_WIDTH):
        xs = pltpu.roll(x, k, 0)
        head = jnp.where(row8 < k, pltpu.roll(tail, k, 0), xs[0:SUBLANES])
        xs = jnp.concatenate([head, xs[SUBLANES:]], axis=0)
        xc = xc + cw_ref[CONV_WIDTH - 1 - k:CONV_WIDTH - k, :] * xs

    xcb = xc.astype(BF16)
    nslot = width // LANES
    ra = jnp.concatenate([jnp.dot(xcb[:, s * LANES:(s + 1) * LANES], wa_ref[s], preferred_element_type=F32)
                          for s in range(nslot)], axis=1)
    ri = jnp.concatenate([jnp.dot(xcb[:, s * LANES:(s + 1) * LANES], wx_ref[s], preferred_element_type=F32)
                          for s in range(nslot)], axis=1)
    r = _sigmoid(ra + ba_ref[...])
    i = _sigmoid(ri + bx_ref[...])
    log_a = (-LRU_C) * r * _softplus(-lam_ref[...])
    a = jnp.exp(log_a)
    y = -jnp.tanh(log_a) * (a * a + 1.0)
    return a, jnp.where(y > 0.0, y * lax.rsqrt(y), 0.0) * (i * xc)


def _lru_scan_groups(a, u):
    rows = a.shape[0]
    sub = jnp.bitwise_and(lax.broadcasted_iota(jnp.int32, (rows, 1), 0), SUBLANES - 1)
    acc_a, acc_h = a, u
    d = 1
    while d < SUBLANES:
        keep = sub >= d
        sh_a = pltpu.roll(acc_a, d, 0)
        sh_h = pltpu.roll(acc_h, d, 0)
        acc_h = jnp.where(keep, acc_a * sh_h + acc_h, acc_h)
        acc_a = jnp.where(keep, acc_a * sh_a, acc_a)
        d *= 2
    return acc_a, acc_h


def _lru_carry(acc_a, acc_h, h_ref):
    rows = acc_a.shape[0]
    carry = h_ref[0:1, :]
    groups = []
    for g in range(rows // SUBLANES):
        hg = acc_h[g * SUBLANES:(g + 1) * SUBLANES] + acc_a[g * SUBLANES:(g + 1) * SUBLANES] * carry
        carry = hg[SUBLANES - 1:SUBLANES]
        groups.append(hg)
    h_ref[0:1, :] = carry
    return jnp.concatenate(groups, axis=0)


def _lru_out(h, z, g_ref):
    ms = jnp.mean(h * h, axis=-1, keepdims=True)
    return (h * lax.rsqrt(ms + EPS) * g_ref[...]) * (z * _sigmoid(z))


def _proj_kernel(cols, rows_t, x_ref, mod_ref, ng_ref, w_ref, wt_ref, gq_ref, gks_ref, gkw_ref,
                 cw_ref, cb_ref, wa_ref, ba_ref, wx_ref, bx_ref, lam_ref, gl_ref,
                 ylru_ref, kcmp_ref, vcmp_ref, kslc_ref, kwin_ref,
                 qt_ref, vst_ref, vwt_ref, gt_ref, znsat_ref, tail_ref, h_ref):
    rows = x_ref.shape[1]
    x = x_ref[0]
    ms = jnp.mean(x * x, axis=-1, keepdims=True)
    gain = ng_ref[...] * (1.0 + mod_ref[0, 1:2, :])
    h = (x * lax.rsqrt(ms + EPS)) * gain + mod_ref[0, 0:1, :]
    hb = h.astype(BF16)

    def mm(name):
        c0, n = cols[name]
        return jnp.dot(hb, w_ref[:, c0:c0 + n], preferred_element_type=F32)

    def mmt(name):
        r0, n = rows_t[name]
        return _nt_dot(wt_ref[r0:r0 + n, :], hb)

    def after(value, done):
        return value + jnp.minimum(jnp.abs(done[done.shape[0] - 1:, 0:1]), 0.0)

    lru_a, lru_u = _lru_gates(mm("xlru"), pl.program_id(1) == 0, cw_ref, cb_ref, wa_ref, ba_ref,
                              wx_ref, bx_ref, lam_ref, tail_ref, h_ref)
    z_lru = mm("zlru")
    kcmp_ref[0] = mm("kcmp")
    vcmp_ref[0] = mm("vcmp")

    pos = pl.program_id(1) * rows + lax.broadcasted_iota(jnp.int32, (rows, 1), 0)
    kslc_ref[0, :, 0:LANES] = _half_rms(mm("kslc"), gks_ref[...]).astype(BF16)
    kslc_ref[0, :, LANES:2 * LANES] = _key_features(pos, True).astype(BF16)
    k_win = mm("kwin")
    kwin_ref[0, :, 0:LANES] = _half_rms(k_win, gkw_ref[...]).astype(BF16)
    kwin_ref[0, :, LANES:2 * LANES] = _key_features(pos, True).astype(BF16)

    lru_a, lru_h = _lru_scan_groups(lru_a, after(lru_u, k_win))
    qt = mmt("q")
    for hd in range(NSA_HEADS):
        qh = qt[hd * HEAD_DIM:(hd + 1) * HEAD_DIM]
        r = lax.rsqrt(jnp.mean(qh * qh, axis=0, keepdims=True) + EPS)
        qt_ref[0, hd * HEAD_DIM:(hd + 1) * HEAD_DIM, :] = (qh * r * gq_ref[...]).astype(BF16)

    h_lru = _lru_carry(lru_a, after(lru_h, qt), h_ref)
    vst = mmt("vslc").astype(BF16)
    vwt = mmt("vwin").astype(BF16)
    for j in range(rows // KEY_CHUNK):
        vst_ref[0, j] = vst[:, j * KEY_CHUNK:(j + 1) * KEY_CHUNK]
        vwt_ref[0, j] = vwt[:, j * KEY_CHUNK:(j + 1) * KEY_CHUNK]
    gt_ref[0] = jax.nn.sigmoid(mmt("gates"))
    z_nsa = mmt("znsa")
    znsat_ref[0] = z_nsa.astype(BF16)
    ylru_ref[0] = _lru_out(after(h_lru, z_nsa), z_lru, gl_ref).astype(BF16)


def _proj(x, mod3, norm_g, w_big, cols, w_t, rows_t, gq_b, gks2, gkw2, lru_params):
    bsz, seq, d = x.shape
    ts = PROJ_ROWS
    nsa_w = NSA_HEADS * HEAD_DIM
    lru_w = cols["xlru"][1]
    n_gate = rows_t["gates"][1]

    def tok_spec(n):
        return pl.BlockSpec((1, ts, n), lambda b, s: (b, s, 0))

    def feat_spec(n):
        return pl.BlockSpec((1, n, ts), lambda b, s: (b, 0, s))

    def chunk_spec():
        return pl.BlockSpec((1, ts // KEY_CHUNK, LANES, KEY_CHUNK), lambda b, s: (b, s, 0, 0))

    def const_spec(shape):
        return pl.BlockSpec(shape, lambda b, s: (0,) * len(shape))

    out_shape = (
        jax.ShapeDtypeStruct((bsz, seq, lru_w), BF16),
        jax.ShapeDtypeStruct((bsz, seq, LANES), F32),
        jax.ShapeDtypeStruct((bsz, seq, LANES), F32),
        jax.ShapeDtypeStruct((bsz, seq, 2 * LANES), BF16),
        jax.ShapeDtypeStruct((bsz, seq, 2 * LANES), BF16),
        jax.ShapeDtypeStruct((bsz, nsa_w, seq), BF16),
        jax.ShapeDtypeStruct((bsz, seq // KEY_CHUNK, LANES, KEY_CHUNK), BF16),
        jax.ShapeDtypeStruct((bsz, seq // KEY_CHUNK, LANES, KEY_CHUNK), BF16),
        jax.ShapeDtypeStruct((bsz, n_gate, seq), F32),
        jax.ShapeDtypeStruct((bsz, nsa_w, seq), BF16),
    )
    out_specs = (tok_spec(lru_w), tok_spec(LANES), tok_spec(LANES),
                 tok_spec(2 * LANES), tok_spec(2 * LANES),
                 feat_spec(nsa_w), chunk_spec(), chunk_spec(), feat_spec(n_gate), feat_spec(nsa_w))
    return pl.pallas_call(
        functools.partial(_proj_kernel, cols, rows_t),
        out_shape=out_shape,
        grid=(bsz, seq // ts),
        in_specs=[tok_spec(d),
                  pl.BlockSpec((1, 3, d), lambda b, s: (b, 0, 0)),
                  const_spec((1, d)),
                  const_spec(w_big.shape), const_spec(w_t.shape),
                  const_spec((HEAD_DIM, ts)), const_spec((1, LANES)), const_spec((1, LANES))]
                 + [const_spec(p.shape) for p in lru_params],
        out_specs=out_specs,
        scratch_shapes=[pltpu.VMEM((SUBLANES, lru_w), F32), pltpu.VMEM((SUBLANES, lru_w), F32)],
        compiler_params=pltpu.CompilerParams(dimension_semantics=("parallel", "arbitrary"),
                                             vmem_limit_bytes=VMEM_LIMIT),
        name="in_proj_lru",
    )(x, mod3, norm_g, w_big, w_t, gq_b, gks2, gkw2, *lru_params)


def _compress_kernel(is_key, c_ref, w1x_ref, pos_ref, w1_ref, w2lo_ref, w2hi_ref, g_ref, o_ref, const_ref):
    nblk = c_ref.shape[1] // CMP_STRIDE
    hid = w1_ref.shape[1]

    @pl.when(pl.program_id(0) == 0)
    def _():
        const_ref[...] = jnp.dot(pos_ref[...], w1_ref[...], preferred_element_type=F32, precision=HIGHEST)

    ab = jnp.zeros((nblk, 4 * hid), F32)
    for j in range(0, CMP_STRIDE, 2):
        pair = jnp.concatenate([c_ref[0, pl.ds(j + i, nblk, stride=CMP_STRIDE), :].astype(BF16) for i in range(2)],
                               axis=1)
        ab = ab + jnp.dot(pair, w1x_ref[j * LANES:(j + 2) * LANES, :], preferred_element_type=F32)
    const = const_ref[0:1, :]
    acts = []
    for hk in range(NSA_KV_HEADS):
        first = ab[:, (2 * hk) * hid:(2 * hk + 1) * hid]
        second = ab[:, (2 * hk + 1) * hid:(2 * hk + 2) * hid]
        hidden = first + pltpu.roll(second, nblk - 1, 0) + const
        acts.append(jax.nn.gelu(hidden).astype(BF16))
    if is_key:
        out = (jnp.dot(acts[0], w2lo_ref[...], preferred_element_type=F32)
               + jnp.dot(acts[1], w2hi_ref[...], preferred_element_type=F32))
        o_ref[0, :, 0:LANES] = _half_rms(out, g_ref[...]).astype(BF16)
        cend = CMP_STRIDE * lax.broadcasted_iota(jnp.int32, (nblk, 1), 0) + (CMP_BLOCK - 1)
        o_ref[0, :, LANES:2 * LANES] = _key_features(cend, False).astype(BF16)
    else:
        out_t = _nt_dot(w2lo_ref[...], acts[0]) + _nt_dot(w2hi_ref[...], acts[1])
        o_ref[0] = out_t.astype(BF16)


def _compress(tokens, w1x, pos_flat, w1, w2lo, w2hi, gain2, is_key):
    bsz, seq, width = tokens.shape
    nblk = seq // CMP_STRIDE
    hid = w1.shape[1]

    def const_spec(shape):
        return pl.BlockSpec(shape, lambda b: (0,) * len(shape))

    out_tail = (nblk, 2 * LANES) if is_key else (LANES, nblk)
    return pl.pallas_call(
        functools.partial(_compress_kernel, is_key),
        out_shape=jax.ShapeDtypeStruct((bsz,) + out_tail, BF16),
        grid=(bsz,),
        in_specs=[pl.BlockSpec((1, seq, width), lambda b: (b, 0, 0)),
                  const_spec(w1x.shape), const_spec((8, w1.shape[0])), const_spec(w1.shape),
                  const_spec(w2lo.shape), const_spec(w2hi.shape), const_spec((1, LANES))],
        out_specs=pl.BlockSpec((1,) + out_tail, lambda b: (b, 0, 0)),
        scratch_shapes=[pltpu.VMEM((SUBLANES, hid), F32)],
        compiler_params=pltpu.CompilerParams(dimension_semantics=("arbitrary",), vmem_limit_bytes=VMEM_LIMIT),
        name="compress_k" if is_key else "compress_v",
    )(tokens, w1x, pos_flat, w1, w2lo, w2hi, gain2)


ONES_ROWS = BF16_SUBLANES


def _with_ones(vt):
    return jnp.concatenate([vt, jnp.ones((ONES_ROWS, vt.shape[1]), BF16)], axis=0)


def _online_update(carry, s, top, vt):
    m, acc = carry
    m_new = jnp.maximum(m, top)
    p = jnp.exp2(s - m_new)
    acc = jnp.exp2(m - m_new) * acc + jnp.dot(_with_ones(vt), p.astype(BF16), preferred_element_type=F32)
    return m_new, acc


def _normalised(acc):
    return acc[0:HEAD_DIM] * (1.0 / acc[HEAD_DIM:HEAD_DIM + 1])


def _attn_kernel(qt_ref, kc_ref, vct_ref, ks_ref, vst_ref, kw_ref, vwt_ref, gt_ref, ovt_ref,
                 o_ref, impt_ref, s_ref, sel_ref):
    tq = qt_ref.shape[2]
    n_cmp = kc_ref.shape[1]
    n_sel = impt_ref.shape[1]
    ncol = NSA_GROUP * tq
    per_wide = SLC_KEYS // KEY_CHUNK
    per_tile = tq // KEY_CHUNK
    qi = pl.program_id(1)
    t0 = qi * tq
    c0 = qi * per_tile
    tloc = lax.broadcasted_iota(jnp.int32, (1, tq), 1)
    tcol = jnp.concatenate([tloc] * NSA_GROUP, axis=1)
    tpos = t0 + tloc
    nrow = lax.broadcasted_iota(jnp.int32, (n_sel, 1), 0)
    cur = jnp.right_shift(tpos, SEL_SHIFT)
    forced = jnp.logical_or(jnp.logical_or(nrow == 0, nrow == cur), nrow == cur - 1)
    frow = lax.broadcasted_iota(jnp.int32, (HEAD_DIM, 1), 0)
    srow = lax.broadcasted_iota(jnp.int32, (SUBLANES, 1), 0)
    zeros_q = jnp.zeros((HEAD_DIM, tq), BF16)
    row_minus_col = lax.broadcasted_iota(jnp.int32, (tq, 1), 0) - tcol
    own_ok = row_minus_col <= 0

    cidx = lax.broadcasted_iota(jnp.int32, (n_cmp, 1), 0)
    cend = CMP_STRIDE * cidx + (CMP_BLOCK - 1)
    cmp_ok = jnp.logical_and(cend <= t0 + tcol, cidx < n_cmp - 1)
    col_has_cmp = (t0 + tcol) >= CMP_BLOCK - 1
    gates = gt_ref[0]
    kv_heads = range(NSA_KV_HEADS)

    def values(ref, hk, first_chunk, n):
        return jnp.concatenate([ref[0, first_chunk + i, hk * HEAD_DIM:(hk + 1) * HEAD_DIM, :]
                                for i in range(n)], axis=1)

    not_past = jnp.logical_and(frow >= FEAT_CHUNK - SEL_BLOCK, frow - (FEAT_CHUNK - SEL_BLOCK) >= c0)

    q_plain, q_plain_past, o_cmp = [], [], []
    for hk in kv_heads:
        q_rows, feat_rows, feat_rows_past = [], [], []
        for g in range(NSA_GROUP):
            hd = hk * NSA_GROUP + g
            qh = qt_ref[0, hd * HEAD_DIM:(hd + 1) * HEAD_DIM, :]
            q_rows.append([qh, zeros_q] if hk == 0 else [zeros_q, qh])
            c = _alibi_coef(hd)
            a = jnp.zeros((HEAD_DIM, 1), F32)
            for i, term in enumerate(_bf16_terms(c)):
                a = jnp.where(frow == FEAT_BLK - SEL_BLOCK + i, SEL_BLOCK * term, a)
                a = jnp.where(frow == FEAT_OFF - SEL_BLOCK + i, term, a)
            a = jnp.where(frow == FEAT_ONE - SEL_BLOCK, -c * t0.astype(F32), a)
            feat_rows.append(jnp.broadcast_to(a, (HEAD_DIM, tq)).astype(BF16))
            feat_rows_past.append(jnp.broadcast_to(jnp.where(not_past, NEG_INF, a), (HEAD_DIM, tq)).astype(BF16))

        def q_operand(pen, feats, q_rows=q_rows):
            return jnp.concatenate([jnp.concatenate(q_rows[g] + [pen, feats[g]], axis=0)
                                    for g in range(NSA_GROUP)], axis=1)

        q_plain.append(q_operand(zeros_q, feat_rows))
        q_plain_past.append(q_operand(zeros_q, feat_rows_past))
        sel_ref[0, hk] = q_plain[hk]
        sel_ref[1, hk] = q_plain_past[hk]

        s = jnp.dot(kc_ref[0], q_plain[hk], preferred_element_type=F32)
        s = jnp.where(cmp_ok, s, NEG_INF)
        e = jnp.exp2(s - jnp.max(s, axis=0, keepdims=True))
        p = e * jnp.where(col_has_cmp, 1.0 / jnp.sum(e, axis=0, keepdims=True), 0.0)
        o_cmp.append(jnp.dot(vct_ref[0, hk * HEAD_DIM:(hk + 1) * HEAD_DIM, :], p.astype(BF16),
                             preferred_element_type=F32))
        p_sum = p[:, 0:tq]
        for g in range(1, NSA_GROUP):
            p_sum = p_sum + p[:, g * tq:(g + 1) * tq]
        imp = jnp.zeros((n_sel, tq), F32)
        rest = p_sum
        for _ in range(N_SPLIT):
            term = rest.astype(BF16)
            imp = imp + jnp.dot(ovt_ref[...], term, preferred_element_type=F32)
            rest = rest - term.astype(F32)
        imp = jnp.where(forced, FORCE, jnp.where(nrow > cur, -FORCE, imp))

        impt_ref[hk] = imp
        groups = [imp[b * SUBLANES:(b + 1) * SUBLANES] for b in range(n_sel // SUBLANES)]
        ranks = [jnp.zeros((SUBLANES, tq), F32) for _ in groups]
        for m in range(n_sel):
            other = impt_ref[hk, m:m + 1, :]
            for b, grp in enumerate(groups):
                if b * SUBLANES > m:
                    ahead = other >= grp
                elif (b + 1) * SUBLANES <= m:
                    ahead = other > grp
                else:
                    ahead = jnp.logical_or(other > grp, jnp.logical_and(other == grp, srow > m - b * SUBLANES))
                ranks[b] = ranks[b] + jnp.where(ahead, 1.0, 0.0)
        rank = jnp.concatenate(ranks, axis=0)
        pen = jnp.where(rank < float(SEL_TOPK), 0.0, NEG_INF).astype(BF16)
        pen_rows = jnp.concatenate([pen] * NSA_GROUP, axis=1)
        sel_ref[0, hk, 2 * HEAD_DIM:3 * HEAD_DIM, :] = pen_rows
        sel_ref[1, hk, 2 * HEAD_DIM:3 * HEAD_DIM, :] = pen_rows

    init = (jnp.full((1, ncol), NEG_INF, F32), jnp.zeros((HEAD_DIM + ONES_ROWS, ncol), F32))

    n_back = WINDOW // KEY_CHUNK
    wc0 = jnp.maximum(c0 - n_back, 0)
    lead = t0 - wc0 * KEY_CHUNK
    k_back = kw_ref[0, pl.ds(pl.multiple_of(wc0 * KEY_CHUNK, KEY_CHUNK), WINDOW), :]
    k_tile = kw_ref[0, pl.ds(pl.multiple_of(t0, KEY_CHUNK), tq), :]
    recent = row_minus_col > lead - WINDOW
    o_win = []
    for hk in kv_heads:
        s_back = jnp.dot(k_back, q_plain_past[hk], preferred_element_type=F32)
        s_own = jnp.where(own_ok, jnp.dot(k_tile, q_plain[hk], preferred_element_type=F32), NEG_INF)
        s = jnp.concatenate([jnp.where(recent, s_back[0:tq], NEG_INF), s_back[tq:], s_own], axis=0)
        vt = jnp.concatenate([values(vwt_ref, hk, wc0, n_back), values(vwt_ref, hk, c0, per_tile)], axis=1)
        _, acc = _online_update(init, s, jnp.max(s, axis=0, keepdims=True), vt)
        o_win.append(_normalised(acc))

    k_own = ks_ref[0, pl.ds(pl.multiple_of(t0, KEY_CHUNK), tq), :]
    stats = []
    for hk in kv_heads:
        s = jnp.where(own_ok, jnp.dot(k_own, sel_ref[0, hk], preferred_element_type=F32), NEG_INF)
        stats.append(_online_update(init, s, jnp.max(s, axis=0, keepdims=True), values(vst_ref, hk, c0, per_tile)))
    stats = tuple(stats)

    def produce(j, slot):
        k = ks_ref[0, j * SLC_KEYS:(j + 1) * SLC_KEYS, :]
        tops = []
        for hk in kv_heads:
            s = jnp.dot(k, sel_ref[1, hk], preferred_element_type=F32)
            s_ref[slot, hk] = s
            tops.append(jnp.max(s, axis=0, keepdims=True))
        return tuple(tops)

    def consume(j, slot, tops, stats):
        return tuple(_online_update(stats[hk], s_ref[slot, hk], tops[hk],
                                    values(vst_ref, hk, j * per_wide, per_wide)) for hk in kv_heads)

    def chain(n_blocks):
        def run(stats):
            tops = produce(0, 0) if n_blocks else None
            for j in range(n_blocks):
                nxt = produce(j + 1, (j + 1) % 2) if j + 1 < n_blocks else None
                stats = consume(j, j % 2, tops, stats)
                tops = nxt
            return stats
        return run

    stats = lax.switch((t0 + SLC_KEYS - 1) // SLC_KEYS,
                       [chain(n) for n in range(ks_ref.shape[1] // SLC_KEYS + 1)], stats)
    o_slc = [_normalised(acc) for _, acc in stats]

    for hk in kv_heads:
        for g in range(NSA_GROUP):
            hd = hk * NSA_GROUP + g
            tot = jnp.zeros((HEAD_DIM, tq), F32)
            for br, o_br in enumerate((o_cmp[hk], o_slc[hk], o_win[hk])):
                r = br * NSA_HEADS + hd
                tot = tot + gates[r:r + 1, :] * o_br[:, g * tq:(g + 1) * tq]
            o_ref[0, hd * HEAD_DIM:(hd + 1) * HEAD_DIM, :] = tot.astype(BF16)


def _attn(qt, kc, vct, kslc, vst, kwin, vwt, gt, overlap_t):
    bsz, nsa_w, seq = qt.shape
    tq = ATTN_ROWS
    n_cmp = kc.shape[1]
    n_sel = seq // SEL_BLOCK
    n_gate = gt.shape[1]
    assert seq % (2 * SLC_KEYS) == 0 and n_sel <= HALF and seq >= WINDOW + tq
    assert seq // KEY_CHUNK <= LANES - FEAT_CHUNK and n_sel % SUBLANES == 0

    def per_batch(shape):
        return pl.BlockSpec((1,) + shape, lambda b, i: (b,) + (0,) * len(shape))

    def const_spec(shape):
        return pl.BlockSpec(shape, lambda b, i: (0,) * len(shape))

    return pl.pallas_call(
        _attn_kernel,
        out_shape=jax.ShapeDtypeStruct((bsz, nsa_w, seq), BF16),
        grid=(bsz, seq // tq),
        in_specs=[pl.BlockSpec((1, nsa_w, tq), lambda b, i: (b, 0, i)),
                  per_batch((n_cmp, 2 * LANES)), per_batch((LANES, n_cmp)),
                  per_batch((seq, 2 * LANES)), per_batch((seq // KEY_CHUNK, LANES, KEY_CHUNK)),
                  per_batch((seq, 2 * LANES)), per_batch((seq // KEY_CHUNK, LANES, KEY_CHUNK)),
                  pl.BlockSpec((1, n_gate, tq), lambda b, i: (b, 0, i)),
                  const_spec(overlap_t.shape)],
        out_specs=pl.BlockSpec((1, nsa_w, tq), lambda b, i: (b, 0, i)),
        scratch_shapes=[pltpu.VMEM((NSA_KV_HEADS, n_sel, tq), F32),
                        pltpu.VMEM((2, NSA_KV_HEADS, SLC_KEYS, NSA_GROUP * tq), F32),
                        pltpu.VMEM((2, NSA_KV_HEADS, 2 * LANES, NSA_GROUP * tq), BF16)],
        compiler_params=pltpu.CompilerParams(dimension_semantics=("parallel", "arbitrary"),
                                             vmem_limit_bytes=VMEM_LIMIT),
        name="nsa_attn",
    )(qt, kc, vct, kslc, vst, kwin, vwt, gt, overlap_t)


def _out_kernel(x_ref, ylru_ref, onsat_ref, znsat_ref, gn_ref, gate_ref, wl_ref, wn_ref, o_ref):
    o = onsat_ref[0].astype(F32)
    ms = jnp.mean(o * o, axis=0, keepdims=True)
    z = znsat_ref[0].astype(F32)
    y_t = (o * lax.rsqrt(ms + EPS) * gn_ref[...]) * (z * jax.nn.sigmoid(z))
    y_nsa = jnp.transpose(y_t).astype(BF16)
    out = (jnp.dot(ylru_ref[0], wl_ref[...], preferred_element_type=F32)
           + jnp.dot(y_nsa, wn_ref[...], preferred_element_type=F32))
    o_ref[0] = x_ref[0] + gate_ref[0, 2:3, :] * out


def _out(x, ylru, onsat, znsat, g_nsa_b, mod3, w_lru, w_nsa):
    bsz, seq, d = x.shape
    ts = OUT_ROWS
    nsa_w = onsat.shape[1]

    def tok_spec(n):
        return pl.BlockSpec((1, ts, n), lambda b, s: (b, s, 0))

    def feat_spec(n):
        return pl.BlockSpec((1, n, ts), lambda b, s: (b, 0, s))

    def const_spec(shape):
        return pl.BlockSpec(shape, lambda b, s: (0,) * len(shape))

    return pl.pallas_call(
        _out_kernel,
        out_shape=jax.ShapeDtypeStruct((bsz, seq, d), F32),
        grid=(bsz, seq // ts),
        in_specs=[tok_spec(d), tok_spec(ylru.shape[2]), feat_spec(nsa_w), feat_spec(nsa_w),
                  const_spec((nsa_w, ts)),
                  pl.BlockSpec((1, 3, d), lambda b, s: (b, 0, 0)),
                  const_spec(w_lru.shape), const_spec(w_nsa.shape)],
        out_specs=tok_spec(d),
        compiler_params=pltpu.CompilerParams(dimension_semantics=("parallel", "parallel"),
                                             vmem_limit_bytes=VMEM_LIMIT),
        name="out_proj",
    )(x, ylru, onsat, znsat, g_nsa_b, mod3, w_lru, w_nsa)


def _overlap_t(n_cmp_pad, n_sel):
    ratio = SEL_BLOCK // CMP_STRIDE
    ov = np.zeros((n_sel, n_cmp_pad), np.float32)
    for c in range(n_cmp_pad - 1):
        for n in (c // ratio, (c + 1) // ratio):
            if n < n_sel:
                ov[n, c] += 1.0
    return ov


def _block_diag_pairs(w):
    nb, bs, _ = w.shape
    z = jnp.zeros((bs, bs), w.dtype)
    return jnp.stack([jnp.block([[w[2 * s], z], [z, w[2 * s + 1]]]) for s in range(nb // 2)])


def _compress_weights(w1, w2, transposed):
    hid = w1.shape[1]
    w1r = w1.reshape(2, CMP_STRIDE, HEAD_DIM, hid)
    parts = []
    for hk in range(NSA_KV_HEADS):
        for half in range(2):
            slot = jnp.zeros((CMP_STRIDE, NSA_KV_HEADS, HEAD_DIM, hid), w1.dtype).at[:, hk].set(w1r[half])
            parts.append(slot.reshape(CMP_STRIDE * LANES, hid))
    w1x = jnp.concatenate(parts, axis=1).astype(BF16)
    zpad = jnp.zeros_like(w2)
    w2lo = jnp.concatenate([w2, zpad], axis=1).astype(BF16)
    w2hi = jnp.concatenate([zpad, w2], axis=1).astype(BF16)
    if transposed:
        w2lo, w2hi = w2lo.T, w2hi.T
    return w1x, w2lo, w2hi


def _pad_to(n, m):
    return -(-n // m) * m


def _layer(x, c, w_ada, b_ada, norm_g, w_in, conv_w, conv_b, w_rg_a, b_rg_a, w_rg_x, b_rg_x, lru_lambda,
           cmp_pos_k, cmp_w1_k, cmp_w2_k, cmp_pos_v, cmp_w1_v, cmp_w2_v, g_q, g_k_cmp, g_k_slc, g_k_win,
           g_out_lru, g_out_nsa, w_out):
    bsz, seq, d = x.shape
    lru_w = d // 2
    nsa_w = NSA_HEADS * HEAD_DIM
    kv_w = NSA_KV_HEADS * HEAD_DIM
    n_sel = seq // SEL_BLOCK

    splits = (lru_w, lru_w, nsa_w, kv_w, kv_w, kv_w, kv_w, kv_w, kv_w, N_BRANCH * NSA_HEADS, nsa_w)
    offs = np.concatenate([[0], np.cumsum(splits)])
    names = ("xlru", "zlru", "q", "kcmp", "vcmp", "kslc", "vslc", "kwin", "vwin", "gates", "znsa")
    src = {n: (int(offs[i]), int(offs[i + 1])) for i, n in enumerate(names)}

    def plan(group, align):
        pieces, where, at = [], {}, 0
        for n in group:
            piece = w_in[:, src[n][0]:src[n][1]]
            width = _pad_to(piece.shape[1], align)
            if width != piece.shape[1]:
                piece = jnp.pad(piece, ((0, 0), (0, width - piece.shape[1])))
            pieces.append(piece)
            where[n] = (at, width)
            at += width
        return jnp.concatenate(pieces, axis=1).astype(BF16), where

    w_big, cols = plan(("xlru", "zlru", "kcmp", "vcmp", "kslc", "kwin"), LANES)
    w_tt, rows_t = plan(("q", "vslc", "vwin", "gates", "znsa"), BF16_SUBLANES)
    w_t = w_tt.T

    def dup(g):
        return jnp.concatenate([g, g]).reshape(1, LANES)

    gq_b = jnp.broadcast_to((g_q * (ATTN_SCALE * LOG2E)).reshape(HEAD_DIM, 1), (HEAD_DIM, PROJ_ROWS))
    mod3 = _mod(c, w_ada, b_ada).reshape(bsz, 3, d)
    lru_params = (conv_w, conv_b.reshape(1, lru_w),
                  _block_diag_pairs(w_rg_a).astype(BF16), b_rg_a.reshape(1, lru_w),
                  _block_diag_pairs(w_rg_x).astype(BF16), b_rg_x.reshape(1, lru_w),
                  lru_lambda.reshape(1, lru_w), g_out_lru.reshape(1, lru_w))
    (ylru, kcmp, vcmp, kslc, kwin, qt, vst, vwt, gt, znsat) = _proj(
        x, mod3, norm_g.reshape(1, d), w_big, cols, w_t, rows_t, gq_b, dup(g_k_slc), dup(g_k_win), lru_params)

    n_chunk = seq // CMP_STRIDE
    w1x_k, w2lo_k, w2hi_k = _compress_weights(cmp_w1_k, cmp_w2_k, False)
    w1x_v, w2lo_v, w2hi_v = _compress_weights(cmp_w1_v, cmp_w2_v, True)

    def pos_rows(pos):
        return jnp.broadcast_to(pos.reshape(1, CMP_BLOCK * HEAD_DIM), (8, CMP_BLOCK * HEAD_DIM))

    kc = _compress(kcmp, w1x_k, pos_rows(cmp_pos_k), cmp_w1_k,
                   w2lo_k, w2hi_k, dup(g_k_cmp), True)
    vct = _compress(vcmp, w1x_v, pos_rows(cmp_pos_v), cmp_w1_v,
                    w2lo_v, w2hi_v, dup(g_k_cmp), False)

    onsat = _attn(qt, kc, vct, kslc, vst, kwin, vwt, gt, jnp.asarray(_overlap_t(n_chunk, n_sel), dtype=BF16))

    g_nsa_b = jnp.broadcast_to(g_out_nsa.reshape(nsa_w, 1), (nsa_w, OUT_ROWS))
    return _out(x, ylru, onsat, znsat, g_nsa_b, mod3, w_out[:lru_w].astype(BF16), w_out[lru_w:].astype(BF16))


def kernel(x, c, w_ada, b_ada, norm_g, w_in, conv_w, conv_b, w_rg_a, b_rg_a, w_rg_x, b_rg_x, lru_lambda, cmp_pos_k, cmp_w1_k, cmp_w2_k, cmp_pos_v, cmp_w1_v, cmp_w2_v, g_q, g_k_cmp, g_k_slc, g_k_win, g_out_lru, g_out_nsa, w_out):
    params = (w_ada, b_ada, norm_g, w_in, conv_w, conv_b, w_rg_a, b_rg_a, w_rg_x, b_rg_x, lru_lambda,
              cmp_pos_k, cmp_w1_k, cmp_w2_k, cmp_pos_v, cmp_w1_v, cmp_w2_v, g_q, g_k_cmp, g_k_slc, g_k_win,
              g_out_lru, g_out_nsa, w_out)
    for layer in range(w_in.shape[0]):
        x = _layer(x, c, *(p[layer] for p in params))
    return x
```

```python
import functools
import math

import ml_dtypes
import numpy as np
import jax
import jax.numpy as jnp
from jax import lax
from jax.experimental import pallas as pl
from jax.experimental.pallas import tpu as pltpu

F32 = jnp.float32
BF16 = jnp.bfloat16
HIGHEST = lax.Precision.HIGHEST

LANES = 128
HALF = LANES // 2
SUBLANES = 8
BF16_SUBLANES = 2 * SUBLANES

CONV_WIDTH = 4
LRU_C = 8.0
NSA_HEADS = 8
HEAD_DIM = 64
NSA_KV_HEADS = 2
NSA_GROUP = NSA_HEADS // NSA_KV_HEADS
N_BRANCH = 3
CMP_STRIDE = 16
CMP_BLOCK = 2 * CMP_STRIDE
SEL_BLOCK = 64
SEL_SHIFT = SEL_BLOCK.bit_length() - 1
SEL_TOPK = 16
WINDOW = 512
ATTN_SCALE = HEAD_DIM ** -0.5
LOG2E = math.log2(math.e)
NEG_INF = -1e30
FORCE = 1e6
EPS = 1e-6

PROJ_ROWS = 1024
OUT_ROWS = 1024
ATTN_ROWS = 128
SLC_KEYS = 512
KEY_CHUNK = 128
VMEM_LIMIT = 56 * 1024 * 1024

FEAT_BLK = SEL_BLOCK
FEAT_OFF = SEL_BLOCK + 3
FEAT_ONE = SEL_BLOCK + 6
MAX_KEY_CHUNKS = 32
FEAT_CHUNK = LANES - MAX_KEY_CHUNKS
N_SPLIT = 3

assert HEAD_DIM == HALF and NSA_KV_HEADS == 2 and KEY_CHUNK == LANES
assert ATTN_ROWS % KEY_CHUNK == 0 and ATTN_ROWS <= WINDOW


def _bf16_terms(value):
    terms, rest = [], np.float32(value)
    for _ in range(N_SPLIT):
        t = np.float32(rest).astype(ml_dtypes.bfloat16).astype(np.float32)
        terms.append(float(t))
        rest = np.float32(rest - t)
    return terms


def _alibi_coef(head):
    return float(2.0 ** (-8.0 * (head + 1) / NSA_HEADS)) * LOG2E


def _nt_dot(a, b):
    return lax.dot_general(a, b, (((1,), (1,)), ((), ())), preferred_element_type=F32)


def _half_rms(x, gain2):
    lane = lax.broadcasted_iota(jnp.int32, (1, LANES), 1)
    lo = lane < HALF
    sq = x * x
    ss_lo = jnp.sum(jnp.where(lo, sq, 0.0), axis=-1, keepdims=True)
    ss_hi = jnp.sum(jnp.where(lo, 0.0, sq), axis=-1, keepdims=True)
    r = jnp.where(lo, lax.rsqrt(ss_lo * (1.0 / HALF) + EPS), lax.rsqrt(ss_hi * (1.0 / HALF) + EPS))
    return x * r * gain2


def _key_features(pos, onehot):
    lane = lax.broadcasted_iota(jnp.int32, (1, LANES), 1)
    blk = jnp.right_shift(pos, SEL_SHIFT)
    off = jnp.bitwise_and(pos, SEL_BLOCK - 1)
    f = jnp.where(jnp.logical_and(lane >= FEAT_BLK, lane < FEAT_OFF), blk.astype(F32),
                  jnp.where(jnp.logical_and(lane >= FEAT_OFF, lane < FEAT_ONE), off.astype(F32),
                            jnp.where(lane == FEAT_ONE, 1.0, 0.0)))
    if onehot:
        f = jnp.where(lane < SEL_BLOCK, jnp.where(lane == blk, 1.0, 0.0), f)
        chunk = jnp.right_shift(pos, KEY_CHUNK.bit_length() - 1)
        f = jnp.where(lane >= FEAT_CHUNK, jnp.where(lane - FEAT_CHUNK == chunk, 1.0, 0.0), f)
    return f


def _mod_kernel(c_ref, w_ref, b_ref, o_ref):
    c = c_ref[...]
    o_ref[...] = jnp.dot(c * jax.nn.sigmoid(c), w_ref[...], preferred_element_type=F32, precision=HIGHEST) + b_ref[...]


def _mod(c, w_ada, b_ada):
    bsz, d = c.shape
    n = w_ada.shape[1]
    return pl.pallas_call(
        _mod_kernel,
        out_shape=jax.ShapeDtypeStruct((bsz, n), F32),
        grid=(n // d,),
        in_specs=[pl.BlockSpec((bsz, d), lambda j: (0, 0)),
                  pl.BlockSpec((d, d), lambda j: (0, j)),
                  pl.BlockSpec((1, d), lambda j: (0, j))],
        out_specs=pl.BlockSpec((bsz, d), lambda j: (0, j)),
        name="adaln_mod",
    )(c, w_ada, b_ada.reshape(1, n))


def _log1p(y):
    w = 1.0 + y
    return jnp.where(w == 1.0, y, jnp.log(w) * (y / (w - 1.0)))


def _softplus(x):
    return jnp.maximum(x, 0.0) + _log1p(jnp.exp(-jnp.abs(x)))


def _sigmoid(x):
    return 0.5 * jnp.tanh(0.5 * x) + 0.5


def _lru_gates(x, first, cw_ref, cb_ref, wa_ref, ba_ref, wx_ref, bx_ref, lam_ref, tail_ref, h_ref):
    rows, width = x.shape

    @pl.when(first)
    def _():
        tail_ref[...] = jnp.zeros_like(tail_ref)
        h_ref[...] = jnp.zeros_like(h_ref)

    tail = tail_ref[...]
    tail_ref[...] = x[rows - SUBLANES:rows]
    row8 = lax.broadcasted_iota(jnp.int32, (SUBLANES, 1), 0)
    xc = cb_ref[...] + cw_ref[CONV_WIDTH - 1:CONV_WIDTH, :] * x
    for k in range(1, CONV_WIDTH):
        xs = pltpu.roll(x, k, 0)
        head = jnp.where(row8 < k, pltpu.roll(tail, k, 0), xs[0:SUBLANES])
        xs = jnp.concatenate([head, xs[SUBLANES:]], axis=0)
        xc = xc + cw_ref[CONV_WIDTH - 1 - k:CONV_WIDTH - k, :] * xs

    xcb = xc.astype(BF16)
    nslot = width // LANES
    ra = jnp.concatenate([jnp.dot(xcb[:, s * LANES:(s + 1) * LANES], wa_ref[s], preferred_element_type=F32)
                          for s in range(nslot)], axis=1)
    ri = jnp.concatenate([jnp.dot(xcb[:, s * LANES:(s + 1) * LANES], wx_ref[s], preferred_element_type=F32)
                          for s in range(nslot)], axis=1)
    r = _sigmoid(ra + ba_ref[...])
    i = _sigmoid(ri + bx_ref[...])
    log_a = (-LRU_C) * r * _softplus(-lam_ref[...])
    a = jnp.exp(log_a)
    y = -jnp.tanh(log_a) * (a * a + 1.0)
    return a, jnp.where(y > 0.0, y * lax.rsqrt(y), 0.0) * (i * xc)


def _lru_scan_groups(a, u):
    rows = a.shape[0]
    sub = jnp.bitwise_and(lax.broadcasted_iota(jnp.int32, (rows, 1), 0), SUBLANES - 1)
    acc_a, acc_h = a, u
    d = 1
    while d < SUBLANES:
        keep = sub >= d
        sh_a = pltpu.roll(acc_a, d, 0)
        sh_h = pltpu.roll(acc_h, d, 0)
        acc_h = jnp.where(keep, acc_a * sh_h + acc_h, acc_h)
        acc_a = jnp.where(keep, acc_a * sh_a, acc_a)
        d *= 2
    return acc_a, acc_h


def _lru_carry(acc_a, acc_h, h_ref):
    rows = acc_a.shape[0]
    carry = h_ref[0:1, :]
    groups = []
    for g in range(rows // SUBLANES):
        hg = acc_h[g * SUBLANES:(g + 1) * SUBLANES] + acc_a[g * SUBLANES:(g + 1) * SUBLANES] * carry
        carry = hg[SUBLANES - 1:SUBLANES]
        groups.append(hg)
    h_ref[0:1, :] = carry
    return jnp.concatenate(groups, axis=0)


def _lru_out(h, z, g_ref):
    ms = jnp.mean(h * h, axis=-1, keepdims=True)
    return (h * lax.rsqrt(ms + EPS) * g_ref[...]) * (z * _sigmoid(z))


def _proj_kernel(cols, rows_t, x_ref, mod_ref, ng_ref, w_ref, wt_ref, gq_ref, gks_ref, gkw_ref,
                 cw_ref, cb_ref, wa_ref, ba_ref, wx_ref, bx_ref, lam_ref, gl_ref,
                 ylru_ref, kcmp_ref, vcmp_ref, kslc_ref, kwin_ref,
                 qt_ref, vst_ref, vwt_ref, gt_ref, znsat_ref, tail_ref, h_ref):
    rows = x_ref.shape[1]
    x = x_ref[0]
    ms = jnp.mean(x * x, axis=-1, keepdims=True)
    gain = ng_ref[...] * (1.0 + mod_ref[0, 1:2, :])
    h = (x * lax.rsqrt(ms + EPS)) * gain + mod_ref[0, 0:1, :]
    hb = h.astype(BF16)

    def mm(name):
        c0, n = cols[name]
        return jnp.dot(hb, w_ref[:, c0:c0 + n], preferred_element_type=F32)

    def mm_pair(first, second):
        (c0, n0), (c1, n1) = cols[first], cols[second]
        assert c1 == c0 + n0
        both = jnp.dot(hb, w_ref[:, c0:c1 + n1], preferred_element_type=F32)
        return both[:, 0:n0], both[:, n0:n0 + n1]

    def mmt(*names):
        spans = [rows_t[name] for name in names]
        r0 = spans[0][0]
        assert all(a[0] + a[1] == b[0] for a, b in zip(spans, spans[1:]))
        out = _nt_dot(wt_ref[r0:spans[-1][0] + spans[-1][1], :], hb)
        parts = tuple(out[r - r0:r - r0 + n] for r, n in spans)
        return parts[0] if len(parts) == 1 else parts

    def after(value, done):
        return value + jnp.minimum(jnp.abs(done[done.shape[0] - 1:, 0:1]), 0.0)

    lru_a, lru_u = _lru_gates(mm("xlru"), pl.program_id(1) == 0, cw_ref, cb_ref, wa_ref, ba_ref,
                              wx_ref, bx_ref, lam_ref, tail_ref, h_ref)
    z_lru = mm("zlru")
    kcmp_ref[0], vcmp_ref[0] = mm_pair("kcmp", "vcmp")

    pos = pl.program_id(1) * rows + lax.broadcasted_iota(jnp.int32, (rows, 1), 0)
    k_slc, k_win = mm_pair("kslc", "kwin")
    kslc_ref[0, :, 0:LANES] = _half_rms(k_slc, gks_ref[...]).astype(BF16)
    kslc_ref[0, :, LANES:2 * LANES] = _key_features(pos, True).astype(BF16)
    kwin_ref[0, :, 0:LANES] = _half_rms(k_win, gkw_ref[...]).astype(BF16)
    kwin_ref[0, :, LANES:2 * LANES] = _key_features(pos, True).astype(BF16)

    lru_a, lru_h = _lru_scan_groups(lru_a, after(lru_u, k_win))
    qt = mmt("q")
    for hd in range(NSA_HEADS):
        qh = qt[hd * HEAD_DIM:(hd + 1) * HEAD_DIM]
        r = lax.rsqrt(jnp.mean(qh * qh, axis=0, keepdims=True) + EPS)
        qt_ref[0, hd * HEAD_DIM:(hd + 1) * HEAD_DIM, :] = (qh * r * gq_ref[...]).astype(BF16)

    h_lru = _lru_carry(lru_a, after(lru_h, qt), h_ref)
    vst, vwt, gate_logits, z_nsa = mmt("vslc", "vwin", "gates", "znsa")
    vst, vwt = vst.astype(BF16), vwt.astype(BF16)
    for j in range(rows // KEY_CHUNK):
        vst_ref[0, j] = vst[:, j * KEY_CHUNK:(j + 1) * KEY_CHUNK]
        vwt_ref[0, j] = vwt[:, j * KEY_CHUNK:(j + 1) * KEY_CHUNK]
    gt_ref[0] = jax.nn.sigmoid(gate_logits)
    znsat_ref[0] = z_nsa.astype(BF16)
    ylru_ref[0] = _lru_out(after(h_lru, z_nsa), z_lru, gl_ref).astype(BF16)


def _proj(x, mod3, norm_g, w_big, cols, w_t, rows_t, gq_b, gks2, gkw2, lru_params):
    bsz, seq, d = x.shape
    ts = PROJ_ROWS
    nsa_w = NSA_HEADS * HEAD_DIM
    lru_w = cols["xlru"][1]
    n_gate = rows_t["gates"][1]

    def tok_spec(n):
        return pl.BlockSpec((1, ts, n), lambda b, s: (b, s, 0))

    def feat_spec(n):
        return pl.BlockSpec((1, n, ts), lambda b, s: (b, 0, s))

    def chunk_spec():
        return pl.BlockSpec((1, ts // KEY_CHUNK, LANES, KEY_CHUNK), lambda b, s: (b, s, 0, 0))

    def const_spec(shape):
        return pl.BlockSpec(shape, lambda b, s: (0,) * len(shape))

    out_shape = (
        jax.ShapeDtypeStruct((bsz, seq, lru_w), BF16),
        jax.ShapeDtypeStruct((bsz, seq, LANES), F32),
        jax.ShapeDtypeStruct((bsz, seq, LANES), F32),
        jax.ShapeDtypeStruct((bsz, seq, 2 * LANES), BF16),
        jax.ShapeDtypeStruct((bsz, seq, 2 * LANES), BF16),
        jax.ShapeDtypeStruct((bsz, nsa_w, seq), BF16),
        jax.ShapeDtypeStruct((bsz, seq // KEY_CHUNK, LANES, KEY_CHUNK), BF16),
        jax.ShapeDtypeStruct((bsz, seq // KEY_CHUNK, LANES, KEY_CHUNK), BF16),
        jax.ShapeDtypeStruct((bsz, n_gate, seq), F32),
        jax.ShapeDtypeStruct((bsz, nsa_w, seq), BF16),
    )
    out_specs = (tok_spec(lru_w), tok_spec(LANES), tok_spec(LANES),
                 tok_spec(2 * LANES), tok_spec(2 * LANES),
                 feat_spec(nsa_w), chunk_spec(), chunk_spec(), feat_spec(n_gate), feat_spec(nsa_w))
    return pl.pallas_call(
        functools.partial(_proj_kernel, cols, rows_t),
        out_shape=out_shape,
        grid=(bsz, seq // ts),
        in_specs=[tok_spec(d),
                  pl.BlockSpec((1, 3, d), lambda b, s: (b, 0, 0)),
                  const_spec((1, d)),
                  const_spec(w_big.shape), const_spec(w_t.shape),
                  const_spec((HEAD_DIM, ts)), const_spec((1, LANES)), const_spec((1, LANES))]
                 + [const_spec(p.shape) for p in lru_params],
        out_specs=out_specs,
        scratch_shapes=[pltpu.VMEM((SUBLANES, lru_w), F32), pltpu.VMEM((SUBLANES, lru_w), F32)],
        compiler_params=pltpu.CompilerParams(dimension_semantics=("parallel", "arbitrary"),
                                             vmem_limit_bytes=VMEM_LIMIT),
        name="in_proj_lru",
    )(x, mod3, norm_g, w_big, w_t, gq_b, gks2, gkw2, *lru_params)


def _compress_kernel(is_key, c_ref, w1x_ref, pos_ref, w1_ref, w2lo_ref, w2hi_ref, g_ref, o_ref, const_ref):
    nblk = c_ref.shape[1] // CMP_STRIDE
    hid = w1_ref.shape[1]

    @pl.when(pl.program_id(0) == 0)
    def _():
        const_ref[...] = jnp.dot(pos_ref[...], w1_ref[...], preferred_element_type=F32, precision=HIGHEST)

    ab = jnp.zeros((nblk, 4 * hid), F32)
    for j in range(0, CMP_STRIDE, 2):
        pair = jnp.concatenate([c_ref[0, pl.ds(j + i, nblk, stride=CMP_STRIDE), :].astype(BF16) for i in range(2)],
                               axis=1)
        ab = ab + jnp.dot(pair, w1x_ref[j * LANES:(j + 2) * LANES, :], preferred_element_type=F32)
    const = const_ref[0:1, :]
    acts = []
    for hk in range(NSA_KV_HEADS):
        first = ab[:, (2 * hk) * hid:(2 * hk + 1) * hid]
        second = ab[:, (2 * hk + 1) * hid:(2 * hk + 2) * hid]
        hidden = first + pltpu.roll(second, nblk - 1, 0) + const
        acts.append(jax.nn.gelu(hidden).astype(BF16))
    if is_key:
        out = (jnp.dot(acts[0], w2lo_ref[...], preferred_element_type=F32)
               + jnp.dot(acts[1], w2hi_ref[...], preferred_element_type=F32))
        o_ref[0, :, 0:LANES] = _half_rms(out, g_ref[...]).astype(BF16)
        cend = CMP_STRIDE * lax.broadcasted_iota(jnp.int32, (nblk, 1), 0) + (CMP_BLOCK - 1)
        o_ref[0, :, LANES:2 * LANES] = _key_features(cend, False).astype(BF16)
    else:
        out_t = _nt_dot(w2lo_ref[...], acts[0]) + _nt_dot(w2hi_ref[...], acts[1])
        o_ref[0] = out_t.astype(BF16)


def _compress(tokens, w1x, pos_flat, w1, w2lo, w2hi, gain2, is_key):
    bsz, seq, width = tokens.shape
    nblk = seq // CMP_STRIDE
    hid = w1.shape[1]

    def const_spec(shape):
        return pl.BlockSpec(shape, lambda b: (0,) * len(shape))

    out_tail = (nblk, 2 * LANES) if is_key else (LANES, nblk)
    return pl.pallas_call(
        functools.partial(_compress_kernel, is_key),
        out_shape=jax.ShapeDtypeStruct((bsz,) + out_tail, BF16),
        grid=(bsz,),
        in_specs=[pl.BlockSpec((1, seq, width), lambda b: (b, 0, 0)),
                  const_spec(w1x.shape), const_spec((8, w1.shape[0])), const_spec(w1.shape),
                  const_spec(w2lo.shape), const_spec(w2hi.shape), const_spec((1, LANES))],
        out_specs=pl.BlockSpec((1,) + out_tail, lambda b: (b, 0, 0)),
        scratch_shapes=[pltpu.VMEM((SUBLANES, hid), F32)],
        compiler_params=pltpu.CompilerParams(dimension_semantics=("arbitrary",), vmem_limit_bytes=VMEM_LIMIT),
        name="compress_k" if is_key else "compress_v",
    )(tokens, w1x, pos_flat, w1, w2lo, w2hi, gain2)


ONES_ROWS = BF16_SUBLANES


def _with_ones(vt):
    return jnp.concatenate([vt, jnp.ones((ONES_ROWS, vt.shape[1]), BF16)], axis=0)


def _online_update(carry, s, top, vt):
    m, acc = carry
    m_new = jnp.maximum(m, top)
    p = jnp.exp2(s - m_new)
    acc = jnp.exp2(m - m_new) * acc + jnp.dot(_with_ones(vt), p.astype(BF16), preferred_element_type=F32)
    return m_new, acc


def _normalised(acc):
    return acc[0:HEAD_DIM] * (1.0 / acc[HEAD_DIM:HEAD_DIM + 1])


def _attn_kernel(qt_ref, kc_ref, vct_ref, ks_ref, vst_ref, kw_ref, vwt_ref, gt_ref, ovt_ref,
                 o_ref, impt_ref, s_ref, sel_ref):
    tq = qt_ref.shape[2]
    n_cmp = kc_ref.shape[1]
    n_sel = impt_ref.shape[1]
    ncol = NSA_GROUP * tq
    per_wide = SLC_KEYS // KEY_CHUNK
    per_tile = tq // KEY_CHUNK
    qi = pl.program_id(1)
    t0 = qi * tq
    c0 = qi * per_tile
    tloc = lax.broadcasted_iota(jnp.int32, (1, tq), 1)
    tcol = jnp.concatenate([tloc] * NSA_GROUP, axis=1)
    tpos = t0 + tloc
    nrow = lax.broadcasted_iota(jnp.int32, (n_sel, 1), 0)
    cur = jnp.right_shift(tpos, SEL_SHIFT)
    forced = jnp.logical_or(jnp.logical_or(nrow == 0, nrow == cur), nrow == cur - 1)
    frow = lax.broadcasted_iota(jnp.int32, (HEAD_DIM, 1), 0)
    srow = lax.broadcasted_iota(jnp.int32, (SUBLANES, 1), 0)
    zeros_q = jnp.zeros((HEAD_DIM, tq), BF16)
    row_minus_col = lax.broadcasted_iota(jnp.int32, (tq, 1), 0) - tcol
    own_ok = row_minus_col <= 0

    cidx = lax.broadcasted_iota(jnp.int32, (n_cmp, 1), 0)
    cend = CMP_STRIDE * cidx + (CMP_BLOCK - 1)
    cmp_ok = jnp.logical_and(cend <= t0 + tcol, cidx < n_cmp - 1)
    col_has_cmp = (t0 + tcol) >= CMP_BLOCK - 1
    gates = gt_ref[0]
    kv_heads = range(NSA_KV_HEADS)

    def values(ref, hk, first_chunk, n):
        return jnp.concatenate([ref[0, first_chunk + i, hk * HEAD_DIM:(hk + 1) * HEAD_DIM, :]
                                for i in range(n)], axis=1)

    not_past = jnp.logical_and(frow >= FEAT_CHUNK - SEL_BLOCK, frow - (FEAT_CHUNK - SEL_BLOCK) >= c0)

    q_plain, q_plain_past, o_cmp = [], [], []
    for hk in kv_heads:
        q_rows, feat_rows, feat_rows_past = [], [], []
        for g in range(NSA_GROUP):
            hd = hk * NSA_GROUP + g
            qh = qt_ref[0, hd * HEAD_DIM:(hd + 1) * HEAD_DIM, :]
            q_rows.append([qh, zeros_q] if hk == 0 else [zeros_q, qh])
            c = _alibi_coef(hd)
            a = jnp.zeros((HEAD_DIM, 1), F32)
            for i, term in enumerate(_bf16_terms(c)):
                a = jnp.where(frow == FEAT_BLK - SEL_BLOCK + i, SEL_BLOCK * term, a)
                a = jnp.where(frow == FEAT_OFF - SEL_BLOCK + i, term, a)
            a = jnp.where(frow == FEAT_ONE - SEL_BLOCK, -c * t0.astype(F32), a)
            feat_rows.append(jnp.broadcast_to(a, (HEAD_DIM, tq)).astype(BF16))
            feat_rows_past.append(jnp.broadcast_to(jnp.where(not_past, NEG_INF, a), (HEAD_DIM, tq)).astype(BF16))

        def q_operand(pen, feats, q_rows=q_rows):
            return jnp.concatenate([jnp.concatenate(q_rows[g] + [pen, feats[g]], axis=0)
                                    for g in range(NSA_GROUP)], axis=1)

        q_plain.append(q_operand(zeros_q, feat_rows))
        q_plain_past.append(q_operand(zeros_q, feat_rows_past))
        sel_ref[0, hk] = q_plain[hk]
        sel_ref[1, hk] = q_plain_past[hk]

        s = jnp.dot(kc_ref[0], q_plain[hk], preferred_element_type=F32)
        s = jnp.where(cmp_ok, s, NEG_INF)
        e = jnp.exp2(s - jnp.max(s, axis=0, keepdims=True))
        p = e * jnp.where(col_has_cmp, 1.0 / jnp.sum(e, axis=0, keepdims=True), 0.0)
        o_cmp.append(jnp.dot(vct_ref[0, hk * HEAD_DIM:(hk + 1) * HEAD_DIM, :], p.astype(BF16),
                             preferred_element_type=F32))
        p_sum = p[:, 0:tq]
        for g in range(1, NSA_GROUP):
            p_sum = p_sum + p[:, g * tq:(g + 1) * tq]
        imp = jnp.zeros((n_sel, tq), F32)
        rest = p_sum
        for _ in range(N_SPLIT):
            term = rest.astype(BF16)
            imp = imp + jnp.dot(ovt_ref[...], term, preferred_element_type=F32)
            rest = rest - term.astype(F32)
        imp = jnp.where(forced, FORCE, jnp.where(nrow > cur, -FORCE, imp))

        impt_ref[hk] = imp
        groups = [imp[b * SUBLANES:(b + 1) * SUBLANES] for b in range(n_sel // SUBLANES)]
        ranks = [jnp.zeros((SUBLANES, tq), F32) for _ in groups]
        for m in range(n_sel):
            other = impt_ref[hk, m:m + 1, :]
            for b, grp in enumerate(groups):
                if b * SUBLANES > m:
                    ahead = other >= grp
                elif (b + 1) * SUBLANES <= m:
                    ahead = other > grp
                else:
                    ahead = jnp.logical_or(other > grp, jnp.logical_and(other == grp, srow > m - b * SUBLANES))
                ranks[b] = ranks[b] + jnp.where(ahead, 1.0, 0.0)
        rank = jnp.concatenate(ranks, axis=0)
        pen = jnp.where(rank < float(SEL_TOPK), 0.0, NEG_INF).astype(BF16)
        pen_rows = jnp.concatenate([pen] * NSA_GROUP, axis=1)
        sel_ref[0, hk, 2 * HEAD_DIM:3 * HEAD_DIM, :] = pen_rows
        sel_ref[1, hk, 2 * HEAD_DIM:3 * HEAD_DIM, :] = pen_rows

    init = (jnp.full((1, ncol), NEG_INF, F32), jnp.zeros((HEAD_DIM + ONES_ROWS, ncol), F32))

    n_back = WINDOW // KEY_CHUNK
    wc0 = jnp.maximum(c0 - n_back, 0)
    lead = t0 - wc0 * KEY_CHUNK
    k_back = kw_ref[0, pl.ds(pl.multiple_of(wc0 * KEY_CHUNK, KEY_CHUNK), WINDOW), :]
    k_tile = kw_ref[0, pl.ds(pl.multiple_of(t0, KEY_CHUNK), tq), :]
    recent = row_minus_col > lead - WINDOW
    o_win = []
    for hk in kv_heads:
        s_back = jnp.dot(k_back, q_plain_past[hk], preferred_element_type=F32)
        s_own = jnp.where(own_ok, jnp.dot(k_tile, q_plain[hk], preferred_element_type=F32), NEG_INF)
        s = jnp.concatenate([jnp.where(recent, s_back[0:tq], NEG_INF), s_back[tq:], s_own], axis=0)
        vt = jnp.concatenate([values(vwt_ref, hk, wc0, n_back), values(vwt_ref, hk, c0, per_tile)], axis=1)
        _, acc = _online_update(init, s, jnp.max(s, axis=0, keepdims=True), vt)
        o_win.append(_normalised(acc))

    k_own = ks_ref[0, pl.ds(pl.multiple_of(t0, KEY_CHUNK), tq), :]
    stats = []
    for hk in kv_heads:
        s = jnp.where(own_ok, jnp.dot(k_own, sel_ref[0, hk], preferred_element_type=F32), NEG_INF)
        stats.append(_online_update(init, s, jnp.max(s, axis=0, keepdims=True), values(vst_ref, hk, c0, per_tile)))
    stats = tuple(stats)

    def produce(j, slot):
        k = ks_ref[0, j * SLC_KEYS:(j + 1) * SLC_KEYS, :]
        tops = []
        for hk in kv_heads:
            s = jnp.dot(k, sel_ref[1, hk], preferred_element_type=F32)
            s_ref[slot, hk] = s
            tops.append(jnp.max(s, axis=0, keepdims=True))
        return tuple(tops)

    def consume(j, slot, tops, stats):
        return tuple(_online_update(stats[hk], s_ref[slot, hk], tops[hk],
                                    values(vst_ref, hk, j * per_wide, per_wide)) for hk in kv_heads)

    def chain(n_blocks):
        def run(stats):
            tops = produce(0, 0) if n_blocks else None
            for j in range(n_blocks):
                nxt = produce(j + 1, (j + 1) % 2) if j + 1 < n_blocks else None
                stats = consume(j, j % 2, tops, stats)
                tops = nxt
            return stats
        return run

    stats = lax.switch((t0 + SLC_KEYS - 1) // SLC_KEYS,
                       [chain(n) for n in range(ks_ref.shape[1] // SLC_KEYS + 1)], stats)
    o_slc = [_normalised(acc) for _, acc in stats]

    for hk in kv_heads:
        for g in range(NSA_GROUP):
            hd = hk * NSA_GROUP + g
            tot = jnp.zeros((HEAD_DIM, tq), F32)
            for br, o_br in enumerate((o_cmp[hk], o_slc[hk], o_win[hk])):
                r = br * NSA_HEADS + hd
                tot = tot + gates[r:r + 1, :] * o_br[:, g * tq:(g + 1) * tq]
            o_ref[0, hd * HEAD_DIM:(hd + 1) * HEAD_DIM, :] = tot.astype(BF16)


def _attn(qt, kc, vct, kslc, vst, kwin, vwt, gt, overlap_t):
    bsz, nsa_w, seq = qt.shape
    tq = ATTN_ROWS
    n_cmp = kc.shape[1]
    n_sel = seq // SEL_BLOCK
    n_gate = gt.shape[1]
    assert seq % (2 * SLC_KEYS) == 0 and n_sel <= HALF and seq >= WINDOW + tq
    assert seq // KEY_CHUNK <= LANES - FEAT_CHUNK and n_sel % SUBLANES == 0

    def per_batch(shape):
        return pl.BlockSpec((1,) + shape, lambda b, i: (b,) + (0,) * len(shape))

    def const_spec(shape):
        return pl.BlockSpec(shape, lambda b, i: (0,) * len(shape))

    return pl.pallas_call(
        _attn_kernel,
        out_shape=jax.ShapeDtypeStruct((bsz, nsa_w, seq), BF16),
        grid=(bsz, seq // tq),
        in_specs=[pl.BlockSpec((1, nsa_w, tq), lambda b, i: (b, 0, i)),
                  per_batch((n_cmp, 2 * LANES)), per_batch((LANES, n_cmp)),
                  per_batch((seq, 2 * LANES)), per_batch((seq // KEY_CHUNK, LANES, KEY_CHUNK)),
                  per_batch((seq, 2 * LANES)), per_batch((seq // KEY_CHUNK, LANES, KEY_CHUNK)),
                  pl.BlockSpec((1, n_gate, tq), lambda b, i: (b, 0, i)),
                  const_spec(overlap_t.shape)],
        out_specs=pl.BlockSpec((1, nsa_w, tq), lambda b, i: (b, 0, i)),
        scratch_shapes=[pltpu.VMEM((NSA_KV_HEADS, n_sel, tq), F32),
                        pltpu.VMEM((2, NSA_KV_HEADS, SLC_KEYS, NSA_GROUP * tq), F32),
                        pltpu.VMEM((2, NSA_KV_HEADS, 2 * LANES, NSA_GROUP * tq), BF16)],
        compiler_params=pltpu.CompilerParams(dimension_semantics=("parallel", "arbitrary"),
                                             vmem_limit_bytes=VMEM_LIMIT),
        name="nsa_attn",
    )(qt, kc, vct, kslc, vst, kwin, vwt, gt, overlap_t)


def _out_kernel(x_ref, ylru_ref, onsat_ref, znsat_ref, gn_ref, gate_ref, wl_ref, wn_ref, o_ref):
    o = onsat_ref[0].astype(F32)
    ms = jnp.mean(o * o, axis=0, keepdims=True)
    z = znsat_ref[0].astype(F32)
    y_t = (o * lax.rsqrt(ms + EPS) * gn_ref[...]) * (z * jax.nn.sigmoid(z))
    y_nsa = jnp.transpose(y_t).astype(BF16)
    out = (jnp.dot(ylru_ref[0], wl_ref[...], preferred_element_type=F32)
           + jnp.dot(y_nsa, wn_ref[...], preferred_element_type=F32))
    o_ref[0] = x_ref[0] + gate_ref[0, 2:3, :] * out


def _out(x, ylru, onsat, znsat, g_nsa_b, mod3, w_lru, w_nsa):
    bsz, seq, d = x.shape
    ts = OUT_ROWS
    nsa_w = onsat.shape[1]

    def tok_spec(n):
        return pl.BlockSpec((1, ts, n), lambda b, s: (b, s, 0))

    def feat_spec(n):
        return pl.BlockSpec((1, n, ts), lambda b, s: (b, 0, s))

    def const_spec(shape):
        return pl.BlockSpec(shape, lambda b, s: (0,) * len(shape))

    return pl.pallas_call(
        _out_kernel,
        out_shape=jax.ShapeDtypeStruct((bsz, seq, d), F32),
        grid=(bsz, seq // ts),
        in_specs=[tok_spec(d), tok_spec(ylru.shape[2]), feat_spec(nsa_w), feat_spec(nsa_w),
                  const_spec((nsa_w, ts)),
                  pl.BlockSpec((1, 3, d), lambda b, s: (b, 0, 0)),
                  const_spec(w_lru.shape), const_spec(w_nsa.shape)],
        out_specs=tok_spec(d),
        compiler_params=pltpu.CompilerParams(dimension_semantics=("parallel", "parallel"),
                                             vmem_limit_bytes=VMEM_LIMIT),
        name="out_proj",
    )(x, ylru, onsat, znsat, g_nsa_b, mod3, w_lru, w_nsa)


def _overlap_t(n_cmp_pad, n_sel):
    ratio = SEL_BLOCK // CMP_STRIDE
    ov = np.zeros((n_sel, n_cmp_pad), np.float32)
    for c in range(n_cmp_pad - 1):
        for n in (c // ratio, (c + 1) // ratio):
            if n < n_sel:
                ov[n, c] += 1.0
    return ov


def _block_diag_pairs(w):
    nb, bs, _ = w.shape
    z = jnp.zeros((bs, bs), w.dtype)
    return jnp.stack([jnp.block([[w[2 * s], z], [z, w[2 * s + 1]]]) for s in range(nb // 2)])


def _compress_weights(w1, w2, transposed):
    hid = w1.shape[1]
    w1r = w1.reshape(2, CMP_STRIDE, HEAD_DIM, hid)
    parts = []
    for hk in range(NSA_KV_HEADS):
        for half in range(2):
            slot = jnp.zeros((CMP_STRIDE, NSA_KV_HEADS, HEAD_DIM, hid), w1.dtype).at[:, hk].set(w1r[half])
            parts.append(slot.reshape(CMP_STRIDE * LANES, hid))
    w1x = jnp.concatenate(parts, axis=1).astype(BF16)
    zpad = jnp.zeros_like(w2)
    w2lo = jnp.concatenate([w2, zpad], axis=1).astype(BF16)
    w2hi = jnp.concatenate([zpad, w2], axis=1).astype(BF16)
    if transposed:
        w2lo, w2hi = w2lo.T, w2hi.T
    return w1x, w2lo, w2hi


def _pad_to(n, m):
    return -(-n // m) * m


def _layer(x, c, w_ada, b_ada, norm_g, w_in, conv_w, conv_b, w_rg_a, b_rg_a, w_rg_x, b_rg_x, lru_lambda,
           cmp_pos_k, cmp_w1_k, cmp_w2_k, cmp_pos_v, cmp_w1_v, cmp_w2_v, g_q, g_k_cmp, g_k_slc, g_k_win,
           g_out_lru, g_out_nsa, w_out):
    bsz, seq, d = x.shape
    lru_w = d // 2
    nsa_w = NSA_HEADS * HEAD_DIM
    kv_w = NSA_KV_HEADS * HEAD_DIM
    n_sel = seq // SEL_BLOCK

    splits = (lru_w, lru_w, nsa_w, kv_w, kv_w, kv_w, kv_w, kv_w, kv_w, N_BRANCH * NSA_HEADS, nsa_w)
    offs = np.concatenate([[0], np.cumsum(splits)])
    names = ("xlru", "zlru", "q", "kcmp", "vcmp", "kslc", "vslc", "kwin", "vwin", "gates", "znsa")
    src = {n: (int(offs[i]), int(offs[i + 1])) for i, n in enumerate(names)}

    def plan(group, align):
        pieces, where, at = [], {}, 0
        for n in group:
            piece = w_in[:, src[n][0]:src[n][1]]
            width = _pad_to(piece.shape[1], align)
            if width != piece.shape[1]:
                piece = jnp.pad(piece, ((0, 0), (0, width - piece.shape[1])))
            pieces.append(piece)
            where[n] = (at, width)
            at += width
        return jnp.concatenate(pieces, axis=1).astype(BF16), where

    w_big, cols = plan(("xlru", "zlru", "kcmp", "vcmp", "kslc", "kwin"), LANES)
    w_tt, rows_t = plan(("q", "vslc", "vwin", "gates", "znsa"), BF16_SUBLANES)
    w_t = w_tt.T

    def dup(g):
        return jnp.concatenate([g, g]).reshape(1, LANES)

    gq_b = jnp.broadcast_to((g_q * (ATTN_SCALE * LOG2E)).reshape(HEAD_DIM, 1), (HEAD_DIM, PROJ_ROWS))
    mod3 = _mod(c, w_ada, b_ada).reshape(bsz, 3, d)
    lru_params = (conv_w, conv_b.reshape(1, lru_w),
                  _block_diag_pairs(w_rg_a).astype(BF16), b_rg_a.reshape(1, lru_w),
                  _block_diag_pairs(w_rg_x).astype(BF16), b_rg_x.reshape(1, lru_w),
                  lru_lambda.reshape(1, lru_w), g_out_lru.reshape(1, lru_w))
    (ylru, kcmp, vcmp, kslc, kwin, qt, vst, vwt, gt, znsat) = _proj(
        x, mod3, norm_g.reshape(1, d), w_big, cols, w_t, rows_t, gq_b, dup(g_k_slc), dup(g_k_win), lru_params)

    n_chunk = seq // CMP_STRIDE
    w1x_k, w2lo_k, w2hi_k = _compress_weights(cmp_w1_k, cmp_w2_k, False)
    w1x_v, w2lo_v, w2hi_v = _compress_weights(cmp_w1_v, cmp_w2_v, True)

    def pos_rows(pos):
        return jnp.broadcast_to(pos.reshape(1, CMP_BLOCK * HEAD_DIM), (8, CMP_BLOCK * HEAD_DIM))

    kc = _compress(kcmp, w1x_k, pos_rows(cmp_pos_k), cmp_w1_k,
                   w2lo_k, w2hi_k, dup(g_k_cmp), True)
    vct = _compress(vcmp, w1x_v, pos_rows(cmp_pos_v), cmp_w1_v,
                    w2lo_v, w2hi_v, dup(g_k_cmp), False)

    onsat = _attn(qt, kc, vct, kslc, vst, kwin, vwt, gt, jnp.asarray(_overlap_t(n_chunk, n_sel), dtype=BF16))

    g_nsa_b = jnp.broadcast_to(g_out_nsa.reshape(nsa_w, 1), (nsa_w, OUT_ROWS))
    return _out(x, ylru, onsat, znsat, g_nsa_b, mod3, w_out[:lru_w].astype(BF16), w_out[lru_w:].astype(BF16))


def kernel(x, c, w_ada, b_ada, norm_g, w_in, conv_w, conv_b, w_rg_a, b_rg_a, w_rg_x, b_rg_x, lru_lambda, cmp_pos_k, cmp_w1_k, cmp_w2_k, cmp_pos_v, cmp_w1_v, cmp_w2_v, g_q, g_k_cmp, g_k_slc, g_k_win, g_out_lru, g_out_nsa, w_out):
    params = (w_ada, b_ada, norm_g, w_in, conv_w, conv_b, w_rg_a, b_rg_a, w_rg_x, b_rg_x, lru_lambda,
              cmp_pos_k, cmp_w1_k, cmp_w2_k, cmp_pos_v, cmp_w1_v, cmp_w2_v, g_q, g_k_cmp, g_k_slc, g_k_win,
              g_out_lru, g_out_nsa, w_out)
    for layer in range(w_in.shape[0]):
        x = _layer(x, c, *(p[layer] for p in params))
    return x
```

```python
import functools
import math

import ml_dtypes
import numpy as np
import jax
import jax.numpy as jnp
from jax import lax
from jax.experimental import pallas as pl
from jax.experimental.pallas import tpu as pltpu

F32 = jnp.float32
BF16 = jnp.bfloat16
HIGHEST = lax.Precision.HIGHEST

LANES = 128
HALF = LANES // 2
SUBLANES = 8
BF16_SUBLANES = 2 * SUBLANES

CONV_WIDTH = 4
LRU_C = 8.0
NSA_HEADS = 8
HEAD_DIM = 64
NSA_KV_HEADS = 2
NSA_GROUP = NSA_HEADS // NSA_KV_HEADS
N_BRANCH = 3
CMP_STRIDE = 16
CMP_BLOCK = 2 * CMP_STRIDE
SEL_BLOCK = 64
SEL_SHIFT = SEL_BLOCK.bit_length() - 1
SEL_TOPK = 16
WINDOW = 512
ATTN_SCALE = HEAD_DIM ** -0.5
LOG2E = math.log2(math.e)
NEG_INF = -1e30
FORCE = 1e6
EPS = 1e-6

PROJ_ROWS = 1024
OUT_ROWS = 1024
ATTN_ROWS = 128
SLC_KEYS = 512
KEY_CHUNK = 128
VMEM_LIMIT = 56 * 1024 * 1024

FEAT_BLK = SEL_BLOCK
FEAT_OFF = SEL_BLOCK + 3
FEAT_ONE = SEL_BLOCK + 6
MAX_KEY_CHUNKS = 32
FEAT_CHUNK = LANES - MAX_KEY_CHUNKS
N_SPLIT = 3

assert HEAD_DIM == HALF and NSA_KV_HEADS == 2 and KEY_CHUNK == LANES
assert ATTN_ROWS % KEY_CHUNK == 0 and ATTN_ROWS <= WINDOW


def _bf16_terms(value):
    terms, rest = [], np.float32(value)
    for _ in range(N_SPLIT):
        t = np.float32(rest).astype(ml_dtypes.bfloat16).astype(np.float32)
        terms.append(float(t))
        rest = np.float32(rest - t)
    return terms


def _alibi_coef(head):
    return float(2.0 ** (-8.0 * (head + 1) / NSA_HEADS)) * LOG2E


def _nt_dot(a, b):
    return lax.dot_general(a, b, (((1,), (1,)), ((), ())), preferred_element_type=F32)


def _half_rms(x, gain2):
    lane = lax.broadcasted_iota(jnp.int32, (1, LANES), 1)
    lo = lane < HALF
    sq = x * x
    ss_lo = jnp.sum(jnp.where(lo, sq, 0.0), axis=-1, keepdims=True)
    ss_hi = jnp.sum(jnp.where(lo, 0.0, sq), axis=-1, keepdims=True)
    r = jnp.where(lo, lax.rsqrt(ss_lo * (1.0 / HALF) + EPS), lax.rsqrt(ss_hi * (1.0 / HALF) + EPS))
    return x * r * gain2


def _key_features(pos, onehot):
    lane = lax.broadcasted_iota(jnp.int32, (1, LANES), 1)
    blk = jnp.right_shift(pos, SEL_SHIFT)
    off = jnp.bitwise_and(pos, SEL_BLOCK - 1)
    f = jnp.where(jnp.logical_and(lane >= FEAT_BLK, lane < FEAT_OFF), blk.astype(F32),
                  jnp.where(jnp.logical_and(lane >= FEAT_OFF, lane < FEAT_ONE), off.astype(F32),
                            jnp.where(lane == FEAT_ONE, 1.0, 0.0)))
    if onehot:
        f = jnp.where(lane < SEL_BLOCK, jnp.where(lane == blk, 1.0, 0.0), f)
        chunk = jnp.right_shift(pos, KEY_CHUNK.bit_length() - 1)
        f = jnp.where(lane >= FEAT_CHUNK, jnp.where(lane - FEAT_CHUNK == chunk, 1.0, 0.0), f)
    return f


def _mod_kernel(c_ref, w_ref, b_ref, o_ref):
    c = c_ref[...]
    o_ref[...] = jnp.dot(c * jax.nn.sigmoid(c), w_ref[...], preferred_element_type=F32, precision=HIGHEST) + b_ref[...]


def _mod(c, w_ada, b_ada):
    bsz, d = c.shape
    n = w_ada.shape[1]
    return pl.pallas_call(
        _mod_kernel,
        out_shape=jax.ShapeDtypeStruct((bsz, n), F32),
        grid=(n // d,),
        in_specs=[pl.BlockSpec((bsz, d), lambda j: (0, 0)),
                  pl.BlockSpec((d, d), lambda j: (0, j)),
                  pl.BlockSpec((1, d), lambda j: (0, j))],
        out_specs=pl.BlockSpec((bsz, d), lambda j: (0, j)),
        name="adaln_mod",
    )(c, w_ada, b_ada.reshape(1, n))


def _log1p(y):
    w = 1.0 + y
    return jnp.where(w == 1.0, y, jnp.log(w) * (y / (w - 1.0)))


def _softplus(x):
    return jnp.maximum(x, 0.0) + _log1p(jnp.exp(-jnp.abs(x)))


def _sigmoid(x):
    return 0.5 * jnp.tanh(0.5 * x) + 0.5


def _lru_gates(x, first, cw_ref, cb_ref, wa_ref, ba_ref, wx_ref, bx_ref, lam_ref, tail_ref, h_ref):
    rows, width = x.shape

    @pl.when(first)
    def _():
        tail_ref[...] = jnp.zeros_like(tail_ref)
        h_ref[...] = jnp.zeros_like(h_ref)

    tail = tail_ref[...]
    tail_ref[...] = x[rows - SUBLANES:rows]
    row8 = lax.broadcasted_iota(jnp.int32, (SUBLANES, 1), 0)
    xc = cb_ref[...] + cw_ref[CONV_WIDTH - 1:CONV_WIDTH, :] * x
    for k in range(1, CONV_WIDTH):
        xs = pltpu.roll(x, k, 0)
        head = jnp.where(row8 < k, pltpu.roll(tail, k, 0), xs[0:SUBLANES])
        xs = jnp.concatenate([head, xs[SUBLANES:]], axis=0)
        xc = xc + cw_ref[CONV_WIDTH - 1 - k:CONV_WIDTH - k, :] * xs

    xcb = xc.astype(BF16)
    nslot = width // LANES
    both = [jnp.dot(xcb[:, s * LANES:(s + 1) * LANES], jnp.concatenate([wa_ref[s], wx_ref[s]], axis=1),
                    preferred_element_type=F32) for s in range(nslot)]
    ra = jnp.concatenate([b[:, 0:LANES] for b in both], axis=1)
    ri = jnp.concatenate([b[:, LANES:2 * LANES] for b in both], axis=1)
    r = _sigmoid(ra + ba_ref[...])
    i = _sigmoid(ri + bx_ref[...])
    log_a = (-LRU_C) * r * _softplus(-lam_ref[...])
    a = jnp.exp(log_a)
    y = -jnp.tanh(log_a) * (a * a + 1.0)
    return a, jnp.where(y > 0.0, y * lax.rsqrt(y), 0.0) * (i * xc)


def _lru_scan_groups(a, u):
    rows = a.shape[0]
    sub = jnp.bitwise_and(lax.broadcasted_iota(jnp.int32, (rows, 1), 0), SUBLANES - 1)
    acc_a, acc_h = a, u
    d = 1
    while d < SUBLANES:
        keep = sub >= d
        sh_a = pltpu.roll(acc_a, d, 0)
        sh_h = pltpu.roll(acc_h, d, 0)
        acc_h = jnp.where(keep, acc_a * sh_h + acc_h, acc_h)
        acc_a = jnp.where(keep, acc_a * sh_a, acc_a)
        d *= 2
    return acc_a, acc_h


def _lru_carry(acc_a, acc_h, h_ref):
    rows = acc_a.shape[0]
    carry = h_ref[0:1, :]
    groups = []
    for g in range(rows // SUBLANES):
        hg = acc_h[g * SUBLANES:(g + 1) * SUBLANES] + acc_a[g * SUBLANES:(g + 1) * SUBLANES] * carry
        carry = hg[SUBLANES - 1:SUBLANES]
        groups.append(hg)
    h_ref[0:1, :] = carry
    return jnp.concatenate(groups, axis=0)


def _lru_out(h, z, g_ref):
    ms = jnp.mean(h * h, axis=-1, keepdims=True)
    return (h * lax.rsqrt(ms + EPS) * g_ref[...]) * (z * _sigmoid(z))


def _proj_kernel(cols, rows_t, x_ref, mod_ref, ng_ref, w_ref, wt_ref, gq_ref, gks_ref, gkw_ref,
                 cw_ref, cb_ref, wa_ref, ba_ref, wx_ref, bx_ref, lam_ref, gl_ref,
                 ylru_ref, kcmp_ref, vcmp_ref, kslc_ref, kwin_ref,
                 qt_ref, vst_ref, vwt_ref, gt_ref, znsat_ref, tail_ref, h_ref):
    rows = x_ref.shape[1]
    x = x_ref[0]
    ms = jnp.mean(x * x, axis=-1, keepdims=True)
    gain = ng_ref[...] * (1.0 + mod_ref[0, 1:2, :])
    h = (x * lax.rsqrt(ms + EPS)) * gain + mod_ref[0, 0:1, :]
    hb = h.astype(BF16)

    def mm(name):
        c0, n = cols[name]
        return jnp.dot(hb, w_ref[:, c0:c0 + n], preferred_element_type=F32)

    def mm_pair(first, second):
        (c0, n0), (c1, n1) = cols[first], cols[second]
        assert c1 == c0 + n0
        both = jnp.dot(hb, w_ref[:, c0:c1 + n1], preferred_element_type=F32)
        return both[:, 0:n0], both[:, n0:n0 + n1]

    def mmt(*names):
        spans = [rows_t[name] for name in names]
        r0 = spans[0][0]
        assert all(a[0] + a[1] == b[0] for a, b in zip(spans, spans[1:]))
        out = _nt_dot(wt_ref[r0:spans[-1][0] + spans[-1][1], :], hb)
        parts = tuple(out[r - r0:r - r0 + n] for r, n in spans)
        return parts[0] if len(parts) == 1 else parts

    def after(value, done):
        return value + jnp.minimum(jnp.abs(done[done.shape[0] - 1:, 0:1]), 0.0)

    lru_a, lru_u = _lru_gates(mm("xlru"), pl.program_id(1) == 0, cw_ref, cb_ref, wa_ref, ba_ref,
                              wx_ref, bx_ref, lam_ref, tail_ref, h_ref)
    z_lru = mm("zlru")
    kcmp_ref[0], vcmp_ref[0] = mm_pair("kcmp", "vcmp")

    pos = pl.program_id(1) * rows + lax.broadcasted_iota(jnp.int32, (rows, 1), 0)
    k_slc, k_win = mm_pair("kslc", "kwin")
    kslc_ref[0, :, 0:LANES] = _half_rms(k_slc, gks_ref[...]).astype(BF16)
    kslc_ref[0, :, LANES:2 * LANES] = _key_features(pos, True).astype(BF16)
    kwin_ref[0, :, 0:LANES] = _half_rms(k_win, gkw_ref[...]).astype(BF16)
    kwin_ref[0, :, LANES:2 * LANES] = _key_features(pos, True).astype(BF16)

    lru_a, lru_h = _lru_scan_groups(lru_a, after(lru_u, k_win))
    qt = mmt("q")
    for hd in range(NSA_HEADS):
        qh = qt[hd * HEAD_DIM:(hd + 1) * HEAD_DIM]
        r = lax.rsqrt(jnp.mean(qh * qh, axis=0, keepdims=True) + EPS)
        qt_ref[0, hd * HEAD_DIM:(hd + 1) * HEAD_DIM, :] = (qh * r * gq_ref[...]).astype(BF16)

    h_lru = _lru_carry(lru_a, after(lru_h, qt), h_ref)
    vst, vwt, gate_logits, z_nsa = mmt("vslc", "vwin", "gates", "znsa")
    vst, vwt = vst.astype(BF16), vwt.astype(BF16)
    for j in range(rows // KEY_CHUNK):
        vst_ref[0, j] = vst[:, j * KEY_CHUNK:(j + 1) * KEY_CHUNK]
        vwt_ref[0, j] = vwt[:, j * KEY_CHUNK:(j + 1) * KEY_CHUNK]
    gt_ref[0] = jax.nn.sigmoid(gate_logits)
    znsat_ref[0] = z_nsa.astype(BF16)
    ylru_ref[0] = _lru_out(after(h_lru, z_nsa), z_lru, gl_ref).astype(BF16)


def _proj(x, mod3, norm_g, w_big, cols, w_t, rows_t, gq_b, gks2, gkw2, lru_params):
    bsz, seq, d = x.shape
    ts = PROJ_ROWS
    nsa_w = NSA_HEADS * HEAD_DIM
    lru_w = cols["xlru"][1]
    n_gate = rows_t["gates"][1]

    def tok_spec(n):
        return pl.BlockSpec((1, ts, n), lambda b, s: (b, s, 0))

    def feat_spec(n):
        return pl.BlockSpec((1, n, ts), lambda b, s: (b, 0, s))

    def chunk_spec():
        return pl.BlockSpec((1, ts // KEY_CHUNK, LANES, KEY_CHUNK), lambda b, s: (b, s, 0, 0))

    def const_spec(shape):
        return pl.BlockSpec(shape, lambda b, s: (0,) * len(shape))

    out_shape = (
        jax.ShapeDtypeStruct((bsz, seq, lru_w), BF16),
        jax.ShapeDtypeStruct((bsz, seq, LANES), F32),
        jax.ShapeDtypeStruct((bsz, seq, LANES), F32),
        jax.ShapeDtypeStruct((bsz, seq, 2 * LANES), BF16),
        jax.ShapeDtypeStruct((bsz, seq, 2 * LANES), BF16),
        jax.ShapeDtypeStruct((bsz, nsa_w, seq), BF16),
        jax.ShapeDtypeStruct((bsz, seq // KEY_CHUNK, LANES, KEY_CHUNK), BF16),
        jax.ShapeDtypeStruct((bsz, seq // KEY_CHUNK, LANES, KEY_CHUNK), BF16),
        jax.ShapeDtypeStruct((bsz, n_gate, seq), F32),
        jax.ShapeDtypeStruct((bsz, nsa_w, seq), BF16),
    )
    out_specs = (tok_spec(lru_w), tok_spec(LANES), tok_spec(LANES),
                 tok_spec(2 * LANES), tok_spec(2 * LANES),
                 feat_spec(nsa_w), chunk_spec(), chunk_spec(), feat_spec(n_gate), feat_spec(nsa_w))
    return pl.pallas_call(
        functools.partial(_proj_kernel, cols, rows_t),
        out_shape=out_shape,
        grid=(bsz, seq // ts),
        in_specs=[tok_spec(d),
                  pl.BlockSpec((1, 3, d), lambda b, s: (b, 0, 0)),
                  const_spec((1, d)),
                  const_spec(w_big.shape), const_spec(w_t.shape),
                  const_spec((HEAD_DIM, ts)), const_spec((1, LANES)), const_spec((1, LANES))]
                 + [const_spec(p.shape) for p in lru_params],
        out_specs=out_specs,
        scratch_shapes=[pltpu.VMEM((SUBLANES, lru_w), F32), pltpu.VMEM((SUBLANES, lru_w), F32)],
        compiler_params=pltpu.CompilerParams(dimension_semantics=("parallel", "arbitrary"),
                                             vmem_limit_bytes=VMEM_LIMIT),
        name="in_proj_lru",
    )(x, mod3, norm_g, w_big, w_t, gq_b, gks2, gkw2, *lru_params)


def _compress_kernel(is_key, c_ref, w1x_ref, pos_ref, w1_ref, w2lo_ref, w2hi_ref, g_ref, o_ref, const_ref):
    nblk = c_ref.shape[1] // CMP_STRIDE
    hid = w1_ref.shape[1]

    @pl.when(pl.program_id(0) == 0)
    def _():
        const_ref[...] = jnp.dot(pos_ref[...], w1_ref[...], preferred_element_type=F32, precision=HIGHEST)

    ab = jnp.zeros((nblk, 4 * hid), F32)
    for j in range(0, CMP_STRIDE, 2):
        pair = jnp.concatenate([c_ref[0, pl.ds(j + i, nblk, stride=CMP_STRIDE), :].astype(BF16) for i in range(2)],
                               axis=1)
        ab = ab + jnp.dot(pair, w1x_ref[j * LANES:(j + 2) * LANES, :], preferred_element_type=F32)
    const = const_ref[0:1, :]
    acts = []
    for hk in range(NSA_KV_HEADS):
        first = ab[:, (2 * hk) * hid:(2 * hk + 1) * hid]
        second = ab[:, (2 * hk + 1) * hid:(2 * hk + 2) * hid]
        hidden = first + pltpu.roll(second, nblk - 1, 0) + const
        acts.append(jax.nn.gelu(hidden).astype(BF16))
    if is_key:
        out = (jnp.dot(acts[0], w2lo_ref[...], preferred_element_type=F32)
               + jnp.dot(acts[1], w2hi_ref[...], preferred_element_type=F32))
        o_ref[0, :, 0:LANES] = _half_rms(out, g_ref[...]).astype(BF16)
        cend = CMP_STRIDE * lax.broadcasted_iota(jnp.int32, (nblk, 1), 0) + (CMP_BLOCK - 1)
        o_ref[0, :, LANES:2 * LANES] = _key_features(cend, False).astype(BF16)
    else:
        out_t = _nt_dot(w2lo_ref[...], acts[0]) + _nt_dot(w2hi_ref[...], acts[1])
        o_ref[0] = out_t.astype(BF16)


def _compress(tokens, w1x, pos_flat, w1, w2lo, w2hi, gain2, is_key):
    bsz, seq, width = tokens.shape
    nblk = seq // CMP_STRIDE
    hid = w1.shape[1]

    def const_spec(shape):
        return pl.BlockSpec(shape, lambda b: (0,) * len(shape))

    out_tail = (nblk, 2 * LANES) if is_key else (LANES, nblk)
    return pl.pallas_call(
        functools.partial(_compress_kernel, is_key),
        out_shape=jax.ShapeDtypeStruct((bsz,) + out_tail, BF16),
        grid=(bsz,),
        in_specs=[pl.BlockSpec((1, seq, width), lambda b: (b, 0, 0)),
                  const_spec(w1x.shape), const_spec((8, w1.shape[0])), const_spec(w1.shape),
                  const_spec(w2lo.shape), const_spec(w2hi.shape), const_spec((1, LANES))],
        out_specs=pl.BlockSpec((1,) + out_tail, lambda b: (b, 0, 0)),
        scratch_shapes=[pltpu.VMEM((SUBLANES, hid), F32)],
        compiler_params=pltpu.CompilerParams(dimension_semantics=("arbitrary",), vmem_limit_bytes=VMEM_LIMIT),
        name="compress_k" if is_key else "compress_v",
    )(tokens, w1x, pos_flat, w1, w2lo, w2hi, gain2)


ONES_ROWS = BF16_SUBLANES


def _with_ones(vt):
    return jnp.concatenate([vt, jnp.ones((ONES_ROWS, vt.shape[1]), BF16)], axis=0)


def _online_update(carry, s, top, vt):
    m, acc = carry
    m_new = jnp.maximum(m, top)
    p = jnp.exp2(s - m_new)
    acc = jnp.exp2(m - m_new) * acc + jnp.dot(_with_ones(vt), p.astype(BF16), preferred_element_type=F32)
    return m_new, acc


def _normalised(acc):
    return acc[0:HEAD_DIM] * (1.0 / acc[HEAD_DIM:HEAD_DIM + 1])


def _attn_kernel(qt_ref, kc_ref, vct_ref, ks_ref, vst_ref, kw_ref, vwt_ref, gt_ref, ovt_ref,
                 o_ref, impt_ref, s_ref, sel_ref):
    tq = qt_ref.shape[2]
    n_cmp = kc_ref.shape[1]
    n_sel = impt_ref.shape[1]
    ncol = NSA_GROUP * tq
    per_wide = SLC_KEYS // KEY_CHUNK
    per_tile = tq // KEY_CHUNK
    qi = pl.program_id(1)
    t0 = qi * tq
    c0 = qi * per_tile
    tloc = lax.broadcasted_iota(jnp.int32, (1, tq), 1)
    tcol = jnp.concatenate([tloc] * NSA_GROUP, axis=1)
    tpos = t0 + tloc
    nrow = lax.broadcasted_iota(jnp.int32, (n_sel, 1), 0)
    cur = jnp.right_shift(tpos, SEL_SHIFT)
    forced = jnp.logical_or(jnp.logical_or(nrow == 0, nrow == cur), nrow == cur - 1)
    frow = lax.broadcasted_iota(jnp.int32, (HEAD_DIM, 1), 0)
    srow = lax.broadcasted_iota(jnp.int32, (SUBLANES, 1), 0)
    zeros_q = jnp.zeros((HEAD_DIM, tq), BF16)
    row_minus_col = lax.broadcasted_iota(jnp.int32, (tq, 1), 0) - tcol
    own_ok = row_minus_col <= 0

    cidx = lax.broadcasted_iota(jnp.int32, (n_cmp, 1), 0)
    cend = CMP_STRIDE * cidx + (CMP_BLOCK - 1)
    cmp_ok = jnp.logical_and(cend <= t0 + tcol, cidx < n_cmp - 1)
    col_has_cmp = (t0 + tcol) >= CMP_BLOCK - 1
    gates = gt_ref[0]
    kv_heads = range(NSA_KV_HEADS)

    def values(ref, hk, first_chunk, n):
        return jnp.concatenate([ref[0, first_chunk + i, hk * HEAD_DIM:(hk + 1) * HEAD_DIM, :]
                                for i in range(n)], axis=1)

    not_past = jnp.logical_and(frow >= FEAT_CHUNK - SEL_BLOCK, frow - (FEAT_CHUNK - SEL_BLOCK) >= c0)

    q_plain, q_plain_past, o_cmp = [], [], []
    for hk in kv_heads:
        q_rows, feat_rows, feat_rows_past = [], [], []
        for g in range(NSA_GROUP):
            hd = hk * NSA_GROUP + g
            qh = qt_ref[0, hd * HEAD_DIM:(hd + 1) * HEAD_DIM, :]
            q_rows.append([qh, zeros_q] if hk == 0 else [zeros_q, qh])
            c = _alibi_coef(hd)
            a = jnp.zeros((HEAD_DIM, 1), F32)
            for i, term in enumerate(_bf16_terms(c)):
                a = jnp.where(frow == FEAT_BLK - SEL_BLOCK + i, SEL_BLOCK * term, a)
                a = jnp.where(frow == FEAT_OFF - SEL_BLOCK + i, term, a)
            a = jnp.where(frow == FEAT_ONE - SEL_BLOCK, -c * t0.astype(F32), a)
            feat_rows.append(jnp.broadcast_to(a, (HEAD_DIM, tq)).astype(BF16))
            feat_rows_past.append(jnp.broadcast_to(jnp.where(not_past, NEG_INF, a), (HEAD_DIM, tq)).astype(BF16))

        def q_operand(pen, feats, q_rows=q_rows):
            return jnp.concatenate([jnp.concatenate(q_rows[g] + [pen, feats[g]], axis=0)
                                    for g in range(NSA_GROUP)], axis=1)

        q_plain.append(q_operand(zeros_q, feat_rows))
        q_plain_past.append(q_operand(zeros_q, feat_rows_past))
        sel_ref[0, hk] = q_plain[hk]
        sel_ref[1, hk] = q_plain_past[hk]

        s = jnp.dot(kc_ref[0], q_plain[hk], preferred_element_type=F32)
        s = jnp.where(cmp_ok, s, NEG_INF)
        e = jnp.exp2(s - jnp.max(s, axis=0, keepdims=True))
        p = e * jnp.where(col_has_cmp, 1.0 / jnp.sum(e, axis=0, keepdims=True), 0.0)
        o_cmp.append(jnp.dot(vct_ref[0, hk * HEAD_DIM:(hk + 1) * HEAD_DIM, :], p.astype(BF16),
                             preferred_element_type=F32))
        p_sum = p[:, 0:tq]
        for g in range(1, NSA_GROUP):
            p_sum = p_sum + p[:, g * tq:(g + 1) * tq]
        imp = jnp.zeros((n_sel, tq), F32)
        rest = p_sum
        for _ in range(N_SPLIT):
            term = rest.astype(BF16)
            imp = imp + jnp.dot(ovt_ref[...], term, preferred_element_type=F32)
            rest = rest - term.astype(F32)
        imp = jnp.where(forced, FORCE, jnp.where(nrow > cur, -FORCE, imp))

        impt_ref[hk] = imp
        groups = [imp[b * SUBLANES:(b + 1) * SUBLANES] for b in range(n_sel // SUBLANES)]
        ranks = [jnp.zeros((SUBLANES, tq), F32) for _ in groups]
        for m in range(n_sel):
            other = impt_ref[hk, m:m + 1, :]
            for b, grp in enumerate(groups):
                if b * SUBLANES > m:
                    ahead = other >= grp
                elif (b + 1) * SUBLANES <= m:
                    ahead = other > grp
                else:
                    ahead = jnp.logical_or(other > grp, jnp.logical_and(other == grp, srow > m - b * SUBLANES))
                ranks[b] = ranks[b] + jnp.where(ahead, 1.0, 0.0)
        rank = jnp.concatenate(ranks, axis=0)
        pen = jnp.where(rank < float(SEL_TOPK), 0.0, NEG_INF).astype(BF16)
        pen_rows = jnp.concatenate([pen] * NSA_GROUP, axis=1)
        sel_ref[0, hk, 2 * HEAD_DIM:3 * HEAD_DIM, :] = pen_rows
        sel_ref[1, hk, 2 * HEAD_DIM:3 * HEAD_DIM, :] = pen_rows

    init = (jnp.full((1, ncol), NEG_INF, F32), jnp.zeros((HEAD_DIM + ONES_ROWS, ncol), F32))

    n_back = WINDOW // KEY_CHUNK
    wc0 = jnp.maximum(c0 - n_back, 0)
    lead = t0 - wc0 * KEY_CHUNK
    k_back = kw_ref[0, pl.ds(pl.multiple_of(wc0 * KEY_CHUNK, KEY_CHUNK), WINDOW), :]
    k_tile = kw_ref[0, pl.ds(pl.multiple_of(t0, KEY_CHUNK), tq), :]
    recent = row_minus_col > lead - WINDOW
    o_win = []
    for hk in kv_heads:
        s_back = jnp.dot(k_back, q_plain_past[hk], preferred_element_type=F32)
        s_own = jnp.where(own_ok, jnp.dot(k_tile, q_plain[hk], preferred_element_type=F32), NEG_INF)
        s = jnp.concatenate([jnp.where(recent, s_back[0:tq], NEG_INF), s_back[tq:], s_own], axis=0)
        vt = jnp.concatenate([values(vwt_ref, hk, wc0, n_back), values(vwt_ref, hk, c0, per_tile)], axis=1)
        _, acc = _online_update(init, s, jnp.max(s, axis=0, keepdims=True), vt)
        o_win.append(_normalised(acc))

    k_own = ks_ref[0, pl.ds(pl.multiple_of(t0, KEY_CHUNK), tq), :]
    stats = []
    for hk in kv_heads:
        s = jnp.where(own_ok, jnp.dot(k_own, sel_ref[0, hk], preferred_element_type=F32), NEG_INF)
        stats.append(_online_update(init, s, jnp.max(s, axis=0, keepdims=True), values(vst_ref, hk, c0, per_tile)))
    stats = tuple(stats)

    def produce(j, slot):
        k = ks_ref[0, j * SLC_KEYS:(j + 1) * SLC_KEYS, :]
        tops = []
        for hk in kv_heads:
            s = jnp.dot(k, sel_ref[1, hk], preferred_element_type=F32)
            s_ref[slot, hk] = s
            tops.append(jnp.max(s, axis=0, keepdims=True))
        return tuple(tops)

    def consume(j, slot, tops, stats):
        return tuple(_online_update(stats[hk], s_ref[slot, hk], tops[hk],
                                    values(vst_ref, hk, j * per_wide, per_wide)) for hk in kv_heads)

    def chain(n_blocks):
        def run(stats):
            tops = produce(0, 0) if n_blocks else None
            for j in range(n_blocks):
                nxt = produce(j + 1, (j + 1) % 2) if j + 1 < n_blocks else None
                stats = consume(j, j % 2, tops, stats)
                tops = nxt
            return stats
        return run

    stats = lax.switch((t0 + SLC_KEYS - 1) // SLC_KEYS,
                       [chain(n) for n in range(ks_ref.shape[1] // SLC_KEYS + 1)], stats)
    o_slc = [_normalised(acc) for _, acc in stats]

    for hk in kv_heads:
        for g in range(NSA_GROUP):
            hd = hk * NSA_GROUP + g
            tot = jnp.zeros((HEAD_DIM, tq), F32)
            for br, o_br in enumerate((o_cmp[hk], o_slc[hk], o_win[hk])):
                r = br * NSA_HEADS + hd
                tot = tot + gates[r:r + 1, :] * o_br[:, g * tq:(g + 1) * tq]
            o_ref[0, hd * HEAD_DIM:(hd + 1) * HEAD_DIM, :] = tot.astype(BF16)


def _attn(qt, kc, vct, kslc, vst, kwin, vwt, gt, overlap_t):
    bsz, nsa_w, seq = qt.shape
    tq = ATTN_ROWS
    n_cmp = kc.shape[1]
    n_sel = seq // SEL_BLOCK
    n_gate = gt.shape[1]
    assert seq % (2 * SLC_KEYS) == 0 and n_sel <= HALF and seq >= WINDOW + tq
    assert seq // KEY_CHUNK <= LANES - FEAT_CHUNK and n_sel % SUBLANES == 0

    def per_batch(shape):
        return pl.BlockSpec((1,) + shape, lambda b, i: (b,) + (0,) * len(shape))

    def const_spec(shape):
        return pl.BlockSpec(shape, lambda b, i: (0,) * len(shape))

    return pl.pallas_call(
        _attn_kernel,
        out_shape=jax.ShapeDtypeStruct((bsz, nsa_w, seq), BF16),
        grid=(bsz, seq // tq),
        in_specs=[pl.BlockSpec((1, nsa_w, tq), lambda b, i: (b, 0, i)),
                  per_batch((n_cmp, 2 * LANES)), per_batch((LANES, n_cmp)),
                  per_batch((seq, 2 * LANES)), per_batch((seq // KEY_CHUNK, LANES, KEY_CHUNK)),
                  per_batch((seq, 2 * LANES)), per_batch((seq // KEY_CHUNK, LANES, KEY_CHUNK)),
                  pl.BlockSpec((1, n_gate, tq), lambda b, i: (b, 0, i)),
                  const_spec(overlap_t.shape)],
        out_specs=pl.BlockSpec((1, nsa_w, tq), lambda b, i: (b, 0, i)),
        scratch_shapes=[pltpu.VMEM((NSA_KV_HEADS, n_sel, tq), F32),
                        pltpu.VMEM((2, NSA_KV_HEADS, SLC_KEYS, NSA_GROUP * tq), F32),
                        pltpu.VMEM((2, NSA_KV_HEADS, 2 * LANES, NSA_GROUP * tq), BF16)],
        compiler_params=pltpu.CompilerParams(dimension_semantics=("parallel", "arbitrary"),
                                             vmem_limit_bytes=VMEM_LIMIT),
        name="nsa_attn",
    )(qt, kc, vct, kslc, vst, kwin, vwt, gt, overlap_t)


def _out_kernel(x_ref, ylru_ref, onsat_ref, znsat_ref, gn_ref, gate_ref, wl_ref, wn_ref, o_ref):
    o = onsat_ref[0].astype(F32)
    ms = jnp.mean(o * o, axis=0, keepdims=True)
    z = znsat_ref[0].astype(F32)
    y_t = (o * lax.rsqrt(ms + EPS) * gn_ref[...]) * (z * jax.nn.sigmoid(z))
    y_nsa = jnp.transpose(y_t).astype(BF16)
    out = (jnp.dot(ylru_ref[0], wl_ref[...], preferred_element_type=F32)
           + jnp.dot(y_nsa, wn_ref[...], preferred_element_type=F32))
    o_ref[0] = x_ref[0] + gate_ref[0, 2:3, :] * out


def _out(x, ylru, onsat, znsat, g_nsa_b, mod3, w_lru, w_nsa):
    bsz, seq, d = x.shape
    ts = OUT_ROWS
    nsa_w = onsat.shape[1]

    def tok_spec(n):
        return pl.BlockSpec((1, ts, n), lambda b, s: (b, s, 0))

    def feat_spec(n):
        return pl.BlockSpec((1, n, ts), lambda b, s: (b, 0, s))

    def const_spec(shape):
        return pl.BlockSpec(shape, lambda b, s: (0,) * len(shape))

    return pl.pallas_call(
        _out_kernel,
        out_shape=jax.ShapeDtypeStruct((bsz, seq, d), F32),
        grid=(bsz, seq // ts),
        in_specs=[tok_spec(d), tok_spec(ylru.shape[2]), feat_spec(nsa_w), feat_spec(nsa_w),
                  const_spec((nsa_w, ts)),
                  pl.BlockSpec((1, 3, d), lambda b, s: (b, 0, 0)),
                  const_spec(w_lru.shape), const_spec(w_nsa.shape)],
        out_specs=tok_spec(d),
        compiler_params=pltpu.CompilerParams(dimension_semantics=("parallel", "parallel"),
                                             vmem_limit_bytes=VMEM_LIMIT),
        name="out_proj",
    )(x, ylru, onsat, znsat, g_nsa_b, mod3, w_lru, w_nsa)


def _overlap_t(n_cmp_pad, n_sel):
    ratio = SEL_BLOCK // CMP_STRIDE
    ov = np.zeros((n_sel, n_cmp_pad), np.float32)
    for c in range(n_cmp_pad - 1):
        for n in (c // ratio, (c + 1) // ratio):
            if n < n_sel:
                ov[n, c] += 1.0
    return ov


def _block_diag_pairs(w):
    nb, bs, _ = w.shape
    z = jnp.zeros((bs, bs), w.dtype)
    return jnp.stack([jnp.block([[w[2 * s], z], [z, w[2 * s + 1]]]) for s in range(nb // 2)])


def _compress_weights(w1, w2, transposed):
    hid = w1.shape[1]
    w1r = w1.reshape(2, CMP_STRIDE, HEAD_DIM, hid)
    parts = []
    for hk in range(NSA_KV_HEADS):
        for half in range(2):
            slot = jnp.zeros((CMP_STRIDE, NSA_KV_HEADS, HEAD_DIM, hid), w1.dtype).at[:, hk].set(w1r[half])
            parts.append(slot.reshape(CMP_STRIDE * LANES, hid))
    w1x = jnp.concatenate(parts, axis=1).astype(BF16)
    zpad = jnp.zeros_like(w2)
    w2lo = jnp.concatenate([w2, zpad], axis=1).astype(BF16)
    w2hi = jnp.concatenate([zpad, w2], axis=1).astype(BF16)
    if transposed:
        w2lo, w2hi = w2lo.T, w2hi.T
    return w1x, w2lo, w2hi


def _pad_to(n, m):
    return -(-n // m) * m


def _layer(x, c, w_ada, b_ada, norm_g, w_in, conv_w, conv_b, w_rg_a, b_rg_a, w_rg_x, b_rg_x, lru_lambda,
           cmp_pos_k, cmp_w1_k, cmp_w2_k, cmp_pos_v, cmp_w1_v, cmp_w2_v, g_q, g_k_cmp, g_k_slc, g_k_win,
           g_out_lru, g_out_nsa, w_out):
    bsz, seq, d = x.shape
    lru_w = d // 2
    nsa_w = NSA_HEADS * HEAD_DIM
    kv_w = NSA_KV_HEADS * HEAD_DIM
    n_sel = seq // SEL_BLOCK

    splits = (lru_w, lru_w, nsa_w, kv_w, kv_w, kv_w, kv_w, kv_w, kv_w, N_BRANCH * NSA_HEADS, nsa_w)
    offs = np.concatenate([[0], np.cumsum(splits)])
    names = ("xlru", "zlru", "q", "kcmp", "vcmp", "kslc", "vslc", "kwin", "vwin", "gates", "znsa")
    src = {n: (int(offs[i]), int(offs[i + 1])) for i, n in enumerate(names)}

    def plan(group, align):
        pieces, where, at = [], {}, 0
        for n in group:
            piece = w_in[:, src[n][0]:src[n][1]]
            width = _pad_to(piece.shape[1], align)
            if width != piece.shape[1]:
                piece = jnp.pad(piece, ((0, 0), (0, width - piece.shape[1])))
            pieces.append(piece)
            where[n] = (at, width)
            at += width
        return jnp.concatenate(pieces, axis=1).astype(BF16), where

    w_big, cols = plan(("xlru", "zlru", "kcmp", "vcmp", "kslc", "kwin"), LANES)
    w_tt, rows_t = plan(("q", "vslc", "vwin", "gates", "znsa"), BF16_SUBLANES)
    w_t = w_tt.T

    def dup(g):
        return jnp.concatenate([g, g]).reshape(1, LANES)

    gq_b = jnp.broadcast_to((g_q * (ATTN_SCALE * LOG2E)).reshape(HEAD_DIM, 1), (HEAD_DIM, PROJ_ROWS))
    mod3 = _mod(c, w_ada, b_ada).reshape(bsz, 3, d)
    lru_params = (conv_w, conv_b.reshape(1, lru_w),
                  _block_diag_pairs(w_rg_a).astype(BF16), b_rg_a.reshape(1, lru_w),
                  _block_diag_pairs(w_rg_x).astype(BF16), b_rg_x.reshape(1, lru_w),
                  lru_lambda.reshape(1, lru_w), g_out_lru.reshape(1, lru_w))
    (ylru, kcmp, vcmp, kslc, kwin, qt, vst, vwt, gt, znsat) = _proj(
        x, mod3, norm_g.reshape(1, d), w_big, cols, w_t, rows_t, gq_b, dup(g_k_slc), dup(g_k_win), lru_params)

    n_chunk = seq // CMP_STRIDE
    w1x_k, w2lo_k, w2hi_k = _compress_weights(cmp_w1_k, cmp_w2_k, False)
    w1x_v, w2lo_v, w2hi_v = _compress_weights(cmp_w1_v, cmp_w2_v, True)

    def pos_rows(pos):
        return jnp.broadcast_to(pos.reshape(1, CMP_BLOCK * HEAD_DIM), (8, CMP_BLOCK * HEAD_DIM))

    kc = _compress(kcmp, w1x_k, pos_rows(cmp_pos_k), cmp_w1_k,
                   w2lo_k, w2hi_k, dup(g_k_cmp), True)
    vct = _compress(vcmp, w1x_v, pos_rows(cmp_pos_v), cmp_w1_v,
                    w2lo_v, w2hi_v, dup(g_k_cmp), False)

    onsat = _attn(qt, kc, vct, kslc, vst, kwin, vwt, gt, jnp.asarray(_overlap_t(n_chunk, n_sel), dtype=BF16))

    g_nsa_b = jnp.broadcast_to(g_out_nsa.reshape(nsa_w, 1), (nsa_w, OUT_ROWS))
    return _out(x, ylru, onsat, znsat, g_nsa_b, mod3, w_out[:lru_w].astype(BF16), w_out[lru_w:].astype(BF16))


def kernel(x, c, w_ada, b_ada, norm_g, w_in, conv_w, conv_b, w_rg_a, b_rg_a, w_rg_x, b_rg_x, lru_lambda, cmp_pos_k, cmp_w1_k, cmp_w2_k, cmp_pos_v, cmp_w1_v, cmp_w2_v, g_q, g_k_cmp, g_k_slc, g_k_win, g_out_lru, g_out_nsa, w_out):
    params = (w_ada, b_ada, norm_g, w_in, conv_w, conv_b, w_rg_a, b_rg_a, w_rg_x, b_rg_x, lru_lambda,
              cmp_pos_k, cmp_w1_k, cmp_w2_k, cmp_pos_v, cmp_w1_v, cmp_w2_v, g_q, g_k_cmp, g_k_slc, g_k_win,
              g_out_lru, g_out_nsa, w_out)
    for layer in range(w_in.shape[0]):
        x = _layer(x, c, *(p[layer] for p in params))
    return x
```

```python
import functools
import math

import ml_dtypes
import numpy as np
import jax
import jax.numpy as jnp
from jax import lax
from jax.experimental import pallas as pl
from jax.experimental.pallas import tpu as pltpu

F32 = jnp.float32
BF16 = jnp.bfloat16
HIGHEST = lax.Precision.HIGHEST

LANES = 128
HALF = LANES // 2
SUBLANES = 8
BF16_SUBLANES = 2 * SUBLANES

CONV_WIDTH = 4
LRU_C = 8.0
NSA_HEADS = 8
HEAD_DIM = 64
NSA_KV_HEADS = 2
NSA_GROUP = NSA_HEADS // NSA_KV_HEADS
N_BRANCH = 3
CMP_STRIDE = 16
CMP_BLOCK = 2 * CMP_STRIDE
SEL_BLOCK = 64
SEL_SHIFT = SEL_BLOCK.bit_length() - 1
SEL_TOPK = 16
WINDOW = 512
ATTN_SCALE = HEAD_DIM ** -0.5
LOG2E = math.log2(math.e)
NEG_INF = -1e30
FORCE = 1e6
EPS = 1e-6

PROJ_ROWS = 1024
OUT_ROWS = 1024
ATTN_ROWS = 128
SLC_KEYS = 512
KEY_CHUNK = 128
VMEM_LIMIT = 56 * 1024 * 1024

FEAT_BLK = SEL_BLOCK
FEAT_OFF = SEL_BLOCK + 3
FEAT_ONE = SEL_BLOCK + 6
MAX_KEY_CHUNKS = 32
FEAT_CHUNK = LANES - MAX_KEY_CHUNKS
N_SPLIT = 3

assert HEAD_DIM == HALF and NSA_KV_HEADS == 2 and KEY_CHUNK == LANES
assert ATTN_ROWS % KEY_CHUNK == 0 and ATTN_ROWS <= WINDOW


def _bf16_terms(value):
    terms, rest = [], np.float32(value)
    for _ in range(N_SPLIT):
        t = np.float32(rest).astype(ml_dtypes.bfloat16).astype(np.float32)
        terms.append(float(t))
        rest = np.float32(rest - t)
    return terms


def _alibi_coef(head):
    return float(2.0 ** (-8.0 * (head + 1) / NSA_HEADS)) * LOG2E


def _nt_dot(a, b):
    return lax.dot_general(a, b, (((1,), (1,)), ((), ())), preferred_element_type=F32)


def _half_rms(x, gain2):
    lane = lax.broadcasted_iota(jnp.int32, (1, LANES), 1)
    lo = lane < HALF
    sq = x * x
    ss_lo = jnp.sum(jnp.where(lo, sq, 0.0), axis=-1, keepdims=True)
    ss_hi = jnp.sum(jnp.where(lo, 0.0, sq), axis=-1, keepdims=True)
    r = jnp.where(lo, lax.rsqrt(ss_lo * (1.0 / HALF) + EPS), lax.rsqrt(ss_hi * (1.0 / HALF) + EPS))
    return x * r * gain2


def _key_features(pos, onehot):
    lane = lax.broadcasted_iota(jnp.int32, (1, LANES), 1)
    blk = jnp.right_shift(pos, SEL_SHIFT)
    off = jnp.bitwise_and(pos, SEL_BLOCK - 1)
    f = jnp.where(jnp.logical_and(lane >= FEAT_BLK, lane < FEAT_OFF), blk.astype(F32),
                  jnp.where(jnp.logical_and(lane >= FEAT_OFF, lane < FEAT_ONE), off.astype(F32),
                            jnp.where(lane == FEAT_ONE, 1.0, 0.0)))
    if onehot:
        f = jnp.where(lane < SEL_BLOCK, jnp.where(lane == blk, 1.0, 0.0), f)
        chunk = jnp.right_shift(pos, KEY_CHUNK.bit_length() - 1)
        f = jnp.where(lane >= FEAT_CHUNK, jnp.where(lane - FEAT_CHUNK == chunk, 1.0, 0.0), f)
    return f


def _mod_kernel(c_ref, w_ref, b_ref, o_ref):
    c = c_ref[...]
    o_ref[...] = jnp.dot(c * jax.nn.sigmoid(c), w_ref[...], preferred_element_type=F32, precision=HIGHEST) + b_ref[...]


def _mod(c, w_ada, b_ada):
    bsz, d = c.shape
    n = w_ada.shape[1]
    return pl.pallas_call(
        _mod_kernel,
        out_shape=jax.ShapeDtypeStruct((bsz, n), F32),
        grid=(n // d,),
        in_specs=[pl.BlockSpec((bsz, d), lambda j: (0, 0)),
                  pl.BlockSpec((d, d), lambda j: (0, j)),
                  pl.BlockSpec((1, d), lambda j: (0, j))],
        out_specs=pl.BlockSpec((bsz, d), lambda j: (0, j)),
        name="adaln_mod",
    )(c, w_ada, b_ada.reshape(1, n))


def _log1p(y):
    w = 1.0 + y
    return jnp.where(w == 1.0, y, jnp.log(w) * (y / (w - 1.0)))


def _softplus(x):
    return jnp.maximum(x, 0.0) + _log1p(jnp.exp(-jnp.abs(x)))


def _sigmoid(x):
    return 0.5 * jnp.tanh(0.5 * x) + 0.5


def _lru_gates(x, first, cw_ref, cb_ref, wa_ref, ba_ref, wx_ref, bx_ref, lam_ref, tail_ref, h_ref):
    rows, width = x.shape

    @pl.when(first)
    def _():
        tail_ref[...] = jnp.zeros_like(tail_ref)
        h_ref[...] = jnp.zeros_like(h_ref)

    tail = tail_ref[...]
    tail_ref[...] = x[rows - SUBLANES:rows]
    row8 = lax.broadcasted_iota(jnp.int32, (SUBLANES, 1), 0)
    xc = cb_ref[...] + cw_ref[CONV_WIDTH - 1:CONV_WIDTH, :] * x
    for k in range(1, CONV_WIDTH):
        xs = pltpu.roll(x, k, 0)
        head = jnp.where(row8 < k, pltpu.roll(tail, k, 0), xs[0:SUBLANES])
        xs = jnp.concatenate([head, xs[SUBLANES:]], axis=0)
        xc = xc + cw_ref[CONV_WIDTH - 1 - k:CONV_WIDTH - k, :] * xs

    xcb = xc.astype(BF16)
    nslot = width // LANES
    ra = jnp.concatenate([jnp.dot(xcb[:, s * LANES:(s + 1) * LANES], wa_ref[s], preferred_element_type=F32)
                          for s in range(nslot)], axis=1)
    ri = jnp.concatenate([jnp.dot(xcb[:, s * LANES:(s + 1) * LANES], wx_ref[s], preferred_element_type=F32)
                          for s in range(nslot)], axis=1)
    r = _sigmoid(ra + ba_ref[...])
    i = _sigmoid(ri + bx_ref[...])
    log_a = (-LRU_C) * r * _softplus(-lam_ref[...])
    a = jnp.exp(log_a)
    y = -jnp.tanh(log_a) * (a * a + 1.0)
    return a, jnp.where(y > 0.0, y * lax.rsqrt(y), 0.0) * (i * xc)


def _lru_scan_groups(a, u):
    rows = a.shape[0]
    sub = jnp.bitwise_and(lax.broadcasted_iota(jnp.int32, (rows, 1), 0), SUBLANES - 1)
    acc_a, acc_h = a, u
    d = 1
    while d < SUBLANES:
        keep = sub >= d
        sh_a = pltpu.roll(acc_a, d, 0)
        sh_h = pltpu.roll(acc_h, d, 0)
        acc_h = jnp.where(keep, acc_a * sh_h + acc_h, acc_h)
        acc_a = jnp.where(keep, acc_a * sh_a, acc_a)
        d *= 2
    return acc_a, acc_h


def _lru_carry(acc_a, acc_h, h_ref):
    rows = acc_a.shape[0]
    carry = h_ref[0:1, :]
    groups = []
    for g in range(rows // SUBLANES):
        hg = acc_h[g * SUBLANES:(g + 1) * SUBLANES] + acc_a[g * SUBLANES:(g + 1) * SUBLANES] * carry
        carry = hg[SUBLANES - 1:SUBLANES]
        groups.append(hg)
    h_ref[0:1, :] = carry
    return jnp.concatenate(groups, axis=0)


def _lru_out(h, z, g_ref):
    ms = jnp.mean(h * h, axis=-1, keepdims=True)
    return (h * lax.rsqrt(ms + EPS) * g_ref[...]) * (z * _sigmoid(z))


def _proj_kernel(cols, rows_t, x_ref, mod_ref, ng_ref, w_ref, wt_ref, gq_ref, gks_ref, gkw_ref,
                 cw_ref, cb_ref, wa_ref, ba_ref, wx_ref, bx_ref, lam_ref, gl_ref,
                 ylru_ref, kcmp_ref, vcmp_ref, kslc_ref, kwin_ref,
                 qt_ref, vst_ref, vwt_ref, gt_ref, znsat_ref, tail_ref, h_ref):
    rows = x_ref.shape[1]
    x = x_ref[0]
    ms = jnp.mean(x * x, axis=-1, keepdims=True)
    gain = ng_ref[...] * (1.0 + mod_ref[0, 1:2, :])
    h = (x * lax.rsqrt(ms + EPS)) * gain + mod_ref[0, 0:1, :]
    hb = h.astype(BF16)

    def mm(name):
        c0, n = cols[name]
        return jnp.dot(hb, w_ref[:, c0:c0 + n], preferred_element_type=F32)

    def mm_pair(first, second):
        (c0, n0), (c1, n1) = cols[first], cols[second]
        assert c1 == c0 + n0
        both = jnp.dot(hb, w_ref[:, c0:c1 + n1], preferred_element_type=F32)
        return both[:, 0:n0], both[:, n0:n0 + n1]

    def mmt(*names):
        spans = [rows_t[name] for name in names]
        r0 = spans[0][0]
        assert all(a[0] + a[1] == b[0] for a, b in zip(spans, spans[1:]))
        out = _nt_dot(wt_ref[r0:spans[-1][0] + spans[-1][1], :], hb)
        parts = tuple(out[r - r0:r - r0 + n] for r, n in spans)
        return parts[0] if len(parts) == 1 else parts

    def after(value, done):
        return value + jnp.minimum(jnp.abs(done[done.shape[0] - 1:, 0:1]), 0.0)

    lru_a, lru_u = _lru_gates(mm("xlru"), pl.program_id(1) == 0, cw_ref, cb_ref, wa_ref, ba_ref,
                              wx_ref, bx_ref, lam_ref, tail_ref, h_ref)
    z_lru = mm("zlru")
    kcmp_ref[0], vcmp_ref[0] = mm_pair("kcmp", "vcmp")

    pos = pl.program_id(1) * rows + lax.broadcasted_iota(jnp.int32, (rows, 1), 0)
    k_slc, k_win = mm_pair("kslc", "kwin")
    kslc_ref[0, :, 0:LANES] = _half_rms(k_slc, gks_ref[...]).astype(BF16)
    kslc_ref[0, :, LANES:2 * LANES] = _key_features(pos, True).astype(BF16)
    kwin_ref[0, :, 0:LANES] = _half_rms(k_win, gkw_ref[...]).astype(BF16)
    kwin_ref[0, :, LANES:2 * LANES] = _key_features(pos, True).astype(BF16)

    lru_a, lru_h = _lru_scan_groups(lru_a, after(lru_u, k_win))
    qt = mmt("q")
    for hd in range(NSA_HEADS):
        qh = qt[hd * HEAD_DIM:(hd + 1) * HEAD_DIM]
        r = lax.rsqrt(jnp.mean(qh * qh, axis=0, keepdims=True) + EPS)
        qt_ref[0, hd * HEAD_DIM:(hd + 1) * HEAD_DIM, :] = (qh * r * gq_ref[...]).astype(BF16)

    h_lru = _lru_carry(lru_a, after(lru_h, qt), h_ref)
    vst, vwt, gate_logits, z_nsa = mmt("vslc", "vwin", "gates", "znsa")
    vst, vwt = vst.astype(BF16), vwt.astype(BF16)
    for j in range(rows // KEY_CHUNK):
        vst_ref[0, j] = vst[:, j * KEY_CHUNK:(j + 1) * KEY_CHUNK]
        vwt_ref[0, j] = vwt[:, j * KEY_CHUNK:(j + 1) * KEY_CHUNK]
    gt_ref[0] = jax.nn.sigmoid(gate_logits)
    znsat_ref[0] = z_nsa.astype(BF16)
    ylru_ref[0] = _lru_out(after(h_lru, z_nsa), z_lru, gl_ref).astype(BF16)


def _proj(x, mod3, norm_g, w_big, cols, w_t, rows_t, gq_b, gks2, gkw2, lru_params):
    bsz, seq, d = x.shape
    ts = PROJ_ROWS
    nsa_w = NSA_HEADS * HEAD_DIM
    lru_w = cols["xlru"][1]
    n_gate = rows_t["gates"][1]

    def tok_spec(n):
        return pl.BlockSpec((1, ts, n), lambda b, s: (b, s, 0))

    def feat_spec(n):
        return pl.BlockSpec((1, n, ts), lambda b, s: (b, 0, s))

    def chunk_spec():
        return pl.BlockSpec((1, ts // KEY_CHUNK, LANES, KEY_CHUNK), lambda b, s: (b, s, 0, 0))

    def const_spec(shape):
        return pl.BlockSpec(shape, lambda b, s: (0,) * len(shape))

    out_shape = (
        jax.ShapeDtypeStruct((bsz, seq, lru_w), BF16),
        jax.ShapeDtypeStruct((bsz, seq, LANES), F32),
        jax.ShapeDtypeStruct((bsz, seq, LANES), F32),
        jax.ShapeDtypeStruct((bsz, seq, 2 * LANES), BF16),
        jax.ShapeDtypeStruct((bsz, seq, 2 * LANES), BF16),
        jax.ShapeDtypeStruct((bsz, nsa_w, seq), BF16),
        jax.ShapeDtypeStruct((bsz, seq // KEY_CHUNK, LANES, KEY_CHUNK), BF16),
        jax.ShapeDtypeStruct((bsz, seq // KEY_CHUNK, LANES, KEY_CHUNK), BF16),
        jax.ShapeDtypeStruct((bsz, n_gate, seq), F32),
        jax.ShapeDtypeStruct((bsz, nsa_w, seq), BF16),
    )
    out_specs = (tok_spec(lru_w), tok_spec(LANES), tok_spec(LANES),
                 tok_spec(2 * LANES), tok_spec(2 * LANES),
                 feat_spec(nsa_w), chunk_spec(), chunk_spec(), feat_spec(n_gate), feat_spec(nsa_w))
    return pl.pallas_call(
        functools.partial(_proj_kernel, cols, rows_t),
        out_shape=out_shape,
        grid=(bsz, seq // ts),
        in_specs=[tok_spec(d),
                  pl.BlockSpec((1, 3, d), lambda b, s: (b, 0, 0)),
                  const_spec((1, d)),
                  const_spec(w_big.shape), const_spec(w_t.shape),
                  const_spec((HEAD_DIM, ts)), const_spec((1, LANES)), const_spec((1, LANES))]
                 + [const_spec(p.shape) for p in lru_params],
        out_specs=out_specs,
        scratch_shapes=[pltpu.VMEM((SUBLANES, lru_w), F32), pltpu.VMEM((SUBLANES, lru_w), F32)],
        compiler_params=pltpu.CompilerParams(dimension_semantics=("parallel", "arbitrary"),
                                             vmem_limit_bytes=VMEM_LIMIT),
        name="in_proj_lru",
    )(x, mod3, norm_g, w_big, w_t, gq_b, gks2, gkw2, *lru_params)


def _compress_kernel(is_key, c_ref, w1x_ref, pos_ref, w1_ref, w2lo_ref, w2hi_ref, g_ref, o_ref, const_ref):
    nblk = c_ref.shape[1] // CMP_STRIDE
    hid = w1_ref.shape[1]

    @pl.when(pl.program_id(0) == 0)
    def _():
        const_ref[...] = jnp.dot(pos_ref[...], w1_ref[...], preferred_element_type=F32, precision=HIGHEST)

    ab = jnp.zeros((nblk, 4 * hid), F32)
    for j in range(0, CMP_STRIDE, 2):
        pair = jnp.concatenate([c_ref[0, pl.ds(j + i, nblk, stride=CMP_STRIDE), :].astype(BF16) for i in range(2)],
                               axis=1)
        ab = ab + jnp.dot(pair, w1x_ref[j * LANES:(j + 2) * LANES, :], preferred_element_type=F32)
    const = const_ref[0:1, :]
    acts = []
    for hk in range(NSA_KV_HEADS):
        first = ab[:, (2 * hk) * hid:(2 * hk + 1) * hid]
        second = ab[:, (2 * hk + 1) * hid:(2 * hk + 2) * hid]
        hidden = first + pltpu.roll(second, nblk - 1, 0) + const
        acts.append(jax.nn.gelu(hidden).astype(BF16))
    if is_key:
        out = (jnp.dot(acts[0], w2lo_ref[...], preferred_element_type=F32)
               + jnp.dot(acts[1], w2hi_ref[...], preferred_element_type=F32))
        o_ref[0, :, 0:LANES] = _half_rms(out, g_ref[...]).astype(BF16)
        cend = CMP_STRIDE * lax.broadcasted_iota(jnp.int32, (nblk, 1), 0) + (CMP_BLOCK - 1)
        o_ref[0, :, LANES:2 * LANES] = _key_features(cend, False).astype(BF16)
    else:
        out_t = _nt_dot(w2lo_ref[...], acts[0]) + _nt_dot(w2hi_ref[...], acts[1])
        o_ref[0] = out_t.astype(BF16)


def _compress(tokens, w1x, pos_flat, w1, w2lo, w2hi, gain2, is_key):
    bsz, seq, width = tokens.shape
    nblk = seq // CMP_STRIDE
    hid = w1.shape[1]

    def const_spec(shape):
        return pl.BlockSpec(shape, lambda b: (0,) * len(shape))

    out_tail = (nblk, 2 * LANES) if is_key else (LANES, nblk)
    return pl.pallas_call(
        functools.partial(_compress_kernel, is_key),
        out_shape=jax.ShapeDtypeStruct((bsz,) + out_tail, BF16),
        grid=(bsz,),
        in_specs=[pl.BlockSpec((1, seq, width), lambda b: (b, 0, 0)),
                  const_spec(w1x.shape), const_spec((8, w1.shape[0])), const_spec(w1.shape),
                  const_spec(w2lo.shape), const_spec(w2hi.shape), const_spec((1, LANES))],
        out_specs=pl.BlockSpec((1,) + out_tail, lambda b: (b, 0, 0)),
        scratch_shapes=[pltpu.VMEM((SUBLANES, hid), F32)],
        compiler_params=pltpu.CompilerParams(dimension_semantics=("arbitrary",), vmem_limit_bytes=VMEM_LIMIT),
        name="compress_k" if is_key else "compress_v",
    )(tokens, w1x, pos_flat, w1, w2lo, w2hi, gain2)


ONES_ROWS = BF16_SUBLANES


def _with_ones(vt):
    return jnp.concatenate([vt, jnp.ones((ONES_ROWS, vt.shape[1]), BF16)], axis=0)


def _online_update(carry, s, top, vt):
    m, acc = carry
    m_new = jnp.maximum(m, top)
    p = jnp.exp2(s - m_new)
    acc = jnp.exp2(m - m_new) * acc + jnp.dot(_with_ones(vt), p.astype(BF16), preferred_element_type=F32)
    return m_new, acc


def _normalised(acc):
    return acc[0:HEAD_DIM] * (1.0 / acc[HEAD_DIM:HEAD_DIM + 1])


def _attn_kernel(qt_ref, kc_ref, vct_ref, ks_ref, vst_ref, kw_ref, vwt_ref, gt_ref, ovt_ref,
                 o_ref, impt_ref, s_ref, sel_ref):
    tq = qt_ref.shape[2]
    n_cmp = kc_ref.shape[1]
    n_sel = impt_ref.shape[1]
    ncol = NSA_GROUP * tq
    per_wide = SLC_KEYS // KEY_CHUNK
    per_tile = tq // KEY_CHUNK
    qi = pl.program_id(1)
    t0 = qi * tq
    c0 = qi * per_tile
    tloc = lax.broadcasted_iota(jnp.int32, (1, tq), 1)
    tcol = jnp.concatenate([tloc] * NSA_GROUP, axis=1)
    tpos = t0 + tloc
    nrow = lax.broadcasted_iota(jnp.int32, (n_sel, 1), 0)
    cur = jnp.right_shift(tpos, SEL_SHIFT)
    forced = jnp.logical_or(jnp.logical_or(nrow == 0, nrow == cur), nrow == cur - 1)
    frow = lax.broadcasted_iota(jnp.int32, (HEAD_DIM, 1), 0)
    srow = lax.broadcasted_iota(jnp.int32, (SUBLANES, 1), 0)
    zeros_q = jnp.zeros((HEAD_DIM, tq), BF16)
    row_minus_col = lax.broadcasted_iota(jnp.int32, (tq, 1), 0) - tcol
    own_ok = row_minus_col <= 0

    cidx = lax.broadcasted_iota(jnp.int32, (n_cmp, 1), 0)
    cend = CMP_STRIDE * cidx + (CMP_BLOCK - 1)
    cmp_ok = jnp.logical_and(cend <= t0 + tcol, cidx < n_cmp - 1)
    col_has_cmp = (t0 + tcol) >= CMP_BLOCK - 1
    gates = gt_ref[0]
    kv_heads = range(NSA_KV_HEADS)

    def values(ref, hk, first_chunk, n):
        return jnp.concatenate([ref[0, first_chunk + i, hk * HEAD_DIM:(hk + 1) * HEAD_DIM, :]
                                for i in range(n)], axis=1)

    not_past = jnp.logical_and(frow >= FEAT_CHUNK - SEL_BLOCK, frow - (FEAT_CHUNK - SEL_BLOCK) >= c0)

    q_plain, q_plain_past, o_cmp = [], [], []
    for hk in kv_heads:
        q_rows, feat_rows, feat_rows_past = [], [], []
        for g in range(NSA_GROUP):
            hd = hk * NSA_GROUP + g
            qh = qt_ref[0, hd * HEAD_DIM:(hd + 1) * HEAD_DIM, :]
            q_rows.append([qh, zeros_q] if hk == 0 else [zeros_q, qh])
            c = _alibi_coef(hd)
            a = jnp.zeros((HEAD_DIM, 1), F32)
            for i, term in enumerate(_bf16_terms(c)):
                a = jnp.where(frow == FEAT_BLK - SEL_BLOCK + i, SEL_BLOCK * term, a)
                a = jnp.where(frow == FEAT_OFF - SEL_BLOCK + i, term, a)
            a = jnp.where(frow == FEAT_ONE - SEL_BLOCK, -c * t0.astype(F32), a)
            feat_rows.append(jnp.broadcast_to(a, (HEAD_DIM, tq)).astype(BF16))
            feat_rows_past.append(jnp.broadcast_to(jnp.where(not_past, NEG_INF, a), (HEAD_DIM, tq)).astype(BF16))

        def q_operand(pen, feats, q_rows=q_rows):
            return jnp.concatenate([jnp.concatenate(q_rows[g] + [pen, feats[g]], axis=0)
                                    for g in range(NSA_GROUP)], axis=1)

        q_plain.append(q_operand(zeros_q, feat_rows))
        q_plain_past.append(q_operand(zeros_q, feat_rows_past))
        sel_ref[0, hk] = q_plain[hk]
        sel_ref[1, hk] = q_plain_past[hk]

        s = jnp.dot(kc_ref[0], q_plain[hk], preferred_element_type=F32)
        s = jnp.where(cmp_ok, s, NEG_INF)
        e = jnp.exp2(s - jnp.max(s, axis=0, keepdims=True))
        p = e * jnp.where(col_has_cmp, 1.0 / jnp.sum(e, axis=0, keepdims=True), 0.0)
        o_cmp.append(jnp.dot(vct_ref[0, hk * HEAD_DIM:(hk + 1) * HEAD_DIM, :], p.astype(BF16),
                             preferred_element_type=F32))
        p_sum = p[:, 0:tq]
        for g in range(1, NSA_GROUP):
            p_sum = p_sum + p[:, g * tq:(g + 1) * tq]
        imp = jnp.zeros((n_sel, tq), F32)
        rest = p_sum
        for _ in range(N_SPLIT):
            term = rest.astype(BF16)
            imp = imp + jnp.dot(ovt_ref[...], term, preferred_element_type=F32)
            rest = rest - term.astype(F32)
        imp = jnp.where(forced, FORCE, jnp.where(nrow > cur, -FORCE, imp))

        impt_ref[hk] = imp
        groups = [imp[b * SUBLANES:(b + 1) * SUBLANES] for b in range(n_sel // SUBLANES)]
        ranks = [jnp.zeros((SUBLANES, tq), F32) for _ in groups]
        for m in range(n_sel):
            other = impt_ref[hk, m:m + 1, :]
            for b, grp in enumerate(groups):
                if b * SUBLANES > m:
                    ahead = other >= grp
                elif (b + 1) * SUBLANES <= m:
                    ahead = other > grp
                else:
                    ahead = jnp.logical_or(other > grp, jnp.logical_and(other == grp, srow > m - b * SUBLANES))
                ranks[b] = ranks[b] + jnp.where(ahead, 1.0, 0.0)
        rank = jnp.concatenate(ranks, axis=0)
        pen = jnp.where(rank < float(SEL_TOPK), 0.0, NEG_INF).astype(BF16)
        pen_rows = jnp.concatenate([pen] * NSA_GROUP, axis=1)
        sel_ref[0, hk, 2 * HEAD_DIM:3 * HEAD_DIM, :] = pen_rows
        sel_ref[1, hk, 2 * HEAD_DIM:3 * HEAD_DIM, :] = pen_rows

    init = (jnp.full((1, ncol), NEG_INF, F32), jnp.zeros((HEAD_DIM + ONES_ROWS, ncol), F32))

    n_back = WINDOW // KEY_CHUNK
    wc0 = jnp.maximum(c0 - n_back, 0)
    lead = t0 - wc0 * KEY_CHUNK
    k_back = kw_ref[0, pl.ds(pl.multiple_of(wc0 * KEY_CHUNK, KEY_CHUNK), WINDOW), :]
    k_tile = kw_ref[0, pl.ds(pl.multiple_of(t0, KEY_CHUNK), tq), :]
    recent = row_minus_col > lead - WINDOW
    o_win = []
    for hk in kv_heads:
        s_back = jnp.dot(k_back, q_plain_past[hk], preferred_element_type=F32)
        s_own = jnp.where(own_ok, jnp.dot(k_tile, q_plain[hk], preferred_element_type=F32), NEG_INF)
        s = jnp.concatenate([jnp.where(recent, s_back[0:tq], NEG_INF), s_back[tq:], s_own], axis=0)
        vt = jnp.concatenate([values(vwt_ref, hk, wc0, n_back), values(vwt_ref, hk, c0, per_tile)], axis=1)
        _, acc = _online_update(init, s, jnp.max(s, axis=0, keepdims=True), vt)
        o_win.append(_normalised(acc))

    k_own = ks_ref[0, pl.ds(pl.multiple_of(t0, KEY_CHUNK), tq), :]
    stats = []
    for hk in kv_heads:
        s = jnp.where(own_ok, jnp.dot(k_own, sel_ref[0, hk], preferred_element_type=F32), NEG_INF)
        stats.append(_online_update(init, s, jnp.max(s, axis=0, keepdims=True), values(vst_ref, hk, c0, per_tile)))
    stats = tuple(stats)

    def produce(j, slot):
        k = ks_ref[0, j * SLC_KEYS:(j + 1) * SLC_KEYS, :]
        tops = []
        for hk in kv_heads:
            s = jnp.dot(k, sel_ref[1, hk], preferred_element_type=F32)
            s_ref[slot, hk] = s
            tops.append(jnp.max(s, axis=0, keepdims=True))
        return tuple(tops)

    def consume(j, slot, tops, stats):
        return tuple(_online_update(stats[hk], s_ref[slot, hk], tops[hk],
                                    values(vst_ref, hk, j * per_wide, per_wide)) for hk in kv_heads)

    def chain(n_blocks):
        def run(stats):
            tops = produce(0, 0) if n_blocks else None
            for j in range(n_blocks):
                nxt = produce(j + 1, (j + 1) % 2) if j + 1 < n_blocks else None
                stats = consume(j, j % 2, tops, stats)
                tops = nxt
            return stats
        return run

    stats = lax.switch((t0 + SLC_KEYS - 1) // SLC_KEYS,
                       [chain(n) for n in range(ks_ref.shape[1] // SLC_KEYS + 1)], stats)
    o_slc = [_normalised(acc) for _, acc in stats]

    for hk in kv_heads:
        for g in range(NSA_GROUP):
            hd = hk * NSA_GROUP + g
            tot = jnp.zeros((HEAD_DIM, tq), F32)
            for br, o_br in enumerate((o_cmp[hk], o_slc[hk], o_win[hk])):
                r = br * NSA_HEADS + hd
                tot = tot + gates[r:r + 1, :] * o_br[:, g * tq:(g + 1) * tq]
            o_ref[0, hd * HEAD_DIM:(hd + 1) * HEAD_DIM, :] = tot.astype(BF16)


def _attn(qt, kc, vct, kslc, vst, kwin, vwt, gt, overlap_t):
    bsz, nsa_w, seq = qt.shape
    tq = ATTN_ROWS
    n_cmp = kc.shape[1]
    n_sel = seq // SEL_BLOCK
    n_gate = gt.shape[1]
    assert seq % (2 * SLC_KEYS) == 0 and n_sel <= HALF and seq >= WINDOW + tq
    assert seq // KEY_CHUNK <= LANES - FEAT_CHUNK and n_sel % SUBLANES == 0

    def per_batch(shape):
        return pl.BlockSpec((1,) + shape, lambda b, i: (b,) + (0,) * len(shape))

    def const_spec(shape):
        return pl.BlockSpec(shape, lambda b, i: (0,) * len(shape))

    return pl.pallas_call(
        _attn_kernel,
        out_shape=jax.ShapeDtypeStruct((bsz, nsa_w, seq), BF16),
        grid=(bsz, seq // tq),
        in_specs=[pl.BlockSpec((1, nsa_w, tq), lambda b, i: (b, 0, i)),
                  per_batch((n_cmp, 2 * LANES)), per_batch((LANES, n_cmp)),
                  per_batch((seq, 2 * LANES)), per_batch((seq // KEY_CHUNK, LANES, KEY_CHUNK)),
                  per_batch((seq, 2 * LANES)), per_batch((seq // KEY_CHUNK, LANES, KEY_CHUNK)),
                  pl.BlockSpec((1, n_gate, tq), lambda b, i: (b, 0, i)),
                  const_spec(overlap_t.shape)],
        out_specs=pl.BlockSpec((1, nsa_w, tq), lambda b, i: (b, 0, i)),
        scratch_shapes=[pltpu.VMEM((NSA_KV_HEADS, n_sel, tq), F32),
                        pltpu.VMEM((2, NSA_KV_HEADS, SLC_KEYS, NSA_GROUP * tq), F32),
                        pltpu.VMEM((2, NSA_KV_HEADS, 2 * LANES, NSA_GROUP * tq), BF16)],
        compiler_params=pltpu.CompilerParams(dimension_semantics=("parallel", "arbitrary"),
                                             vmem_limit_bytes=VMEM_LIMIT),
        name="nsa_attn",
    )(qt, kc, vct, kslc, vst, kwin, vwt, gt, overlap_t)


def _out_kernel(x_ref, ylru_ref, onsat_ref, znsat_ref, gn_ref, gate_ref, w_ref, o_ref):
    o = onsat_ref[0].astype(F32)
    ms = jnp.mean(o * o, axis=0, keepdims=True)
    z = znsat_ref[0].astype(F32)
    y_t = (o * lax.rsqrt(ms + EPS) * gn_ref[...]) * (z * jax.nn.sigmoid(z))
    y_nsa = jnp.transpose(y_t.astype(BF16))
    out = jnp.dot(jnp.concatenate([ylru_ref[0], y_nsa], axis=1), w_ref[...], preferred_element_type=F32)
    o_ref[0] = x_ref[0] + gate_ref[0, 2:3, :] * out


def _out(x, ylru, onsat, znsat, g_nsa_b, mod3, w_out):
    bsz, seq, d = x.shape
    ts = OUT_ROWS
    nsa_w = onsat.shape[1]

    def tok_spec(n):
        return pl.BlockSpec((1, ts, n), lambda b, s: (b, s, 0))

    def feat_spec(n):
        return pl.BlockSpec((1, n, ts), lambda b, s: (b, 0, s))

    def const_spec(shape):
        return pl.BlockSpec(shape, lambda b, s: (0,) * len(shape))

    return pl.pallas_call(
        _out_kernel,
        out_shape=jax.ShapeDtypeStruct((bsz, seq, d), F32),
        grid=(bsz, seq // ts),
        in_specs=[tok_spec(d), tok_spec(ylru.shape[2]), feat_spec(nsa_w), feat_spec(nsa_w),
                  const_spec((nsa_w, ts)),
                  pl.BlockSpec((1, 3, d), lambda b, s: (b, 0, 0)),
                  const_spec(w_out.shape)],
        out_specs=tok_spec(d),
        compiler_params=pltpu.CompilerParams(dimension_semantics=("parallel", "parallel"),
                                             vmem_limit_bytes=VMEM_LIMIT),
        name="out_proj",
    )(x, ylru, onsat, znsat, g_nsa_b, mod3, w_out)


def _overlap_t(n_cmp_pad, n_sel):
    ratio = SEL_BLOCK // CMP_STRIDE
    ov = np.zeros((n_sel, n_cmp_pad), np.float32)
    for c in range(n_cmp_pad - 1):
        for n in (c // ratio, (c + 1) // ratio):
            if n < n_sel:
                ov[n, c] += 1.0
    return ov


def _block_diag_pairs(w):
    nb, bs, _ = w.shape
    z = jnp.zeros((bs, bs), w.dtype)
    return jnp.stack([jnp.block([[w[2 * s], z], [z, w[2 * s + 1]]]) for s in range(nb // 2)])


def _compress_weights(w1, w2, transposed):
    hid = w1.shape[1]
    w1r = w1.reshape(2, CMP_STRIDE, HEAD_DIM, hid)
    parts = []
    for hk in range(NSA_KV_HEADS):
        for half in range(2):
            slot = jnp.zeros((CMP_STRIDE, NSA_KV_HEADS, HEAD_DIM, hid), w1.dtype).at[:, hk].set(w1r[half])
            parts.append(slot.reshape(CMP_STRIDE * LANES, hid))
    w1x = jnp.concatenate(parts, axis=1).astype(BF16)
    zpad = jnp.zeros_like(w2)
    w2lo = jnp.concatenate([w2, zpad], axis=1).astype(BF16)
    w2hi = jnp.concatenate([zpad, w2], axis=1).astype(BF16)
    if transposed:
        w2lo, w2hi = w2lo.T, w2hi.T
    return w1x, w2lo, w2hi


def _pad_to(n, m):
    return -(-n // m) * m


def _layer(x, c, w_ada, b_ada, norm_g, w_in, conv_w, conv_b, w_rg_a, b_rg_a, w_rg_x, b_rg_x, lru_lambda,
           cmp_pos_k, cmp_w1_k, cmp_w2_k, cmp_pos_v, cmp_w1_v, cmp_w2_v, g_q, g_k_cmp, g_k_slc, g_k_win,
           g_out_lru, g_out_nsa, w_out):
    bsz, seq, d = x.shape
    lru_w = d // 2
    nsa_w = NSA_HEADS * HEAD_DIM
    kv_w = NSA_KV_HEADS * HEAD_DIM
    n_sel = seq // SEL_BLOCK

    splits = (lru_w, lru_w, nsa_w, kv_w, kv_w, kv_w, kv_w, kv_w, kv_w, N_BRANCH * NSA_HEADS, nsa_w)
    offs = np.concatenate([[0], np.cumsum(splits)])
    names = ("xlru", "zlru", "q", "kcmp", "vcmp", "kslc", "vslc", "kwin", "vwin", "gates", "znsa")
    src = {n: (int(offs[i]), int(offs[i + 1])) for i, n in enumerate(names)}

    def plan(group, align):
        pieces, where, at = [], {}, 0
        for n in group:
            piece = w_in[:, src[n][0]:src[n][1]]
            width = _pad_to(piece.shape[1], align)
            if width != piece.shape[1]:
                piece = jnp.pad(piece, ((0, 0), (0, width - piece.shape[1])))
            pieces.append(piece)
            where[n] = (at, width)
            at += width
        return jnp.concatenate(pieces, axis=1).astype(BF16), where

    w_big, cols = plan(("xlru", "zlru", "kcmp", "vcmp", "kslc", "kwin"), LANES)
    w_tt, rows_t = plan(("q", "vslc", "vwin", "gates", "znsa"), BF16_SUBLANES)
    w_t = w_tt.T

    def dup(g):
        return jnp.concatenate([g, g]).reshape(1, LANES)

    gq_b = jnp.broadcast_to((g_q * (ATTN_SCALE * LOG2E)).reshape(HEAD_DIM, 1), (HEAD_DIM, PROJ_ROWS))
    mod3 = _mod(c, w_ada, b_ada).reshape(bsz, 3, d)
    lru_params = (conv_w, conv_b.reshape(1, lru_w),
                  _block_diag_pairs(w_rg_a).astype(BF16), b_rg_a.reshape(1, lru_w),
                  _block_diag_pairs(w_rg_x).astype(BF16), b_rg_x.reshape(1, lru_w),
                  lru_lambda.reshape(1, lru_w), g_out_lru.reshape(1, lru_w))
    (ylru, kcmp, vcmp, kslc, kwin, qt, vst, vwt, gt, znsat) = _proj(
        x, mod3, norm_g.reshape(1, d), w_big, cols, w_t, rows_t, gq_b, dup(g_k_slc), dup(g_k_win), lru_params)

    n_chunk = seq // CMP_STRIDE
    w1x_k, w2lo_k, w2hi_k = _compress_weights(cmp_w1_k, cmp_w2_k, False)
    w1x_v, w2lo_v, w2hi_v = _compress_weights(cmp_w1_v, cmp_w2_v, True)

    def pos_rows(pos):
        return jnp.broadcast_to(pos.reshape(1, CMP_BLOCK * HEAD_DIM), (8, CMP_BLOCK * HEAD_DIM))

    kc = _compress(kcmp, w1x_k, pos_rows(cmp_pos_k), cmp_w1_k,
                   w2lo_k, w2hi_k, dup(g_k_cmp), True)
    vct = _compress(vcmp, w1x_v, pos_rows(cmp_pos_v), cmp_w1_v,
                    w2lo_v, w2hi_v, dup(g_k_cmp), False)

    onsat = _attn(qt, kc, vct, kslc, vst, kwin, vwt, gt, jnp.asarray(_overlap_t(n_chunk, n_sel), dtype=BF16))

    g_nsa_b = jnp.broadcast_to(g_out_nsa.reshape(nsa_w, 1), (nsa_w, OUT_ROWS))
    return _out(x, ylru, onsat, znsat, g_nsa_b, mod3, w_out.astype(BF16))


def kernel(x, c, w_ada, b_ada, norm_g, w_in, conv_w, conv_b, w_rg_a, b_rg_a, w_rg_x, b_rg_x, lru_lambda, cmp_pos_k, cmp_w1_k, cmp_w2_k, cmp_pos_v, cmp_w1_v, cmp_w2_v, g_q, g_k_cmp, g_k_slc, g_k_win, g_out_lru, g_out_nsa, w_out):
    params = (w_ada, b_ada, norm_g, w_in, conv_w, conv_b, w_rg_a, b_rg_a, w_rg_x, b_rg_x, lru_lambda,
              cmp_pos_k, cmp_w1_k, cmp_w2_k, cmp_pos_v, cmp_w1_v, cmp_w2_v, g_q, g_k_cmp, g_k_slc, g_k_win,
              g_out_lru, g_out_nsa, w_out)
    for layer in range(w_in.shape[0]):
        x = _layer(x, c, *(p[layer] for p in params))
    return x
```

```python
import functools
import math

import ml_dtypes
import numpy as np
import jax
import jax.numpy as jnp
from jax import lax
from jax.experimental import pallas as pl
from jax.experimental.pallas import tpu as pltpu

F32 = jnp.float32
BF16 = jnp.bfloat16
HIGHEST = lax.Precision.HIGHEST

LANES = 128
HALF = LANES // 2
SUBLANES = 8
BF16_SUBLANES = 2 * SUBLANES

CONV_WIDTH = 4
LRU_C = 8.0
NSA_HEADS = 8
HEAD_DIM = 64
NSA_KV_HEADS = 2
NSA_GROUP = NSA_HEADS // NSA_KV_HEADS
N_BRANCH = 3
CMP_STRIDE = 16
CMP_BLOCK = 2 * CMP_STRIDE
SEL_BLOCK = 64
SEL_SHIFT = SEL_BLOCK.bit_length() - 1
SEL_TOPK = 16
WINDOW = 512
ATTN_SCALE = HEAD_DIM ** -0.5
LOG2E = math.log2(math.e)
NEG_INF = -1e30
FORCE = 1e6
EPS = 1e-6

PROJ_ROWS = 1024
OUT_ROWS = 1024
ATTN_ROWS = 128
SLC_KEYS = 512
KEY_CHUNK = 128
VMEM_LIMIT = 56 * 1024 * 1024

FEAT_BLK = SEL_BLOCK
FEAT_OFF = SEL_BLOCK + 3
FEAT_ONE = SEL_BLOCK + 6
MAX_KEY_CHUNKS = 32
FEAT_CHUNK = LANES - MAX_KEY_CHUNKS
N_SPLIT = 3

assert HEAD_DIM == HALF and NSA_KV_HEADS == 2 and KEY_CHUNK == LANES
assert ATTN_ROWS % KEY_CHUNK == 0 and ATTN_ROWS <= WINDOW


def _bf16_terms(value):
    terms, rest = [], np.float32(value)
    for _ in range(N_SPLIT):
        t = np.float32(rest).astype(ml_dtypes.bfloat16).astype(np.float32)
        terms.append(float(t))
        rest = np.float32(rest - t)
    return terms


def _alibi_coef(head):
    return float(2.0 ** (-8.0 * (head + 1) / NSA_HEADS)) * LOG2E


def _nt_dot(a, b):
    return lax.dot_general(a, b, (((1,), (1,)), ((), ())), preferred_element_type=F32)


def _half_rms(x, gain2):
    lane = lax.broadcasted_iota(jnp.int32, (1, LANES), 1)
    lo = lane < HALF
    sq = x * x
    ss_lo = jnp.sum(jnp.where(lo, sq, 0.0), axis=-1, keepdims=True)
    ss_hi = jnp.sum(jnp.where(lo, 0.0, sq), axis=-1, keepdims=True)
    r = jnp.where(lo, lax.rsqrt(ss_lo * (1.0 / HALF) + EPS), lax.rsqrt(ss_hi * (1.0 / HALF) + EPS))
    return x * r * gain2


def _key_features(pos, onehot):
    lane = lax.broadcasted_iota(jnp.int32, (1, LANES), 1)
    blk = jnp.right_shift(pos, SEL_SHIFT)
    off = jnp.bitwise_and(pos, SEL_BLOCK - 1)
    f = jnp.where(jnp.logical_and(lane >= FEAT_BLK, lane < FEAT_OFF), blk.astype(F32),
                  jnp.where(jnp.logical_and(lane >= FEAT_OFF, lane < FEAT_ONE), off.astype(F32),
                            jnp.where(lane == FEAT_ONE, 1.0, 0.0)))
    if onehot:
        f = jnp.where(lane < SEL_BLOCK, jnp.where(lane == blk, 1.0, 0.0), f)
        chunk = jnp.right_shift(pos, KEY_CHUNK.bit_length() - 1)
        f = jnp.where(lane >= FEAT_CHUNK, jnp.where(lane - FEAT_CHUNK == chunk, 1.0, 0.0), f)
    return f


def _mod_kernel(c_ref, w_ref, b_ref, o_ref):
    c = c_ref[...]
    o_ref[...] = jnp.dot(c * jax.nn.sigmoid(c), w_ref[...], preferred_element_type=F32, precision=HIGHEST) + b_ref[...]


def _mod(c, w_ada, b_ada):
    bsz, d = c.shape
    n = w_ada.shape[1]
    return pl.pallas_call(
        _mod_kernel,
        out_shape=jax.ShapeDtypeStruct((bsz, n), F32),
        grid=(n // d,),
        in_specs=[pl.BlockSpec((bsz, d), lambda j: (0, 0)),
                  pl.BlockSpec((d, d), lambda j: (0, j)),
                  pl.BlockSpec((1, d), lambda j: (0, j))],
        out_specs=pl.BlockSpec((bsz, d), lambda j: (0, j)),
        name="adaln_mod",
    )(c, w_ada, b_ada.reshape(1, n))


def _log1p(y):
    w = 1.0 + y
    return jnp.where(w == 1.0, y, jnp.log(w) * (y / (w - 1.0)))


def _softplus(x):
    return jnp.maximum(x, 0.0) + _log1p(jnp.exp(-jnp.abs(x)))


def _sigmoid(x):
    return 0.5 * jnp.tanh(0.5 * x) + 0.5


def _lru_gates(x, first, cw_ref, cb_ref, wa_ref, ba_ref, wx_ref, bx_ref, lam_ref, tail_ref, h_ref):
    rows, width = x.shape

    @pl.when(first)
    def _():
        tail_ref[...] = jnp.zeros_like(tail_ref)
        h_ref[...] = jnp.zeros_like(h_ref)

    tail = tail_ref[...]
    tail_ref[...] = x[rows - SUBLANES:rows]
    row8 = lax.broadcasted_iota(jnp.int32, (SUBLANES, 1), 0)
    xc = cb_ref[...] + cw_ref[CONV_WIDTH - 1:CONV_WIDTH, :] * x
    for k in range(1, CONV_WIDTH):
        xs = pltpu.roll(x, k, 0)
        head = jnp.where(row8 < k, pltpu.roll(tail, k, 0), xs[0:SUBLANES])
        xs = jnp.concatenate([head, xs[SUBLANES:]], axis=0)
        xc = xc + cw_ref[CONV_WIDTH - 1 - k:CONV_WIDTH - k, :] * xs

    xcb = xc.astype(BF16)
    nslot = width // LANES
    ra = jnp.concatenate([jnp.dot(xcb[:, s * LANES:(s + 1) * LANES], wa_ref[s], preferred_element_type=F32)
                          for s in range(nslot)], axis=1)
    ri = jnp.concatenate([jnp.dot(xcb[:, s * LANES:(s + 1) * LANES], wx_ref[s], preferred_element_type=F32)
                          for s in range(nslot)], axis=1)
    r = _sigmoid(ra + ba_ref[...])
    i = _sigmoid(ri + bx_ref[...])
    log_a = (-LRU_C) * r * _softplus(-lam_ref[...])
    a = jnp.exp(log_a)
    y = -jnp.tanh(log_a) * (a * a + 1.0)
    return a, jnp.where(y > 0.0, y * lax.rsqrt(y), 0.0) * (i * xc)


def _lru_scan_groups(a, u):
    rows = a.shape[0]
    sub = jnp.bitwise_and(lax.broadcasted_iota(jnp.int32, (rows, 1), 0), SUBLANES - 1)
    acc_a, acc_h = a, u
    d = 1
    while d < SUBLANES:
        keep = sub >= d
        sh_a = pltpu.roll(acc_a, d, 0)
        sh_h = pltpu.roll(acc_h, d, 0)
        acc_h = jnp.where(keep, acc_a * sh_h + acc_h, acc_h)
        acc_a = jnp.where(keep, acc_a * sh_a, acc_a)
        d *= 2
    return acc_a, acc_h


def _lru_carry(acc_a, acc_h, h_ref):
    rows = acc_a.shape[0]
    carry = h_ref[0:1, :]
    groups = []
    for g in range(rows // SUBLANES):
        hg = acc_h[g * SUBLANES:(g + 1) * SUBLANES] + acc_a[g * SUBLANES:(g + 1) * SUBLANES] * carry
        carry = hg[SUBLANES - 1:SUBLANES]
        groups.append(hg)
    h_ref[0:1, :] = carry
    return jnp.concatenate(groups, axis=0)


def _lru_out(h, z, g_ref):
    ms = jnp.mean(h * h, axis=-1, keepdims=True)
    return (h * lax.rsqrt(ms + EPS) * g_ref[...]) * (z * _sigmoid(z))


def _proj_kernel(cols, rows_t, x_ref, mod_ref, ng_ref, w_ref, wt_ref, gq_ref, gks_ref, gkw_ref,
                 cw_ref, cb_ref, wa_ref, ba_ref, wx_ref, bx_ref, lam_ref, gl_ref,
                 ylru_ref, kcmp_ref, vcmp_ref, kslc_ref, kwin_ref,
                 qt_ref, vst_ref, vwt_ref, gt_ref, znsat_ref, tail_ref, h_ref):
    rows = x_ref.shape[1]
    x = x_ref[0]
    ms = jnp.mean(x * x, axis=-1, keepdims=True)
    gain = ng_ref[...] * (1.0 + mod_ref[0, 1:2, :])
    h = (x * lax.rsqrt(ms + EPS)) * gain + mod_ref[0, 0:1, :]
    hb = h.astype(BF16)

    def mm(name):
        c0, n = cols[name]
        return jnp.dot(hb, w_ref[:, c0:c0 + n], preferred_element_type=F32)

    def mm_pair(first, second):
        (c0, n0), (c1, n1) = cols[first], cols[second]
        assert c1 == c0 + n0
        both = jnp.dot(hb, w_ref[:, c0:c1 + n1], preferred_element_type=F32)
        return both[:, 0:n0], both[:, n0:n0 + n1]

    def mmt(*names):
        spans = [rows_t[name] for name in names]
        r0 = spans[0][0]
        assert all(a[0] + a[1] == b[0] for a, b in zip(spans, spans[1:]))
        out = _nt_dot(wt_ref[r0:spans[-1][0] + spans[-1][1], :], hb)
        parts = tuple(out[r - r0:r - r0 + n] for r, n in spans)
        return parts[0] if len(parts) == 1 else parts

    def after(value, done):
        return value + jnp.minimum(jnp.abs(done[done.shape[0] - 1:, 0:1]), 0.0)

    lru_a, lru_u = _lru_gates(mm("xlru"), pl.program_id(1) == 0, cw_ref, cb_ref, wa_ref, ba_ref,
                              wx_ref, bx_ref, lam_ref, tail_ref, h_ref)
    z_lru = mm("zlru")
    kcmp_ref[0], vcmp_ref[0] = mm_pair("kcmp", "vcmp")

    pos = pl.program_id(1) * rows + lax.broadcasted_iota(jnp.int32, (rows, 1), 0)
    k_slc, k_win = mm_pair("kslc", "kwin")
    kslc_ref[0, :, 0:LANES] = _half_rms(k_slc, gks_ref[...]).astype(BF16)
    kslc_ref[0, :, LANES:2 * LANES] = _key_features(pos, True).astype(BF16)
    kwin_ref[0, :, 0:LANES] = _half_rms(k_win, gkw_ref[...]).astype(BF16)
    kwin_ref[0, :, LANES:2 * LANES] = _key_features(pos, True).astype(BF16)

    lru_a, lru_h = _lru_scan_groups(lru_a, after(lru_u, k_win))
    qt = mmt("q")
    for hd in range(NSA_HEADS):
        qh = qt[hd * HEAD_DIM:(hd + 1) * HEAD_DIM]
        r = lax.rsqrt(jnp.mean(qh * qh, axis=0, keepdims=True) + EPS)
        qt_ref[0, hd * HEAD_DIM:(hd + 1) * HEAD_DIM, :] = (qh * r * gq_ref[...]).astype(BF16)

    h_lru = _lru_carry(lru_a, after(lru_h, qt), h_ref)
    vst, vwt, gate_logits, z_nsa = mmt("vslc", "vwin", "gates", "znsa")
    vst, vwt = vst.astype(BF16), vwt.astype(BF16)
    for j in range(rows // KEY_CHUNK):
        vst_ref[0, j] = vst[:, j * KEY_CHUNK:(j + 1) * KEY_CHUNK]
        vwt_ref[0, j] = vwt[:, j * KEY_CHUNK:(j + 1) * KEY_CHUNK]
    gt_ref[0] = jax.nn.sigmoid(gate_logits)
    znsat_ref[0] = z_nsa.astype(BF16)
    ylru_ref[0] = _lru_out(after(h_lru, z_nsa), z_lru, gl_ref).astype(BF16)


def _proj(x, mod3, norm_g, w_big, cols, w_t, rows_t, gq_b, gks2, gkw2, lru_params):
    bsz, seq, d = x.shape
    ts = PROJ_ROWS
    nsa_w = NSA_HEADS * HEAD_DIM
    lru_w = cols["xlru"][1]
    n_gate = rows_t["gates"][1]

    def tok_spec(n):
        return pl.BlockSpec((1, ts, n), lambda b, s: (b, s, 0))

    def feat_spec(n):
        return pl.BlockSpec((1, n, ts), lambda b, s: (b, 0, s))

    def chunk_spec():
        return pl.BlockSpec((1, ts // KEY_CHUNK, LANES, KEY_CHUNK), lambda b, s: (b, s, 0, 0))

    def const_spec(shape):
        return pl.BlockSpec(shape, lambda b, s: (0,) * len(shape))

    out_shape = (
        jax.ShapeDtypeStruct((bsz, seq, lru_w), BF16),
        jax.ShapeDtypeStruct((bsz, seq, LANES), F32),
        jax.ShapeDtypeStruct((bsz, seq, LANES), F32),
        jax.ShapeDtypeStruct((bsz, seq, 2 * LANES), BF16),
        jax.ShapeDtypeStruct((bsz, seq, 2 * LANES), BF16),
        jax.ShapeDtypeStruct((bsz, nsa_w, seq), BF16),
        jax.ShapeDtypeStruct((bsz, seq // KEY_CHUNK, LANES, KEY_CHUNK), BF16),
        jax.ShapeDtypeStruct((bsz, seq // KEY_CHUNK, LANES, KEY_CHUNK), BF16),
        jax.ShapeDtypeStruct((bsz, n_gate, seq), F32),
        jax.ShapeDtypeStruct((bsz, nsa_w, seq), BF16),
    )
    out_specs = (tok_spec(lru_w), tok_spec(LANES), tok_spec(LANES),
                 tok_spec(2 * LANES), tok_spec(2 * LANES),
                 feat_spec(nsa_w), chunk_spec(), chunk_spec(), feat_spec(n_gate), feat_spec(nsa_w))
    return pl.pallas_call(
        functools.partial(_proj_kernel, cols, rows_t),
        out_shape=out_shape,
        grid=(bsz, seq // ts),
        in_specs=[tok_spec(d),
                  pl.BlockSpec((1, 3, d), lambda b, s: (b, 0, 0)),
                  const_spec((1, d)),
                  const_spec(w_big.shape), const_spec(w_t.shape),
                  const_spec((HEAD_DIM, ts)), const_spec((1, LANES)), const_spec((1, LANES))]
                 + [const_spec(p.shape) for p in lru_params],
        out_specs=out_specs,
        scratch_shapes=[pltpu.VMEM((SUBLANES, lru_w), F32), pltpu.VMEM((SUBLANES, lru_w), F32)],
        compiler_params=pltpu.CompilerParams(dimension_semantics=("parallel", "arbitrary"),
                                             vmem_limit_bytes=VMEM_LIMIT),
        name="in_proj_lru",
    )(x, mod3, norm_g, w_big, w_t, gq_b, gks2, gkw2, *lru_params)


def _compress_kernel(is_key, c_ref, w1x_ref, pos_ref, w1_ref, w2lo_ref, w2hi_ref, g_ref, o_ref, const_ref):
    nblk = c_ref.shape[1] // CMP_STRIDE
    hid = w1_ref.shape[1]

    @pl.when(pl.program_id(0) == 0)
    def _():
        const_ref[...] = jnp.dot(pos_ref[...], w1_ref[...], preferred_element_type=F32, precision=HIGHEST)

    ab = jnp.zeros((nblk, 4 * hid), F32)
    for j in range(0, CMP_STRIDE, 2):
        pair = jnp.concatenate([c_ref[0, pl.ds(j + i, nblk, stride=CMP_STRIDE), :].astype(BF16) for i in range(2)],
                               axis=1)
        ab = ab + jnp.dot(pair, w1x_ref[j * LANES:(j + 2) * LANES, :], preferred_element_type=F32)
    const = const_ref[0:1, :]
    acts = []
    for hk in range(NSA_KV_HEADS):
        first = ab[:, (2 * hk) * hid:(2 * hk + 1) * hid]
        second = ab[:, (2 * hk + 1) * hid:(2 * hk + 2) * hid]
        hidden = first + pltpu.roll(second, nblk - 1, 0) + const
        acts.append(jax.nn.gelu(hidden).astype(BF16))
    if is_key:
        out = (jnp.dot(acts[0], w2lo_ref[...], preferred_element_type=F32)
               + jnp.dot(acts[1], w2hi_ref[...], preferred_element_type=F32))
        o_ref[0, :, 0:LANES] = _half_rms(out, g_ref[...]).astype(BF16)
        cend = CMP_STRIDE * lax.broadcasted_iota(jnp.int32, (nblk, 1), 0) + (CMP_BLOCK - 1)
        o_ref[0, :, LANES:2 * LANES] = _key_features(cend, False).astype(BF16)
    else:
        out_t = _nt_dot(w2lo_ref[...], acts[0]) + _nt_dot(w2hi_ref[...], acts[1])
        o_ref[0] = out_t.astype(BF16)


def _compress(tokens, w1x, pos_flat, w1, w2lo, w2hi, gain2, is_key):
    bsz, seq, width = tokens.shape
    nblk = seq // CMP_STRIDE
    hid = w1.shape[1]

    def const_spec(shape):
        return pl.BlockSpec(shape, lambda b: (0,) * len(shape))

    out_tail = (nblk, 2 * LANES) if is_key else (LANES, nblk)
    return pl.pallas_call(
        functools.partial(_compress_kernel, is_key),
        out_shape=jax.ShapeDtypeStruct((bsz,) + out_tail, BF16),
        grid=(bsz,),
        in_specs=[pl.BlockSpec((1, seq, width), lambda b: (b, 0, 0)),
                  const_spec(w1x.shape), const_spec((8, w1.shape[0])), const_spec(w1.shape),
                  const_spec(w2lo.shape), const_spec(w2hi.shape), const_spec((1, LANES))],
        out_specs=pl.BlockSpec((1,) + out_tail, lambda b: (b, 0, 0)),
        scratch_shapes=[pltpu.VMEM((SUBLANES, hid), F32)],
        compiler_params=pltpu.CompilerParams(dimension_semantics=("arbitrary",), vmem_limit_bytes=VMEM_LIMIT),
        name="compress_k" if is_key else "compress_v",
    )(tokens, w1x, pos_flat, w1, w2lo, w2hi, gain2)


ONES_ROWS = BF16_SUBLANES


def _with_ones(vt):
    return jnp.concatenate([vt, jnp.ones((ONES_ROWS, vt.shape[1]), BF16)], axis=0)


def _online_update(carry, s, top, vt):
    m, acc = carry
    m_new = jnp.maximum(m, top)
    p = jnp.exp2(s - m_new)
    acc = jnp.exp2(m - m_new) * acc + jnp.dot(_with_ones(vt), p.astype(BF16), preferred_element_type=F32)
    return m_new, acc


def _normalised(acc):
    return acc[0:HEAD_DIM] * (1.0 / acc[HEAD_DIM:HEAD_DIM + 1])


def _attn_kernel(qt_ref, kc_ref, vct_ref, ks_ref, vst_ref, kw_ref, vwt_ref, gt_ref, ovt_ref,
                 o_ref, impt_ref, s_ref, sel_ref):
    tq = qt_ref.shape[2]
    n_cmp = kc_ref.shape[1]
    n_sel = impt_ref.shape[1]
    ncol = NSA_GROUP * tq
    per_wide = SLC_KEYS // KEY_CHUNK
    per_tile = tq // KEY_CHUNK
    qi = pl.program_id(1)
    t0 = qi * tq
    c0 = qi * per_tile
    tloc = lax.broadcasted_iota(jnp.int32, (1, tq), 1)
    tcol = jnp.concatenate([tloc] * NSA_GROUP, axis=1)
    tpos = t0 + tloc
    nrow = lax.broadcasted_iota(jnp.int32, (n_sel, 1), 0)
    cur = jnp.right_shift(tpos, SEL_SHIFT)
    forced = jnp.logical_or(jnp.logical_or(nrow == 0, nrow == cur), nrow == cur - 1)
    frow = lax.broadcasted_iota(jnp.int32, (HEAD_DIM, 1), 0)
    srow = lax.broadcasted_iota(jnp.int32, (SUBLANES, 1), 0)
    zeros_q = jnp.zeros((HEAD_DIM, tq), BF16)
    row_minus_col = lax.broadcasted_iota(jnp.int32, (tq, 1), 0) - tcol
    own_ok = row_minus_col <= 0

    cidx = lax.broadcasted_iota(jnp.int32, (n_cmp, 1), 0)
    cend = CMP_STRIDE * cidx + (CMP_BLOCK - 1)
    cmp_ok = jnp.logical_and(cend <= t0 + tcol, cidx < n_cmp - 1)
    col_has_cmp = (t0 + tcol) >= CMP_BLOCK - 1
    gates = gt_ref[0]
    kv_heads = range(NSA_KV_HEADS)

    def values(ref, hk, first_chunk, n):
        return jnp.concatenate([ref[0, first_chunk + i, hk * HEAD_DIM:(hk + 1) * HEAD_DIM, :]
                                for i in range(n)], axis=1)

    not_past = jnp.logical_and(frow >= FEAT_CHUNK - SEL_BLOCK, frow - (FEAT_CHUNK - SEL_BLOCK) >= c0)

    k_tile_win = kw_ref[0, pl.ds(pl.multiple_of(t0, KEY_CHUNK), tq), :]
    q_plain, q_plain_past, o_cmp, s_own_win = [], [], [], []
    for hk in kv_heads:
        q_rows, feat_rows, feat_rows_past = [], [], []
        for g in range(NSA_GROUP):
            hd = hk * NSA_GROUP + g
            qh = qt_ref[0, hd * HEAD_DIM:(hd + 1) * HEAD_DIM, :]
            q_rows.append([qh, zeros_q] if hk == 0 else [zeros_q, qh])
            c = _alibi_coef(hd)
            a = jnp.zeros((HEAD_DIM, 1), F32)
            for i, term in enumerate(_bf16_terms(c)):
                a = jnp.where(frow == FEAT_BLK - SEL_BLOCK + i, SEL_BLOCK * term, a)
                a = jnp.where(frow == FEAT_OFF - SEL_BLOCK + i, term, a)
            a = jnp.where(frow == FEAT_ONE - SEL_BLOCK, -c * t0.astype(F32), a)
            feat_rows.append(jnp.broadcast_to(a, (HEAD_DIM, tq)).astype(BF16))
            feat_rows_past.append(jnp.broadcast_to(jnp.where(not_past, NEG_INF, a), (HEAD_DIM, tq)).astype(BF16))

        def q_operand(pen, feats, q_rows=q_rows):
            return jnp.concatenate([jnp.concatenate(q_rows[g] + [pen, feats[g]], axis=0)
                                    for g in range(NSA_GROUP)], axis=1)

        q_plain.append(q_operand(zeros_q, feat_rows))
        q_plain_past.append(q_operand(zeros_q, feat_rows_past))
        sel_ref[0, hk] = q_plain[hk]
        sel_ref[1, hk] = q_plain_past[hk]

        s_both = jnp.dot(jnp.concatenate([kc_ref[0], k_tile_win], axis=0), q_plain[hk],
                         preferred_element_type=F32)
        s_own_win.append(s_both[n_cmp:])
        s = jnp.where(cmp_ok, s_both[0:n_cmp], NEG_INF)
        e = jnp.exp2(s - jnp.max(s, axis=0, keepdims=True))
        p = e * jnp.where(col_has_cmp, 1.0 / jnp.sum(e, axis=0, keepdims=True), 0.0)
        o_cmp.append(jnp.dot(vct_ref[0, hk * HEAD_DIM:(hk + 1) * HEAD_DIM, :], p.astype(BF16),
                             preferred_element_type=F32))
        p_sum = p[:, 0:tq]
        for g in range(1, NSA_GROUP):
            p_sum = p_sum + p[:, g * tq:(g + 1) * tq]
        imp = jnp.zeros((n_sel, tq), F32)
        rest = p_sum
        for _ in range(N_SPLIT):
            term = rest.astype(BF16)
            imp = imp + jnp.dot(ovt_ref[...], term, preferred_element_type=F32)
            rest = rest - term.astype(F32)
        imp = jnp.where(forced, FORCE, jnp.where(nrow > cur, -FORCE, imp))

        impt_ref[hk] = imp
        groups = [imp[b * SUBLANES:(b + 1) * SUBLANES] for b in range(n_sel // SUBLANES)]
        ranks = [jnp.zeros((SUBLANES, tq), F32) for _ in groups]
        for m in range(n_sel):
            other = impt_ref[hk, m:m + 1, :]
            for b, grp in enumerate(groups):
                if b * SUBLANES > m:
                    ahead = other >= grp
                elif (b + 1) * SUBLANES <= m:
                    ahead = other > grp
                else:
                    ahead = jnp.logical_or(other > grp, jnp.logical_and(other == grp, srow > m - b * SUBLANES))
                ranks[b] = ranks[b] + jnp.where(ahead, 1.0, 0.0)
        rank = jnp.concatenate(ranks, axis=0)
        pen = jnp.where(rank < float(SEL_TOPK), 0.0, NEG_INF).astype(BF16)
        pen_rows = jnp.concatenate([pen] * NSA_GROUP, axis=1)
        sel_ref[0, hk, 2 * HEAD_DIM:3 * HEAD_DIM, :] = pen_rows
        sel_ref[1, hk, 2 * HEAD_DIM:3 * HEAD_DIM, :] = pen_rows

    init = (jnp.full((1, ncol), NEG_INF, F32), jnp.zeros((HEAD_DIM + ONES_ROWS, ncol), F32))

    n_back = WINDOW // KEY_CHUNK
    wc0 = jnp.maximum(c0 - n_back, 0)
    lead = t0 - wc0 * KEY_CHUNK
    k_back = kw_ref[0, pl.ds(pl.multiple_of(wc0 * KEY_CHUNK, KEY_CHUNK), WINDOW), :]
    recent = row_minus_col > lead - WINDOW
    o_win = []
    for hk in kv_heads:
        s_back = jnp.dot(k_back, q_plain_past[hk], preferred_element_type=F32)
        s_own = jnp.where(own_ok, s_own_win[hk], NEG_INF)
        s = jnp.concatenate([jnp.where(recent, s_back[0:tq], NEG_INF), s_back[tq:], s_own], axis=0)
        vt = jnp.concatenate([values(vwt_ref, hk, wc0, n_back), values(vwt_ref, hk, c0, per_tile)], axis=1)
        _, acc = _online_update(init, s, jnp.max(s, axis=0, keepdims=True), vt)
        o_win.append(_normalised(acc))

    k_own = ks_ref[0, pl.ds(pl.multiple_of(t0, KEY_CHUNK), tq), :]
    stats = []
    for hk in kv_heads:
        s = jnp.where(own_ok, jnp.dot(k_own, sel_ref[0, hk], preferred_element_type=F32), NEG_INF)
        stats.append(_online_update(init, s, jnp.max(s, axis=0, keepdims=True), values(vst_ref, hk, c0, per_tile)))
    stats = tuple(stats)

    def produce(j, slot):
        k = ks_ref[0, j * SLC_KEYS:(j + 1) * SLC_KEYS, :]
        tops = []
        for hk in kv_heads:
            s = jnp.dot(k, sel_ref[1, hk], preferred_element_type=F32)
            s_ref[slot, hk] = s
            tops.append(jnp.max(s, axis=0, keepdims=True))
        return tuple(tops)

    def consume(j, slot, tops, stats):
        return tuple(_online_update(stats[hk], s_ref[slot, hk], tops[hk],
                                    values(vst_ref, hk, j * per_wide, per_wide)) for hk in kv_heads)

    def chain(n_blocks):
        def run(stats):
            tops = produce(0, 0) if n_blocks else None
            for j in range(n_blocks):
                nxt = produce(j + 1, (j + 1) % 2) if j + 1 < n_blocks else None
                stats = consume(j, j % 2, tops, stats)
                tops = nxt
            return stats
        return run

    stats = lax.switch((t0 + SLC_KEYS - 1) // SLC_KEYS,
                       [chain(n) for n in range(ks_ref.shape[1] // SLC_KEYS + 1)], stats)
    o_slc = [_normalised(acc) for _, acc in stats]

    for hk in kv_heads:
        for g in range(NSA_GROUP):
            hd = hk * NSA_GROUP + g
            tot = jnp.zeros((HEAD_DIM, tq), F32)
            for br, o_br in enumerate((o_cmp[hk], o_slc[hk], o_win[hk])):
                r = br * NSA_HEADS + hd
                tot = tot + gates[r:r + 1, :] * o_br[:, g * tq:(g + 1) * tq]
            o_ref[0, hd * HEAD_DIM:(hd + 1) * HEAD_DIM, :] = tot.astype(BF16)


def _attn(qt, kc, vct, kslc, vst, kwin, vwt, gt, overlap_t):
    bsz, nsa_w, seq = qt.shape
    tq = ATTN_ROWS
    n_cmp = kc.shape[1]
    n_sel = seq // SEL_BLOCK
    n_gate = gt.shape[1]
    assert seq % (2 * SLC_KEYS) == 0 and n_sel <= HALF and seq >= WINDOW + tq
    assert seq // KEY_CHUNK <= LANES - FEAT_CHUNK and n_sel % SUBLANES == 0

    def per_batch(shape):
        return pl.BlockSpec((1,) + shape, lambda b, i: (b,) + (0,) * len(shape))

    def const_spec(shape):
        return pl.BlockSpec(shape, lambda b, i: (0,) * len(shape))

    return pl.pallas_call(
        _attn_kernel,
        out_shape=jax.ShapeDtypeStruct((bsz, nsa_w, seq), BF16),
        grid=(bsz, seq // tq),
        in_specs=[pl.BlockSpec((1, nsa_w, tq), lambda b, i: (b, 0, i)),
                  per_batch((n_cmp, 2 * LANES)), per_batch((LANES, n_cmp)),
                  per_batch((seq, 2 * LANES)), per_batch((seq // KEY_CHUNK, LANES, KEY_CHUNK)),
                  per_batch((seq, 2 * LANES)), per_batch((seq // KEY_CHUNK, LANES, KEY_CHUNK)),
                  pl.BlockSpec((1, n_gate, tq), lambda b, i: (b, 0, i)),
                  const_spec(overlap_t.shape)],
        out_specs=pl.BlockSpec((1, nsa_w, tq), lambda b, i: (b, 0, i)),
        scratch_shapes=[pltpu.VMEM((NSA_KV_HEADS, n_sel, tq), F32),
                        pltpu.VMEM((2, NSA_KV_HEADS, SLC_KEYS, NSA_GROUP * tq), F32),
                        pltpu.VMEM((2, NSA_KV_HEADS, 2 * LANES, NSA_GROUP * tq), BF16)],
        compiler_params=pltpu.CompilerParams(dimension_semantics=("parallel", "arbitrary"),
                                             vmem_limit_bytes=VMEM_LIMIT),
        name="nsa_attn",
    )(qt, kc, vct, kslc, vst, kwin, vwt, gt, overlap_t)


def _out_kernel(x_ref, ylru_ref, onsat_ref, znsat_ref, gn_ref, gate_ref, w_ref, o_ref):
    o = onsat_ref[0].astype(F32)
    ms = jnp.mean(o * o, axis=0, keepdims=True)
    z = znsat_ref[0].astype(F32)
    y_t = (o * lax.rsqrt(ms + EPS) * gn_ref[...]) * (z * jax.nn.sigmoid(z))
    y_nsa = jnp.transpose(y_t.astype(BF16))
    out = jnp.dot(jnp.concatenate([ylru_ref[0], y_nsa], axis=1), w_ref[...], preferred_element_type=F32)
    o_ref[0] = x_ref[0] + gate_ref[0, 2:3, :] * out


def _out(x, ylru, onsat, znsat, g_nsa_b, mod3, w_out):
    bsz, seq, d = x.shape
    ts = OUT_ROWS
    nsa_w = onsat.shape[1]

    def tok_spec(n):
        return pl.BlockSpec((1, ts, n), lambda b, s: (b, s, 0))

    def feat_spec(n):
        return pl.BlockSpec((1, n, ts), lambda b, s: (b, 0, s))

    def const_spec(shape):
        return pl.BlockSpec(shape, lambda b, s: (0,) * len(shape))

    return pl.pallas_call(
        _out_kernel,
        out_shape=jax.ShapeDtypeStruct((bsz, seq, d), F32),
        grid=(bsz, seq // ts),
        in_specs=[tok_spec(d), tok_spec(ylru.shape[2]), feat_spec(nsa_w), feat_spec(nsa_w),
                  const_spec((nsa_w, ts)),
                  pl.BlockSpec((1, 3, d), lambda b, s: (b, 0, 0)),
                  const_spec(w_out.shape)],
        out_specs=tok_spec(d),
        compiler_params=pltpu.CompilerParams(dimension_semantics=("parallel", "parallel"),
                                             vmem_limit_bytes=VMEM_LIMIT),
        name="out_proj",
    )(x, ylru, onsat, znsat, g_nsa_b, mod3, w_out)


def _overlap_t(n_cmp_pad, n_sel):
    ratio = SEL_BLOCK // CMP_STRIDE
    ov = np.zeros((n_sel, n_cmp_pad), np.float32)
    for c in range(n_cmp_pad - 1):
        for n in (c // ratio, (c + 1) // ratio):
            if n < n_sel:
                ov[n, c] += 1.0
    return ov


def _block_diag_pairs(w):
    nb, bs, _ = w.shape
    z = jnp.zeros((bs, bs), w.dtype)
    return jnp.stack([jnp.block([[w[2 * s], z], [z, w[2 * s + 1]]]) for s in range(nb // 2)])


def _compress_weights(w1, w2, transposed):
    hid = w1.shape[1]
    w1r = w1.reshape(2, CMP_STRIDE, HEAD_DIM, hid)
    parts = []
    for hk in range(NSA_KV_HEADS):
        for half in range(2):
            slot = jnp.zeros((CMP_STRIDE, NSA_KV_HEADS, HEAD_DIM, hid), w1.dtype).at[:, hk].set(w1r[half])
            parts.append(slot.reshape(CMP_STRIDE * LANES, hid))
    w1x = jnp.concatenate(parts, axis=1).astype(BF16)
    zpad = jnp.zeros_like(w2)
    w2lo = jnp.concatenate([w2, zpad], axis=1).astype(BF16)
    w2hi = jnp.concatenate([zpad, w2], axis=1).astype(BF16)
    if transposed:
        w2lo, w2hi = w2lo.T, w2hi.T
    return w1x, w2lo, w2hi


def _pad_to(n, m):
    return -(-n // m) * m


def _layer(x, c, w_ada, b_ada, norm_g, w_in, conv_w, conv_b, w_rg_a, b_rg_a, w_rg_x, b_rg_x, lru_lambda,
           cmp_pos_k, cmp_w1_k, cmp_w2_k, cmp_pos_v, cmp_w1_v, cmp_w2_v, g_q, g_k_cmp, g_k_slc, g_k_win,
           g_out_lru, g_out_nsa, w_out):
    bsz, seq, d = x.shape
    lru_w = d // 2
    nsa_w = NSA_HEADS * HEAD_DIM
    kv_w = NSA_KV_HEADS * HEAD_DIM
    n_sel = seq // SEL_BLOCK

    splits = (lru_w, lru_w, nsa_w, kv_w, kv_w, kv_w, kv_w, kv_w, kv_w, N_BRANCH * NSA_HEADS, nsa_w)
    offs = np.concatenate([[0], np.cumsum(splits)])
    names = ("xlru", "zlru", "q", "kcmp", "vcmp", "kslc", "vslc", "kwin", "vwin", "gates", "znsa")
    src = {n: (int(offs[i]), int(offs[i + 1])) for i, n in enumerate(names)}

    def plan(group, align):
        pieces, where, at = [], {}, 0
        for n in group:
            piece = w_in[:, src[n][0]:src[n][1]]
            width = _pad_to(piece.shape[1], align)
            if width != piece.shape[1]:
                piece = jnp.pad(piece, ((0, 0), (0, width - piece.shape[1])))
            pieces.append(piece)
            where[n] = (at, width)
            at += width
        return jnp.concatenate(pieces, axis=1).astype(BF16), where

    w_big, cols = plan(("xlru", "zlru", "kcmp", "vcmp", "kslc", "kwin"), LANES)
    w_tt, rows_t = plan(("q", "vslc", "vwin", "gates", "znsa"), BF16_SUBLANES)
    w_t = w_tt.T

    def dup(g):
        return jnp.concatenate([g, g]).reshape(1, LANES)

    gq_b = jnp.broadcast_to((g_q * (ATTN_SCALE * LOG2E)).reshape(HEAD_DIM, 1), (HEAD_DIM, PROJ_ROWS))
    mod3 = _mod(c, w_ada, b_ada).reshape(bsz, 3, d)
    lru_params = (conv_w, conv_b.reshape(1, lru_w),
                  _block_diag_pairs(w_rg_a).astype(BF16), b_rg_a.reshape(1, lru_w),
                  _block_diag_pairs(w_rg_x).astype(BF16), b_rg_x.reshape(1, lru_w),
                  lru_lambda.reshape(1, lru_w), g_out_lru.reshape(1, lru_w))
    (ylru, kcmp, vcmp, kslc, kwin, qt, vst, vwt, gt, znsat) = _proj(
        x, mod3, norm_g.reshape(1, d), w_big, cols, w_t, rows_t, gq_b, dup(g_k_slc), dup(g_k_win), lru_params)

    n_chunk = seq // CMP_STRIDE
    w1x_k, w2lo_k, w2hi_k = _compress_weights(cmp_w1_k, cmp_w2_k, False)
    w1x_v, w2lo_v, w2hi_v = _compress_weights(cmp_w1_v, cmp_w2_v, True)

    def pos_rows(pos):
        return jnp.broadcast_to(pos.reshape(1, CMP_BLOCK * HEAD_DIM), (8, CMP_BLOCK * HEAD_DIM))

    kc = _compress(kcmp, w1x_k, pos_rows(cmp_pos_k), cmp_w1_k,
                   w2lo_k, w2hi_k, dup(g_k_cmp), True)
    vct = _compress(vcmp, w1x_v, pos_rows(cmp_pos_v), cmp_w1_v,
                    w2lo_v, w2hi_v, dup(g_k_cmp), False)

    onsat = _attn(qt, kc, vct, kslc, vst, kwin, vwt, gt, jnp.asarray(_overlap_t(n_chunk, n_sel), dtype=BF16))

    g_nsa_b = jnp.broadcast_to(g_out_nsa.reshape(nsa_w, 1), (nsa_w, OUT_ROWS))
    return _out(x, ylru, onsat, znsat, g_nsa_b, mod3, w_out.astype(BF16))


def kernel(x, c, w_ada, b_ada, norm_g, w_in, conv_w, conv_b, w_rg_a, b_rg_a, w_rg_x, b_rg_x, lru_lambda, cmp_pos_k, cmp_w1_k, cmp_w2_k, cmp_pos_v, cmp_w1_v, cmp_w2_v, g_q, g_k_cmp, g_k_slc, g_k_win, g_out_lru, g_out_nsa, w_out):
    params = (w_ada, b_ada, norm_g, w_in, conv_w, conv_b, w_rg_a, b_rg_a, w_rg_x, b_rg_x, lru_lambda,
              cmp_pos_k, cmp_w1_k, cmp_w2_k, cmp_pos_v, cmp_w1_v, cmp_w2_v, g_q, g_k_cmp, g_k_slc, g_k_win,
              g_out_lru, g_out_nsa, w_out)
    for layer in range(w_in.shape[0]):
        x = _layer(x, c, *(p[layer] for p in params))
    return x
```
